```python
import jax, jax.numpy as jnp
from jax import lax
import numpy as np

D_MODEL = 1024
BATCH = 8
SEQ = 2048
DEPTH = 4
DEC_BATCH = 128
DEC_SEQ = 1
PAST_LEN = 16384
PAGE_SIZE = 128

SSM_HEAD_DIM = 64
SSM_INNER = D_MODEL
SSM_HEADS = SSM_INNER // SSM_HEAD_DIM
SSM_GROUPS = 2
SSM_STATE = 128
CONV_WIDTH = 4
CONV_DIM = SSM_INNER + 2 * SSM_GROUPS * SSM_STATE
SSD_CHUNK = 128
CMLP_WIDTH = D_MODEL
CMLP_GROUPS = 8
CMLP_GROUP_DIM = CMLP_WIDTH // CMLP_GROUPS
CMLP_CHUNK = 128
FFN_DIM = 2816
N_EXPERTS = 8
TOP_K = 2
EPS = 1e-6
IN_SIZES = (SSM_INNER, CONV_DIM, SSM_HEADS, CMLP_WIDTH, CMLP_WIDTH, D_MODEL, D_MODEL)
IN_DIM = SSM_INNER + CONV_DIM + SSM_HEADS + 2 * CMLP_WIDTH + 2 * D_MODEL

kernel_name = 'hybrid_ssd_chunkmlp_moe_step'


def split_cols(a, sizes):
    idx, acc = [], 0
    for s in sizes[:-1]:
        acc += s
        idx.append(acc)
    return jnp.split(a, idx, axis=-1)


def rmsnorm(x, g):
    xf = x.astype(jnp.float32)
    y = xf * lax.rsqrt(jnp.mean(xf * xf, axis=-1, keepdims=True) + EPS)
    return (y * g.astype(jnp.float32)).astype(x.dtype)


def layernorm(x, g, b):
    xf = x.astype(jnp.float32)
    mu = jnp.mean(xf, axis=-1, keepdims=True)
    xc = xf - mu
    y = xc * lax.rsqrt(jnp.mean(xc * xc, axis=-1, keepdims=True) + EPS)
    return (y * g.astype(jnp.float32) + b.astype(jnp.float32)).astype(x.dtype)


def causal_dwconv(x, conv_state, w, b):
    L = x.shape[1]
    xp = jnp.concatenate([conv_state.astype(x.dtype), x], axis=1)
    y = b + sum(xp[:, k:k + L] * w[k] for k in range(CONV_WIDTH))
    return y, xp[:, L:]


def ssd_scan(x, dt, a, bm, cm, h0):
    Bsz, L = x.shape[:2]
    G, N, P = SSM_GROUPS, SSM_STATE, SSM_HEAD_DIM
    R = SSM_HEADS // G
    Q = SSD_CHUNK if L % SSD_CHUNK == 0 else L
    nc = L // Q
    f32 = jnp.float32
    xs = x.astype(f32).reshape(Bsz, nc, Q, G, R, P).transpose(1, 0, 2, 3, 4, 5)
    dts = dt.astype(f32).reshape(Bsz, nc, Q, G, R).transpose(1, 0, 2, 3, 4)
    bs = bm.astype(f32).reshape(Bsz, nc, Q, G, N).transpose(1, 0, 2, 3, 4)
    cs = cm.astype(f32).reshape(Bsz, nc, Q, G, N).transpose(1, 0, 2, 3, 4)
    ar = a.astype(f32).reshape(G, R)
    causal = jnp.tril(jnp.ones((Q, Q), bool))[None, :, :, None, None]

    def step(h, inp):
        xc, dtc, bc, cc = inp
        cum = jnp.cumsum(dtc * ar, axis=1)
        seg = cum[:, :, None] - cum[:, None, :]
        decay = jnp.exp(jnp.where(causal, seg, -jnp.inf))
        cb = jnp.einsum('btgn,bsgn->btsg', cc, bc)
        mix = cb[..., None] * decay * dtc[:, None]
        y_in = jnp.einsum('btsgr,bsgrp->btgrp', mix, xc)
        y_st = jnp.einsum('btgn,bgrpn->btgrp', cc, h) * jnp.exp(cum)[..., None]
        last = cum[:, -1]
        w_end = jnp.exp(last[:, None] - cum) * dtc
        h_new = h * jnp.exp(last)[..., None, None] + jnp.einsum('bsgn,bsgr,bsgrp->bgrpn', bc, w_end, xc)
        return h_new, y_in + y_st

    h0g = h0.astype(f32).reshape(Bsz, G, R, P, N)
    hT, ys = lax.scan(step, h0g, (xs, dts, bs, cs))
    y = ys.transpose(1, 0, 2, 3, 4, 5).reshape(Bsz, L, SSM_HEADS, P)
    return y, hT.reshape(Bsz, SSM_HEADS, P, N)


def ssd_branch(z, xbc, dt_raw, conv_state, h0, conv_w, conv_b, dt_bias, a_log, d_skip, norm_g):
    Bsz, L = z.shape[:2]
    xbc_c, conv_new = causal_dwconv(xbc, conv_state, conv_w, conv_b)
    xbc_c = jax.nn.silu(xbc_c)
    xs, bm, cm = jnp.split(xbc_c, [SSM_INNER, SSM_INNER + SSM_GROUPS * SSM_STATE], axis=-1)
    xs = xs.reshape(Bsz, L, SSM_HEADS, SSM_HEAD_DIM)
    bm = bm.reshape(Bsz, L, SSM_GROUPS, SSM_STATE)
    cm = cm.reshape(Bsz, L, SSM_GROUPS, SSM_STATE)
    dt = jax.nn.softplus(dt_raw.astype(jnp.float32) + dt_bias.astype(jnp.float32))
    a = -jnp.exp(a_log.astype(jnp.float32))
    y, h_new = ssd_scan(xs, dt, a, bm, cm, h0)
    y = y + d_skip.astype(jnp.float32)[:, None] * xs.astype(jnp.float32)
    y = y.reshape(Bsz, L, SSM_INNER) * jax.nn.silu(z.astype(jnp.float32))
    yg = y.reshape(Bsz, L, SSM_GROUPS, SSM_INNER // SSM_GROUPS)
    yg = yg * lax.rsqrt(jnp.mean(yg * yg, axis=-1, keepdims=True) + EPS)
    y = yg.reshape(Bsz, L, SSM_INNER) * norm_g.astype(jnp.float32)
    return y.astype(z.dtype), h_new.astype(h0.dtype), conv_new.astype(conv_state.dtype)


def chunk_mlp_branch(u, v, ln_g, ln_b, w_sp, b_sp):
    Bsz, L = u.shape[:2]
    v = layernorm(v, ln_g, ln_b)
    Q = CMLP_CHUNK if L % CMLP_CHUNK == 0 else L
    nc = L // Q
    w = jnp.where(jnp.tril(jnp.ones((Q, Q), bool))[None], w_sp[:, :Q, :Q], 0)
    vc = v.reshape(Bsz, nc, Q, CMLP_GROUPS, CMLP_GROUP_DIM)
    s = jnp.einsum('gts,bcsgd->bctgd', w, vc) + b_sp[:, :Q].T[None, None, :, :, None]
    return u * s.reshape(Bsz, L, CMLP_WIDTH), v


def swiglu(h, wg, wu, wd):
    return (jax.nn.silu(h @ wg) * (h @ wu)) @ wd


def moe_swiglu(h, w_router, b_router, wg, wu, wd):
    logits = (h @ w_router + b_router).astype(jnp.float32)
    top_val, top_idx = lax.top_k(logits, TOP_K)
    probs = jax.nn.softmax(top_val, axis=-1)
    combine = jnp.sum(jax.nn.one_hot(top_idx, N_EXPERTS, dtype=jnp.float32) * probs[..., None], axis=-2)
    combine = combine.astype(h.dtype)
    out = jnp.zeros_like(h)
    for e in range(N_EXPERTS):
        out = out + combine[..., e:e + 1] * swiglu(h, wg[e], wu[e], wd[e])
    return out


def run_group(x, c, ssm0, conv0, W):
    ssm_new, conv_new, v_rows = [], [], []
    for l in range(DEPTH):
        mod = jax.nn.silu(c) @ W['w_mod'][l] + W['b_mod'][l]
        sh1, sc1, g1, sh2, sc2, g2 = [t[:, None, :] for t in jnp.split(mod, 6, axis=-1)]
        h = rmsnorm(x, W['mix_pre_g'][l]) * (1 + sc1) + sh1
        z, xbc, dt_raw, u, v, ga, gb = split_cols(h @ W['w_in'][l], IN_SIZES)
        ya, hs, cs = ssd_branch(z, xbc, dt_raw, conv0[l], ssm0[l], W['conv_w'][l], W['conv_b'][l],
                                W['dt_bias'][l], W['a_log'][l], W['d_skip'][l], W['ssm_norm_g'][l])
        yb, vr = chunk_mlp_branch(jax.nn.gelu(u), jax.nn.gelu(v), W['cmlp_ln_g'][l], W['cmlp_ln_b'][l],
                                  W['w_spatial'][l], W['b_spatial'][l])
        merged = jax.nn.sigmoid(ga) * (ya @ W['w_ssd_out'][l]) + jax.nn.sigmoid(gb) * (yb @ W['w_cmlp_out'][l])
        x = x + g1 * rmsnorm(merged @ W['w_o'][l], W['mix_post_g'][l])
        h = rmsnorm(x, W['ffn_pre_g'][l]) * (1 + sc2) + sh2
        j = l // 2
        if l % 2 == 0:
            f = swiglu(h, W['ffn_wg'][j], W['ffn_wu'][j], W['ffn_wd'][j])
        else:
            f = moe_swiglu(h, W['router_w'][j], W['router_b'][j], W['exp_wg'][j], W['exp_wu'][j], W['exp_wd'][j])
        x = x + g2 * rmsnorm(f, W['ffn_post_g'][l])
        ssm_new.append(hs)
        conv_new.append(cs)
        v_rows.append(vr)
    return x, jnp.stack(ssm_new), jnp.stack(conv_new), jnp.stack(v_rows)


def setup_inputs(seed: int = 0) -> dict:
    key = jax.random.key(seed)
    keys = iter(jax.random.split(key, 48))
    f32 = jnp.float32

    def nrm(shape, scale):
        return jax.random.normal(next(keys), shape, f32) * scale

    def gain(shape):
        return 1.0 + nrm(shape, 0.1)

    n_dense = (DEPTH + 1) // 2
    n_moe = DEPTH // 2
    d = D_MODEL
    inp = {}
    inp['x_prompt'] = nrm((BATCH, SEQ, d), 1.0)
    inp['x_sample'] = nrm((DEC_BATCH, DEC_SEQ, d), 1.0)
    inp['state_ssm'] = nrm((DEPTH, DEC_BATCH, SSM_HEADS, SSM_HEAD_DIM, SSM_STATE), 0.1)
    inp['state_conv'] = nrm((DEPTH, DEC_BATCH, CONV_WIDTH - 1, CONV_DIM), 1.0)
    inp['c_prompt'] = nrm((BATCH, d), 1.0)
    inp['c_sample'] = nrm((DEC_BATCH, d), 1.0)
    inp['w_mod'] = nrm((DEPTH, d, 6 * d), 0.5 * d ** -0.5)
    inp['b_mod'] = nrm((DEPTH, 6 * d), 0.02)
    inp['mix_pre_g'] = gain((DEPTH, d))
    inp['mix_post_g'] = gain((DEPTH, d))
    inp['ffn_pre_g'] = gain((DEPTH, d))
    inp['ffn_post_g'] = gain((DEPTH, d))
    inp['w_in'] = nrm((DEPTH, d, IN_DIM), d ** -0.5)
    inp['conv_w'] = nrm((DEPTH, CONV_WIDTH, CONV_DIM), CONV_WIDTH ** -0.5)
    inp['conv_b'] = nrm((DEPTH, CONV_DIM), 0.02)
    dt0 = jnp.exp(jax.random.uniform(next(keys), (DEPTH, SSM_HEADS), f32, np.log(1e-3), np.log(1e-1)))
    inp['dt_bias'] = dt0 + jnp.log(-jnp.expm1(-dt0))
    inp['a_log'] = jnp.log(jax.random.uniform(next(keys), (DEPTH, SSM_HEADS), f32, 1.0, 16.0))
    inp['d_skip'] = gain((DEPTH, SSM_HEADS))
    inp['ssm_norm_g'] = gain((DEPTH, SSM_INNER))
    inp['w_ssd_out'] = nrm((DEPTH, SSM_INNER, d), SSM_INNER ** -0.5)
    inp['cmlp_ln_g'] = gain((DEPTH, CMLP_WIDTH))
    inp['cmlp_ln_b'] = nrm((DEPTH, CMLP_WIDTH), 0.02)
    inp['w_spatial'] = nrm((DEPTH, CMLP_GROUPS, CMLP_CHUNK, CMLP_CHUNK), CMLP_CHUNK ** -0.5)
    inp['b_spatial'] = gain((DEPTH, CMLP_GROUPS, CMLP_CHUNK))
    inp['w_cmlp_out'] = nrm((DEPTH, CMLP_WIDTH, d), CMLP_WIDTH ** -0.5)
    inp['w_o'] = nrm((DEPTH, d, d), d ** -0.5)
    inp['ffn_wg'] = nrm((n_dense, d, FFN_DIM), d ** -0.5)
    inp['ffn_wu'] = nrm((n_dense, d, FFN_DIM), d ** -0.5)
    inp['ffn_wd'] = nrm((n_dense, FFN_DIM, d), FFN_DIM ** -0.5)
    inp['router_w'] = nrm((n_moe, d, N_EXPERTS), d ** -0.5)
    inp['router_b'] = nrm((n_moe, N_EXPERTS), 0.01)
    inp['exp_wg'] = nrm((n_moe, N_EXPERTS, d, FFN_DIM), d ** -0.5)
    inp['exp_wu'] = nrm((n_moe, N_EXPERTS, d, FFN_DIM), d ** -0.5)
    inp['exp_wd'] = nrm((n_moe, N_EXPERTS, FFN_DIM, d), FFN_DIM ** -0.5)
    return inp


def reference(x_prompt, x_sample, state_ssm, state_conv, c_prompt, c_sample, w_mod, b_mod,
              mix_pre_g, mix_post_g, ffn_pre_g, ffn_post_g, w_in, conv_w, conv_b, dt_bias, a_log,
              d_skip, ssm_norm_g, w_ssd_out, cmlp_ln_g, cmlp_ln_b, w_spatial, b_spatial, w_cmlp_out,
              w_o, ffn_wg, ffn_wu, ffn_wd, router_w, router_b, exp_wg, exp_wu, exp_wd):
    W = dict(w_mod=w_mod, b_mod=b_mod, mix_pre_g=mix_pre_g, mix_post_g=mix_post_g,
             ffn_pre_g=ffn_pre_g, ffn_post_g=ffn_post_g, w_in=w_in, conv_w=conv_w, conv_b=conv_b,
             dt_bias=dt_bias, a_log=a_log, d_skip=d_skip, ssm_norm_g=ssm_norm_g, w_ssd_out=w_ssd_out,
             cmlp_ln_g=cmlp_ln_g, cmlp_ln_b=cmlp_ln_b, w_spatial=w_spatial, b_spatial=b_spatial,
             w_cmlp_out=w_cmlp_out, w_o=w_o, ffn_wg=ffn_wg, ffn_wu=ffn_wu, ffn_wd=ffn_wd,
             router_w=router_w, router_b=router_b, exp_wg=exp_wg, exp_wu=exp_wu, exp_wd=exp_wd)
    nb = x_prompt.shape[0]
    ssm0 = jnp.zeros((DEPTH, nb, SSM_HEADS, SSM_HEAD_DIM, SSM_STATE), x_prompt.dtype)
    conv0 = jnp.zeros((DEPTH, nb, CONV_WIDTH - 1, CONV_DIM), x_prompt.dtype)
    y_prompt, ssm_p, conv_p, _ = run_group(x_prompt, c_prompt, ssm0, conv0, W)
    y_sample, ssm_s, conv_s, v_s = run_group(x_sample, c_sample, state_ssm, state_conv, W)
    return (y_prompt, y_sample, ssm_p, conv_p, ssm_s, conv_s, v_s)
```

```python
import functools

import jax
import jax.numpy as jnp
from jax import lax
from jax.experimental import pallas as pl
from jax.experimental.pallas import tpu as pltpu

F32 = jnp.float32
BF16 = jnp.bfloat16

D_MODEL = 1024
DEPTH = 4
SSM_HEADS = 16
SSM_HEAD_DIM = 64
SSM_GROUPS = 2
SSM_STATE = 128
SSM_INNER = 1024
GROUP_WIDTH = SSM_INNER // SSM_GROUPS
HEADS_PER_GROUP = SSM_HEADS // SSM_GROUPS
CONV_WIDTH = 4
CONV_DIM = 1536
CHUNK = 128
CMLP_GROUPS = 8
CMLP_GROUP_DIM = 128
FFN_DIM = 2816
N_EXPERTS = 8
EPS = 1e-6
LANES = 128
COL_Z, COL_U, COL_V, COL_GA, COL_GB, COL_XS = 0, 1, 2, 3, 4, 5
COL_BC_512 = 12
PROJ_MAIN = 6 * D_MODEL + 2 * SSM_GROUPS * SSM_STATE
PROJ_TILE = 512
MOD_SH1, MOD_SC1, MOD_G1, MOD_SH2, MOD_SC2, MOD_G2 = range(6)
VMEM_LIMIT = 56 * 1024 * 1024


def _params(semantics):
    return pltpu.CompilerParams(dimension_semantics=semantics, vmem_limit_bytes=VMEM_LIMIT)


def _rms(x):
    return x * lax.rsqrt(jnp.mean(x * x, axis=-1, keepdims=True) + EPS)


def _dot(a, b):
    return jnp.dot(a, b, preferred_element_type=F32)


def _dot_nt(a, b):
    return lax.dot_general(a, b, (((1,), (1,)), ((), ())), preferred_element_type=F32)


def _dot_tn(a, b):
    return lax.dot_general(a, b, (((0,), (0,)), ((), ())), preferred_element_type=F32)


def _mod_kernel(c_ref, w_ref, b_ref, o_ref):
    a = jax.nn.silu(c_ref[...]).astype(BF16)
    o_ref[0] = _dot(a, w_ref[0].astype(BF16)) + b_ref[0]


def _mod_call(c_all, w_mod, b_mod):
    rows = c_all.shape[0]
    return pl.pallas_call(
        _mod_kernel,
        grid=(DEPTH, 6),
        in_specs=[
            pl.BlockSpec((rows, D_MODEL), lambda l, j: (0, 0)),
            pl.BlockSpec((1, D_MODEL, D_MODEL), lambda l, j: (l, 0, j)),
            pl.BlockSpec((1, 1, D_MODEL), lambda l, j: (l, 0, j)),
        ],
        out_specs=pl.BlockSpec((1, rows, D_MODEL), lambda l, j: (l, 0, j)),
        out_shape=jax.ShapeDtypeStruct((DEPTH, rows, 6 * D_MODEL), F32),
        compiler_params=_params(("arbitrary", "arbitrary")),
        name="adaln_mod",
    )(c_all, w_mod, b_mod.reshape(DEPTH, 1, 6 * D_MODEL))


def _mod_spec(mod, tm, seg, grid_rank):
    per_row = mod.shape[1] != 1
    rows = tm if per_row else 1
    if grid_rank == 2:
        return pl.BlockSpec((1, rows, D_MODEL), lambda b, i: (b, i if per_row else 0, seg))
    return pl.BlockSpec((1, rows, D_MODEL), lambda b, i, j: (b, i if per_row else 0, seg))


def _inproj_kernel(x_ref, g_ref, sh_ref, sc_ref, w_ref, wdt_ref, p_ref, pdt_ref, h_scr):
    @pl.when(pl.program_id(2) == 0)
    def _():
        h = _rms(x_ref[0]) * g_ref[...] * (1.0 + sc_ref[0]) + sh_ref[0]
        hb = h.astype(BF16)
        h_scr[...] = hb
        pdt_ref[0] = _dot(hb, wdt_ref[...])

    p_ref[0] = _dot(h_scr[...], w_ref[...])


def _inproj_call(x, mod, g, w_main, w_dt, tm):
    nb, seq, _ = x.shape
    grid = (nb, seq // tm, PROJ_MAIN // PROJ_TILE)
    return pl.pallas_call(
        _inproj_kernel,
        grid=grid,
        in_specs=[
            pl.BlockSpec((1, tm, D_MODEL), lambda b, i, j: (b, i, 0)),
            pl.BlockSpec((1, D_MODEL), lambda b, i, j: (0, 0)),
            _mod_spec(mod, tm, MOD_SH1, 3),
            _mod_spec(mod, tm, MOD_SC1, 3),
            pl.BlockSpec((D_MODEL, PROJ_TILE), lambda b, i, j: (0, j)),
            pl.BlockSpec((D_MODEL, LANES), lambda b, i, j: (0, 0)),
        ],
        out_specs=[
            pl.BlockSpec((1, tm, PROJ_TILE), lambda b, i, j: (b, i, j)),
            pl.BlockSpec((1, tm, LANES), lambda b, i, j: (b, i, 0)),
        ],
        out_shape=[
            jax.ShapeDtypeStruct((nb, seq, PROJ_MAIN), F32),
            jax.ShapeDtypeStruct((nb, seq, LANES), F32),
        ],
        scratch_shapes=[pltpu.VMEM((tm, D_MODEL), BF16)],
        compiler_params=_params(("arbitrary", "arbitrary", "arbitrary")),
        name="in_proj",
    )(x, g, mod, mod, w_main, w_dt)


def _gated_group_norm(y, z, norm_g):
    y = y * jax.nn.silu(z)
    parts = [_rms(y[:, g * GROUP_WIDTH:(g + 1) * GROUP_WIDTH]) for g in range(SSM_GROUPS)]
    return jnp.concatenate(parts, axis=-1) * norm_g


def _layernorm(x, g, b):
    mu = jnp.mean(x, axis=-1, keepdims=True)
    xc = x - mu
    return xc * lax.rsqrt(jnp.mean(xc * xc, axis=-1, keepdims=True) + EPS) * g + b


def _pair_columns(v, pair, lane_lo):
    h0 = 2 * pair
    return jnp.where(lane_lo, v[:, h0:h0 + 1], v[:, h0 + 1:h0 + 2])


def _mixer_kernel(z_ref, u_ref, v_ref, xs_ref, bc_ref, dt_ref,
                  cw_ref, cb_ref, dtb_ref, alog_ref, dskip_ref, ng_ref,
                  lng_ref, lnb_ref, wsp_ref, bsp_ref,
                  ya_ref, yb_ref, ssm_ref, conv_ref,
                  h_scr, cbuf, wsp_scr):
    c = pl.program_id(1)
    row = lax.broadcasted_iota(jnp.int32, (CHUNK, CHUNK), 0)
    col = lax.broadcasted_iota(jnp.int32, (CHUNK, CHUNK), 1)
    causal = row >= col
    lane_lo = col < SSM_HEAD_DIM

    @pl.when(c == 0)
    def _():
        h_scr[...] = jnp.zeros_like(h_scr)
        cbuf[0:8, :] = jnp.zeros((8, CONV_DIM), F32)
        for g in range(CMLP_GROUPS):
            wsp_scr[g] = jnp.where(causal, wsp_ref[g], 0.0).astype(BF16)

    cbuf[8:8 + CHUNK, 0:SSM_INNER] = xs_ref[0]
    cbuf[8:8 + CHUNK, SSM_INNER:CONV_DIM] = bc_ref[0]
    acc = cbuf[5:5 + CHUNK, :] * cw_ref[0:1, :]
    for k in range(1, CONV_WIDTH):
        acc = acc + cbuf[5 + k:5 + k + CHUNK, :] * cw_ref[k:k + 1, :]
    tail = cbuf[5 + CHUNK:8 + CHUNK, :]
    cbuf[5:8, :] = tail

    @pl.when(c == pl.num_programs(1) - 1)
    def _():
        conv_ref[0] = tail

    xbc = jax.nn.silu(acc + cb_ref[...])
    xc = xbc[:, 0:SSM_INNER]
    bm = xbc[:, SSM_INNER:SSM_INNER + SSM_GROUPS * SSM_STATE].astype(BF16)
    cm = xbc[:, SSM_INNER + SSM_GROUPS * SSM_STATE:CONV_DIM].astype(BF16)

    dt = jax.nn.softplus(dt_ref[0] + dtb_ref[...])
    da = dt * (-jnp.exp(alog_ref[...]))
    cum = jnp.dot(causal.astype(F32), da, preferred_element_type=F32,
                  precision=lax.Precision.HIGHEST)
    last = cum[CHUNK - 1:CHUNK, :]
    ecum = jnp.exp(cum)
    wend = jnp.exp(last - cum) * dt
    elast = jnp.exp(last)
    cum_t = cum.T
    dt_t = dt.T

    y_pairs = []
    for g in range(SSM_GROUPS):
        bg = bm[:, g * SSM_STATE:(g + 1) * SSM_STATE]
        cg = cm[:, g * SSM_STATE:(g + 1) * SSM_STATE]
        cb = _dot_nt(cg, bg)
        hg = h_scr[g * GROUP_WIDTH:(g + 1) * GROUP_WIDTH, :]
        y_state = _dot_nt(cg, hg.astype(BF16))
        xw_parts = []
        for q in range(HEADS_PER_GROUP // 2):
            pair = g * (HEADS_PER_GROUP // 2) + q
            mixes = []
            for h in (2 * pair, 2 * pair + 1):
                seg = cum[:, h:h + 1] - cum_t[h:h + 1, :]
                decay = jnp.where(causal, jnp.exp(seg), 0.0)
                mixes.append((cb * decay * dt_t[h:h + 1, :]).astype(BF16))
            xp = xc[:, pair * LANES:(pair + 1) * LANES]
            rhs = jnp.concatenate([jnp.where(lane_lo, xp, 0.0), jnp.where(lane_lo, 0.0, xp)],
                                  axis=0).astype(BF16)
            y_in = _dot(jnp.concatenate(mixes, axis=1), rhs)
            y_st = y_state[:, q * LANES:(q + 1) * LANES] * _pair_columns(ecum, pair, lane_lo)
            y_pairs.append(y_in + y_st)
            xw_parts.append((xp * _pair_columns(wend, pair, lane_lo)).astype(BF16))
        upd = _dot_tn(jnp.concatenate(xw_parts, axis=1), bg)
        for r in range(HEADS_PER_GROUP):
            h = g * HEADS_PER_GROUP + r
            rows = slice(h * SSM_HEAD_DIM, (h + 1) * SSM_HEAD_DIM)
            scale = jnp.broadcast_to(elast[0:1, h:h + 1], (SSM_HEAD_DIM, SSM_STATE))
            h_scr[rows, :] = h_scr[rows, :] * scale + upd[r * SSM_HEAD_DIM:(r + 1) * SSM_HEAD_DIM, :]

    y = jnp.concatenate(y_pairs, axis=1) + dskip_ref[...] * xc
    ya_ref[0] = _gated_group_norm(y, z_ref[0], ng_ref[...]).astype(BF16)

    @pl.when(c == pl.num_programs(1) - 1)
    def _():
        ssm_ref[0] = h_scr[...].reshape(SSM_HEADS, SSM_HEAD_DIM, SSM_STATE)

    ug = jax.nn.gelu(u_ref[0], approximate=True)
    vn = _layernorm(jax.nn.gelu(v_ref[0], approximate=True), lng_ref[...], lnb_ref[...])
    gates = []
    for g in range(CMLP_GROUPS):
        vg = vn[:, g * CMLP_GROUP_DIM:(g + 1) * CMLP_GROUP_DIM].astype(BF16)
        gates.append(_dot(wsp_scr[g], vg) + bsp_ref[:, g:g + 1])
    yb_ref[0] = (ug * jnp.concatenate(gates, axis=1)).astype(BF16)


def _mixer_call(proj, proj_dt, conv_w, conv_b, dt_bias, a_log, d_skip, norm_g, ln_g, ln_b, w_sp, b_sp_t):
    nb, seq, _ = proj.shape
    nchunks = seq // CHUNK

    def col(block, width=D_MODEL):
        return pl.BlockSpec((1, CHUNK, width), lambda b, c: (b, c, block))

    def whole(a):
        zeros = (0,) * a.ndim
        return pl.BlockSpec(a.shape, lambda b, c: zeros)

    small = (conv_w, conv_b, dt_bias, a_log, d_skip, norm_g, ln_g, ln_b, w_sp, b_sp_t)
    return pl.pallas_call(
        _mixer_kernel,
        grid=(nb, nchunks),
        in_specs=[col(COL_Z), col(COL_U), col(COL_V), col(COL_XS),
                  col(COL_BC_512, 2 * SSM_GROUPS * SSM_STATE),
                  pl.BlockSpec((1, CHUNK, LANES), lambda b, c: (b, c, 0))]
                 + [whole(a) for a in small],
        out_specs=[
            pl.BlockSpec((1, CHUNK, SSM_INNER), lambda b, c: (b, c, 0)),
            pl.BlockSpec((1, CHUNK, D_MODEL), lambda b, c: (b, c, 0)),
            pl.BlockSpec((1, SSM_HEADS, SSM_HEAD_DIM, SSM_STATE), lambda b, c: (b, 0, 0, 0)),
            pl.BlockSpec((1, CONV_WIDTH - 1, CONV_DIM), lambda b, c: (b, 0, 0)),
        ],
        out_shape=[
            jax.ShapeDtypeStruct((nb, seq, SSM_INNER), BF16),
            jax.ShapeDtypeStruct((nb, seq, D_MODEL), BF16),
            jax.ShapeDtypeStruct((nb, SSM_HEADS, SSM_HEAD_DIM, SSM_STATE), F32),
            jax.ShapeDtypeStruct((nb, CONV_WIDTH - 1, CONV_DIM), F32),
        ],
        scratch_shapes=[
            pltpu.VMEM((SSM_INNER, SSM_STATE), F32),
            pltpu.VMEM((CHUNK + 8, CONV_DIM), F32),
            pltpu.VMEM((CMLP_GROUPS, CHUNK, CHUNK), BF16),
        ],
        compiler_params=_params(("arbitrary", "arbitrary")),
        name="mixer_prompt",
    )(proj, proj, proj, proj, proj, proj_dt, *small)


SAMPLE_TILE = 16


def _mixer_step_kernel(z_ref, u_ref, v_ref, xs_ref, bc_ref, dt_ref, ssm_in_ref, conv_in_ref,
                       cw_ref, cb_ref, dtb_ref, alog_ref, dskip_ref, ng_ref,
                       lng_ref, lnb_ref, wsp0_ref, bsp0_ref,
                       ya_ref, yb_ref, vout_ref, ssm_ref, conv_ref):
    tb = SAMPLE_TILE
    xbc_new = jnp.concatenate([xs_ref[...], bc_ref[...]], axis=1)
    hist = conv_in_ref[...]
    acc = xbc_new * cw_ref[CONV_WIDTH - 1:CONV_WIDTH, :]
    for k in range(CONV_WIDTH - 1):
        acc = acc + hist[:, k * CONV_DIM:(k + 1) * CONV_DIM] * cw_ref[k:k + 1, :]
    conv_ref[:, 0:(CONV_WIDTH - 2) * CONV_DIM] = hist[:, CONV_DIM:]
    conv_ref[:, (CONV_WIDTH - 2) * CONV_DIM:] = xbc_new

    xbc = jax.nn.silu(acc + cb_ref[...])
    xc = xbc[:, 0:SSM_INNER]
    bm = xbc[:, SSM_INNER:SSM_INNER + SSM_GROUPS * SSM_STATE]
    cm = xbc[:, SSM_INNER + SSM_GROUPS * SSM_STATE:CONV_DIM]
    dt = jax.nn.softplus(dt_ref[...] + dtb_ref[...])
    dec = jnp.exp(dt * (-jnp.exp(alog_ref[...])))

    def transposed(a):
        pad = jnp.zeros((LANES - tb, a.shape[1]), F32)
        return jnp.concatenate([a, pad], axis=0).T

    xc_t = transposed(xc)
    dt_t = transposed(dt)
    dec_t = transposed(dec)
    row_id = lax.broadcasted_iota(jnp.int32, (tb, SSM_STATE), 0)

    y_groups = [jnp.zeros((tb, GROUP_WIDTH), F32) for _ in range(SSM_GROUPS)]
    for b in range(tb):
        for g in range(SSM_GROUPS):
            b_row = bm[b:b + 1, g * SSM_STATE:(g + 1) * SSM_STATE]
            c_only = jnp.where(row_id == b, cm[:, g * SSM_STATE:(g + 1) * SSM_STATE], 0.0).astype(BF16)
            new_heads = []
            for r in range(HEADS_PER_GROUP):
                h = g * HEADS_PER_GROUP + r
                x_col = xc_t[h * SSM_HEAD_DIM:(h + 1) * SSM_HEAD_DIM, b:b + 1]
                push = x_col * dt_t[h:h + 1, b:b + 1]
                keep = jnp.broadcast_to(dec_t[h:h + 1, b:b + 1], (SSM_HEAD_DIM, SSM_STATE))
                h_new = ssm_in_ref[b, h] * keep + push * b_row
                ssm_ref[b, h] = h_new
                new_heads.append(h_new.astype(BF16))
            hg = jnp.concatenate(new_heads, axis=0)
            y_groups[g] = y_groups[g] + _dot_nt(c_only, hg)

    y = jnp.concatenate(y_groups, axis=1) + dskip_ref[...] * xc
    ya_ref[...] = _gated_group_norm(y, z_ref[...], ng_ref[...]).astype(BF16)

    ug = jax.nn.gelu(u_ref[...], approximate=True)
    vn = _layernorm(jax.nn.gelu(v_ref[...], approximate=True), lng_ref[...], lnb_ref[...])
    vout_ref[...] = vn
    yb_ref[...] = (ug * (vn * wsp0_ref[...] + bsp0_ref[...])).astype(BF16)


def _mixer_step_call(layer, proj, proj_dt, state_ssm, conv_hist, conv_w, conv_b, dt_bias, a_log,
                     d_skip, norm_g, ln_g, ln_b, w_sp0, b_sp0):
    nseq = proj.shape[0]
    tb = SAMPLE_TILE

    def col(block, width=D_MODEL):
        return pl.BlockSpec((tb, width), lambda i: (i, block))

    def whole(a):
        zeros = (0,) * a.ndim
        return pl.BlockSpec(a.shape, lambda i: zeros)

    small = (conv_w, conv_b, dt_bias, a_log, d_skip, norm_g, ln_g, ln_b, w_sp0, b_sp0)
    hist_width = (CONV_WIDTH - 1) * CONV_DIM
    return pl.pallas_call(
        _mixer_step_kernel,
        grid=(nseq // tb,),
        in_specs=[col(COL_Z), col(COL_U), col(COL_V), col(COL_XS),
                  col(COL_BC_512, 2 * SSM_GROUPS * SSM_STATE),
                  pl.BlockSpec((tb, LANES), lambda i: (i, 0)),
                  pl.BlockSpec((None, tb, SSM_HEADS, SSM_HEAD_DIM, SSM_STATE),
                               lambda i: (layer, i, 0, 0, 0)),
                  pl.BlockSpec((tb, hist_width), lambda i: (i, 0))]
                 + [whole(a) for a in small],
        out_specs=[
            pl.BlockSpec((tb, SSM_INNER), lambda i: (i, 0)),
            pl.BlockSpec((tb, D_MODEL), lambda i: (i, 0)),
            pl.BlockSpec((tb, D_MODEL), lambda i: (i, 0)),
            pl.BlockSpec((tb, SSM_HEADS, SSM_HEAD_DIM, SSM_STATE), lambda i: (i, 0, 0, 0)),
            pl.BlockSpec((tb, hist_width), lambda i: (i, 0)),
        ],
        out_shape=[
            jax.ShapeDtypeStruct((nseq, SSM_INNER), BF16),
            jax.ShapeDtypeStruct((nseq, D_MODEL), BF16),
            jax.ShapeDtypeStruct((nseq, D_MODEL), F32),
            jax.ShapeDtypeStruct((nseq, SSM_HEADS, SSM_HEAD_DIM, SSM_STATE), F32),
            jax.ShapeDtypeStruct((nseq, hist_width), F32),
        ],
        compiler_params=_params(("arbitrary",)),
        name="mixer_sample",
    )(proj, proj, proj, proj, proj, proj_dt, state_ssm, conv_hist, *small)


def _outproj_kernel(ya_ref, yb_ref, ga_ref, gb_ref, x_ref, g1_ref, pg_ref, wa_ref, wb_ref, wo_ref, o_ref):
    merged = (jax.nn.sigmoid(ga_ref[0]) * _dot(ya_ref[0], wa_ref[...])
              + jax.nn.sigmoid(gb_ref[0]) * _dot(yb_ref[0], wb_ref[...]))
    o = _dot(merged.astype(BF16), wo_ref[...])
    o_ref[0] = x_ref[0] + g1_ref[0] * (_rms(o) * pg_ref[...])


def _outproj_call(ya, yb, proj, x, mod, post_g, wa, wb, wo, tm):
    nb, seq, _ = x.shape

    def rows(block=0):
        return pl.BlockSpec((1, tm, D_MODEL), lambda b, i: (b, i, block))

    def whole(a):
        return pl.BlockSpec(a.shape, lambda b, i: (0, 0))

    return pl.pallas_call(
        _outproj_kernel,
        grid=(nb, seq // tm),
        in_specs=[rows(), rows(), rows(COL_GA), rows(COL_GB), rows(),
                  _mod_spec(mod, tm, MOD_G1, 2), whole(post_g), whole(wa), whole(wb), whole(wo)],
        out_specs=rows(),
        out_shape=jax.ShapeDtypeStruct(x.shape, F32),
        compiler_params=_params(("arbitrary", "arbitrary")),
        name="out_proj",
    )(ya, yb, proj, proj, x, mod, post_g, wa, wb, wo)


def _route(h, rw_ref, rb_ref):
    logits = jnp.dot(h, rw_ref[...], preferred_element_type=F32,
                     precision=lax.Precision.HIGHEST) + rb_ref[...]
    lane = lax.broadcasted_iota(jnp.int32, logits.shape, 1)
    neg = jnp.float32(-jnp.inf)
    logits = jnp.where(lane < N_EXPERTS, logits, neg)
    m1 = jnp.max(logits, axis=-1, keepdims=True)
    i1 = jnp.min(jnp.where(logits == m1, lane, LANES), axis=-1, keepdims=True)
    rest = jnp.where(lane == i1, neg, logits)
    m2 = jnp.max(rest, axis=-1, keepdims=True)
    i2 = jnp.min(jnp.where(rest == m2, lane, LANES), axis=-1, keepdims=True)
    e2 = jnp.exp(m2 - m1)
    p1 = 1.0 / (1.0 + e2)
    return jnp.where(lane == i1, p1, 0.0) + jnp.where(lane == i2, e2 * p1, 0.0)


def _ffn_kernel(routed, x_ref, pre_ref, post_ref, sh_ref, sc_ref, g2_ref, rw_ref, rb_ref,
                wg_ref, wu_ref, wd_ref, o_ref, h_scr, acc_scr, comb_scr):
    e = pl.program_id(2)
    f = pl.program_id(3)
    first = jnp.logical_and(e == 0, f == 0)
    final = jnp.logical_and(e == pl.num_programs(2) - 1, f == pl.num_programs(3) - 1)

    @pl.when(first)
    def _():
        h = _rms(x_ref[0]) * pre_ref[...] * (1.0 + sc_ref[0]) + sh_ref[0]
        h_scr[...] = h.astype(BF16)
        acc_scr[...] = jnp.zeros_like(acc_scr)
        if routed:
            comb_scr[...] = _route(h, rw_ref, rb_ref)

    hb = h_scr[...]
    act = jax.nn.silu(_dot(hb, wg_ref[0])) * _dot(hb, wu_ref[0])
    if routed:
        lane = lax.broadcasted_iota(jnp.int32, comb_scr.shape, 1)
        weight = jnp.sum(jnp.where(lane == e, comb_scr[...], 0.0), axis=-1, keepdims=True)
        contrib = weight * _dot(act.astype(BF16), wd_ref[0])
    else:
        contrib = _dot(act.astype(BF16), wd_ref[0])
    acc_scr[...] += contrib

    @pl.when(final)
    def _():
        o_ref[0] = x_ref[0] + g2_ref[0] * (_rms(acc_scr[...]) * post_ref[...])


def _ffn_call(x, mod, pre_g, post_g, router_w, router_b, wg, wu, wd, tm, tf, routed):
    nb, seq, _ = x.shape
    n_exp = wg.shape[0]
    grid = (nb, seq // tm, n_exp, FFN_DIM // tf)

    def whole(a):
        return pl.BlockSpec(a.shape, lambda b, i, e, f: (0, 0))

    def mod_spec(seg):
        per_row = mod.shape[1] != 1
        rows = tm if per_row else 1
        return pl.BlockSpec((1, rows, D_MODEL), lambda b, i, e, f: (b, i if per_row else 0, seg))

    return pl.pallas_call(
        functools.partial(_ffn_kernel, routed),
        grid=grid,
        in_specs=[
            pl.BlockSpec((1, tm, D_MODEL), lambda b, i, e, f: (b, i, 0)),
            whole(pre_g), whole(post_g), mod_spec(MOD_SH2), mod_spec(MOD_SC2), mod_spec(MOD_G2),
            whole(router_w), whole(router_b),
            pl.BlockSpec((1, D_MODEL, tf), lambda b, i, e, f: (e, 0, f)),
            pl.BlockSpec((1, D_MODEL, tf), lambda b, i, e, f: (e, 0, f)),
            pl.BlockSpec((1, tf, D_MODEL), lambda b, i, e, f: (e, f, 0)),
        ],
        out_specs=pl.BlockSpec((1, tm, D_MODEL), lambda b, i, e, f: (b, i, 0)),
        out_shape=jax.ShapeDtypeStruct(x.shape, F32),
        scratch_shapes=[
            pltpu.VMEM((tm, D_MODEL), BF16),
            pltpu.VMEM((tm, D_MODEL), F32),
            pltpu.VMEM((tm, LANES), F32),
        ],
        compiler_params=_params(("arbitrary",) * 4),
        name="moe_ffn" if routed else "dense_ffn",
    )(x, pre_g, post_g, mod, mod, mod, router_w, router_b, wg, wu, wd)


def _pad_lanes(v):
    return jnp.pad(v, (0, LANES - v.shape[0])).reshape(1, LANES)


def kernel(x_prompt, x_sample, state_ssm, state_conv, c_prompt, c_sample, w_mod, b_mod, mix_pre_g, mix_post_g, ffn_pre_g, ffn_post_g, w_in, conv_w, conv_b, dt_bias, a_log, d_skip, ssm_norm_g, w_ssd_out, cmlp_ln_g, cmlp_ln_b, w_spatial, b_spatial, w_cmlp_out, w_o, ffn_wg, ffn_wu, ffn_wd, router_w, router_b, exp_wg, exp_wu, exp_wd):
    n_prompt = x_prompt.shape[0]
    n_sample = x_sample.shape[0]

    c_all = jnp.concatenate([c_prompt, c_sample, jnp.zeros((8, D_MODEL), F32)], axis=0)
    mod_all = _mod_call(c_all, w_mod, b_mod)
    mod_p = mod_all[:, :n_prompt].reshape(DEPTH, n_prompt, 1, 6 * D_MODEL)
    mod_s = mod_all[:, n_prompt:n_prompt + n_sample].reshape(DEPTH, 1, n_sample, 6 * D_MODEL)

    xp = x_prompt
    xs = x_sample.reshape(1, n_sample, D_MODEL)
    conv_hist = state_conv.reshape(DEPTH, n_sample, (CONV_WIDTH - 1) * CONV_DIM)

    o_xbc = SSM_INNER
    o_dt = o_xbc + CONV_DIM
    o_u = o_dt + SSM_HEADS

    ssm_p, conv_p, ssm_s, conv_s, v_s = [], [], [], [], []
    for l in range(DEPTH):
        w = w_in[l]
        w_main = jnp.concatenate([w[:, :o_xbc], w[:, o_u:], w[:, o_xbc:o_dt]], axis=1).astype(BF16)
        w_dt = jnp.pad(w[:, o_dt:o_u], ((0, 0), (0, LANES - SSM_HEADS))).astype(BF16)
        row = lambda a: a[l].reshape(1, -1)
        mixer_small = (conv_w[l], row(conv_b), _pad_lanes(dt_bias[l]), _pad_lanes(a_log[l]),
                       jnp.repeat(d_skip[l], SSM_HEAD_DIM).reshape(1, -1), row(ssm_norm_g),
                       row(cmlp_ln_g), row(cmlp_ln_b))
        wa = w_ssd_out[l].astype(BF16)
        wb = w_cmlp_out[l].astype(BF16)
        wo = w_o[l].astype(BF16)

        proj_p, dt_p = _inproj_call(xp, mod_p[l], row(mix_pre_g), w_main, w_dt, tm=1024)
        proj_s, dt_s = _inproj_call(xs, mod_s[l], row(mix_pre_g), w_main, w_dt, tm=n_sample)
        ya_p, yb_p, hs_p, cs_p = _mixer_call(proj_p, dt_p, *mixer_small, w_spatial[l], b_spatial[l].T)
        ya_s, yb_s, vr_s, hs_s, cs_s = _mixer_step_call(
            l, proj_s[0], dt_s[0], state_ssm, conv_hist[l], *mixer_small,
            jnp.repeat(w_spatial[l, :, 0, 0], CMLP_GROUP_DIM).reshape(1, -1),
            jnp.repeat(b_spatial[l, :, 0], CMLP_GROUP_DIM).reshape(1, -1))
        xp = _outproj_call(ya_p, yb_p, proj_p, xp, mod_p[l], row(mix_post_g), wa, wb, wo, tm=512)
        xs = _outproj_call(ya_s[None], yb_s[None], proj_s, xs, mod_s[l], row(mix_post_g), wa, wb, wo,
                           tm=n_sample)

        j = l // 2
        if l % 2 == 0:
            rw = jnp.zeros((D_MODEL, LANES), F32)
            rb = jnp.zeros((1, LANES), F32)
            wg, wu, wd = ffn_wg[j][None], ffn_wu[j][None], ffn_wd[j][None]
        else:
            rw = jnp.pad(router_w[j], ((0, 0), (0, LANES - N_EXPERTS)))
            rb = _pad_lanes(router_b[j])
            wg, wu, wd = exp_wg[j], exp_wu[j], exp_wd[j]
        wg, wu, wd = wg.astype(BF16), wu.astype(BF16), wd.astype(BF16)
        routed = l % 2 == 1
        xp = _ffn_call(xp, mod_p[l], row(ffn_pre_g), row(ffn_post_g), rw, rb, wg, wu, wd,
                       tm=1024, tf=256, routed=routed)
        xs = _ffn_call(xs, mod_s[l], row(ffn_pre_g), row(ffn_post_g), rw, rb, wg, wu, wd,
                       tm=n_sample, tf=256, routed=routed)

        ssm_p.append(hs_p)
        conv_p.append(cs_p)
        ssm_s.append(hs_s)
        conv_s.append(cs_s.reshape(n_sample, CONV_WIDTH - 1, CONV_DIM))
        v_s.append(vr_s.reshape(n_sample, 1, D_MODEL))

    return (xp, xs.reshape(n_sample, 1, D_MODEL), jnp.stack(ssm_p), jnp.stack(conv_p),
            jnp.stack(ssm_s), jnp.stack(conv_s), jnp.stack(v_s))
```

```python
import functools

import jax
import jax.numpy as jnp
from jax import lax
from jax.experimental import pallas as pl
from jax.experimental.pallas import tpu as pltpu

F32 = jnp.float32
BF16 = jnp.bfloat16

D_MODEL = 1024
DEPTH = 4
SSM_HEADS = 16
SSM_HEAD_DIM = 64
SSM_GROUPS = 2
SSM_STATE = 128
SSM_INNER = 1024
GROUP_WIDTH = SSM_INNER // SSM_GROUPS
HEADS_PER_GROUP = SSM_HEADS // SSM_GROUPS
CONV_WIDTH = 4
CONV_DIM = 1536
CHUNK = 128
CMLP_GROUPS = 8
CMLP_GROUP_DIM = 128
FFN_DIM = 2816
N_EXPERTS = 8
EPS = 1e-6
LANES = 128
COL_Z, COL_U, COL_V, COL_GA, COL_GB, COL_XS = 0, 1, 2, 3, 4, 5
COL_BC_512 = 12
PROJ_MAIN = 6 * D_MODEL + 2 * SSM_GROUPS * SSM_STATE
PROJ_TILE = 512
MOD_SH1, MOD_SC1, MOD_G1, MOD_SH2, MOD_SC2, MOD_G2 = range(6)
VMEM_LIMIT = 56 * 1024 * 1024


def _params(semantics):
    return pltpu.CompilerParams(dimension_semantics=semantics, vmem_limit_bytes=VMEM_LIMIT)


def _rms(x):
    return x * lax.rsqrt(jnp.mean(x * x, axis=-1, keepdims=True) + EPS)


def _dot(a, b):
    return jnp.dot(a, b, preferred_element_type=F32)


def _dot_nt(a, b):
    return lax.dot_general(a, b, (((1,), (1,)), ((), ())), preferred_element_type=F32)


def _dot_tn(a, b):
    return lax.dot_general(a, b, (((0,), (0,)), ((), ())), preferred_element_type=F32)


def _mod_kernel(c_ref, w_ref, b_ref, o_ref):
    a = jax.nn.silu(c_ref[...]).astype(BF16)
    o_ref[0] = _dot(a, w_ref[0].astype(BF16)) + b_ref[0]


def _mod_call(c_all, w_mod, b_mod):
    rows = c_all.shape[0]
    return pl.pallas_call(
        _mod_kernel,
        grid=(DEPTH, 6),
        in_specs=[
            pl.BlockSpec((rows, D_MODEL), lambda l, j: (0, 0)),
            pl.BlockSpec((1, D_MODEL, D_MODEL), lambda l, j: (l, 0, j)),
            pl.BlockSpec((1, 1, D_MODEL), lambda l, j: (l, 0, j)),
        ],
        out_specs=pl.BlockSpec((1, rows, D_MODEL), lambda l, j: (l, 0, j)),
        out_shape=jax.ShapeDtypeStruct((DEPTH, rows, 6 * D_MODEL), F32),
        compiler_params=_params(("arbitrary", "arbitrary")),
        name="adaln_mod",
    )(c_all, w_mod, b_mod.reshape(DEPTH, 1, 6 * D_MODEL))


def _mod_spec(mod, tm, seg, grid_rank):
    per_row = mod.shape[1] != 1
    rows = tm if per_row else 1
    if grid_rank == 2:
        return pl.BlockSpec((1, rows, D_MODEL), lambda b, i: (b, i if per_row else 0, seg))
    return pl.BlockSpec((1, rows, D_MODEL), lambda b, i, j: (b, i if per_row else 0, seg))


def _inproj_kernel(x_ref, g_ref, sh_ref, sc_ref, w_ref, wdt_ref, p_ref, pdt_ref, h_scr):
    @pl.when(pl.program_id(2) == 0)
    def _():
        h = _rms(x_ref[0]) * g_ref[...] * (1.0 + sc_ref[0]) + sh_ref[0]
        hb = h.astype(BF16)
        h_scr[...] = hb
        pdt_ref[0] = _dot(hb, wdt_ref[...])

    p_ref[0] = _dot(h_scr[...], w_ref[...])


def _inproj_call(x, mod, g, w_main, w_dt, tm):
    nb, seq, _ = x.shape
    grid = (nb, seq // tm, PROJ_MAIN // PROJ_TILE)
    return pl.pallas_call(
        _inproj_kernel,
        grid=grid,
        in_specs=[
            pl.BlockSpec((1, tm, D_MODEL), lambda b, i, j: (b, i, 0)),
            pl.BlockSpec((1, D_MODEL), lambda b, i, j: (0, 0)),
            _mod_spec(mod, tm, MOD_SH1, 3),
            _mod_spec(mod, tm, MOD_SC1, 3),
            pl.BlockSpec((D_MODEL, PROJ_TILE), lambda b, i, j: (0, j)),
            pl.BlockSpec((D_MODEL, LANES), lambda b, i, j: (0, 0)),
        ],
        out_specs=[
            pl.BlockSpec((1, tm, PROJ_TILE), lambda b, i, j: (b, i, j)),
            pl.BlockSpec((1, tm, LANES), lambda b, i, j: (b, i, 0)),
        ],
        out_shape=[
            jax.ShapeDtypeStruct((nb, seq, PROJ_MAIN), F32),
            jax.ShapeDtypeStruct((nb, seq, LANES), F32),
        ],
        scratch_shapes=[pltpu.VMEM((tm, D_MODEL), BF16)],
        compiler_params=_params(("arbitrary", "arbitrary", "arbitrary")),
        name="in_proj",
    )(x, g, mod, mod, w_main, w_dt)


def _gated_group_norm(y, z, norm_g):
    y = y * jax.nn.silu(z)
    parts = [_rms(y[:, g * GROUP_WIDTH:(g + 1) * GROUP_WIDTH]) for g in range(SSM_GROUPS)]
    return jnp.concatenate(parts, axis=-1) * norm_g


def _layernorm(x, g, b):
    mu = jnp.mean(x, axis=-1, keepdims=True)
    xc = x - mu
    return xc * lax.rsqrt(jnp.mean(xc * xc, axis=-1, keepdims=True) + EPS) * g + b


def _pair_columns(v, pair, lane_lo):
    h0 = 2 * pair
    return jnp.where(lane_lo, v[:, h0:h0 + 1], v[:, h0 + 1:h0 + 2])


def _mixer_kernel(z_ref, u_ref, v_ref, xs_ref, bc_ref, dt_ref,
                  cw_ref, cb_ref, dtb_ref, alog_ref, dskip_ref, ng_ref,
                  lng_ref, lnb_ref, wsp_ref, bsp_ref,
                  ya_ref, yb_ref, ssm_ref, conv_ref,
                  h_scr, cbuf, wsp_scr):
    c = pl.program_id(1)
    row = lax.broadcasted_iota(jnp.int32, (CHUNK, CHUNK), 0)
    col = lax.broadcasted_iota(jnp.int32, (CHUNK, CHUNK), 1)
    causal = row >= col
    lane_lo = col < SSM_HEAD_DIM

    @pl.when(c == 0)
    def _():
        h_scr[...] = jnp.zeros_like(h_scr)
        cbuf[0:8, :] = jnp.zeros((8, CONV_DIM), F32)
        for g in range(CMLP_GROUPS):
            wsp_scr[g] = jnp.where(causal, wsp_ref[g], 0.0).astype(BF16)

    cbuf[8:8 + CHUNK, 0:SSM_INNER] = xs_ref[0]
    cbuf[8:8 + CHUNK, SSM_INNER:CONV_DIM] = bc_ref[0]
    acc = cbuf[5:5 + CHUNK, :] * cw_ref[0:1, :]
    for k in range(1, CONV_WIDTH):
        acc = acc + cbuf[5 + k:5 + k + CHUNK, :] * cw_ref[k:k + 1, :]
    tail = cbuf[5 + CHUNK:8 + CHUNK, :]
    cbuf[5:8, :] = tail

    @pl.when(c == pl.num_programs(1) - 1)
    def _():
        conv_ref[0] = tail

    xbc = jax.nn.silu(acc + cb_ref[...])
    xc = xbc[:, 0:SSM_INNER]
    bm = xbc[:, SSM_INNER:SSM_INNER + SSM_GROUPS * SSM_STATE].astype(BF16)
    cm = xbc[:, SSM_INNER + SSM_GROUPS * SSM_STATE:CONV_DIM].astype(BF16)

    dt = jax.nn.softplus(dt_ref[0] + dtb_ref[...])
    da = dt * (-jnp.exp(alog_ref[...]))
    cum = jnp.dot(causal.astype(F32), da, preferred_element_type=F32,
                  precision=lax.Precision.HIGHEST)
    last = cum[CHUNK - 1:CHUNK, :]
    ecum = jnp.exp(cum)
    wend = jnp.exp(last - cum) * dt
    elast = jnp.exp(last)
    cum_t = cum.T
    dt_t = dt.T

    y_pairs = []
    for g in range(SSM_GROUPS):
        bg = bm[:, g * SSM_STATE:(g + 1) * SSM_STATE]
        cg = cm[:, g * SSM_STATE:(g + 1) * SSM_STATE]
        cb = _dot_nt(cg, bg)
        hg = h_scr[g * GROUP_WIDTH:(g + 1) * GROUP_WIDTH, :]
        y_state = _dot_nt(cg, hg.astype(BF16))
        xw_parts = []
        for q in range(HEADS_PER_GROUP // 2):
            pair = g * (HEADS_PER_GROUP // 2) + q
            mixes = []
            for h in (2 * pair, 2 * pair + 1):
                seg = cum[:, h:h + 1] - cum_t[h:h + 1, :]
                decay = jnp.where(causal, jnp.exp(seg), 0.0)
                mixes.append((cb * decay * dt_t[h:h + 1, :]).astype(BF16))
            xp = xc[:, pair * LANES:(pair + 1) * LANES]
            rhs = jnp.concatenate([jnp.where(lane_lo, xp, 0.0), jnp.where(lane_lo, 0.0, xp)],
                                  axis=0).astype(BF16)
            y_in = _dot(jnp.concatenate(mixes, axis=1), rhs)
            y_st = y_state[:, q * LANES:(q + 1) * LANES] * _pair_columns(ecum, pair, lane_lo)
            y_pairs.append(y_in + y_st)
            xw_parts.append((xp * _pair_columns(wend, pair, lane_lo)).astype(BF16))
        upd = _dot_tn(jnp.concatenate(xw_parts, axis=1), bg)
        for r in range(HEADS_PER_GROUP):
            h = g * HEADS_PER_GROUP + r
            rows = slice(h * SSM_HEAD_DIM, (h + 1) * SSM_HEAD_DIM)
            scale = jnp.broadcast_to(elast[0:1, h:h + 1], (SSM_HEAD_DIM, SSM_STATE))
            h_scr[rows, :] = h_scr[rows, :] * scale + upd[r * SSM_HEAD_DIM:(r + 1) * SSM_HEAD_DIM, :]

    y = jnp.concatenate(y_pairs, axis=1) + dskip_ref[...] * xc
    ya_ref[0] = _gated_group_norm(y, z_ref[0], ng_ref[...]).astype(BF16)

    @pl.when(c == pl.num_programs(1) - 1)
    def _():
        ssm_ref[0] = h_scr[...].reshape(SSM_HEADS, SSM_HEAD_DIM, SSM_STATE)

    ug = jax.nn.gelu(u_ref[0], approximate=True)
    vn = _layernorm(jax.nn.gelu(v_ref[0], approximate=True), lng_ref[...], lnb_ref[...])
    gates = []
    for g in range(CMLP_GROUPS):
        vg = vn[:, g * CMLP_GROUP_DIM:(g + 1) * CMLP_GROUP_DIM].astype(BF16)
        gates.append(_dot(wsp_scr[g], vg) + bsp_ref[:, g:g + 1])
    yb_ref[0] = (ug * jnp.concatenate(gates, axis=1)).astype(BF16)


def _mixer_call(proj, proj_dt, conv_w, conv_b, dt_bias, a_log, d_skip, norm_g, ln_g, ln_b, w_sp, b_sp_t):
    nb, seq, _ = proj.shape
    nchunks = seq // CHUNK

    def col(block, width=D_MODEL):
        return pl.BlockSpec((1, CHUNK, width), lambda b, c: (b, c, block))

    def whole(a):
        zeros = (0,) * a.ndim
        return pl.BlockSpec(a.shape, lambda b, c: zeros)

    small = (conv_w, conv_b, dt_bias, a_log, d_skip, norm_g, ln_g, ln_b, w_sp, b_sp_t)
    return pl.pallas_call(
        _mixer_kernel,
        grid=(nb, nchunks),
        in_specs=[col(COL_Z), col(COL_U), col(COL_V), col(COL_XS),
                  col(COL_BC_512, 2 * SSM_GROUPS * SSM_STATE),
                  pl.BlockSpec((1, CHUNK, LANES), lambda b, c: (b, c, 0))]
                 + [whole(a) for a in small],
        out_specs=[
            pl.BlockSpec((1, CHUNK, SSM_INNER), lambda b, c: (b, c, 0)),
            pl.BlockSpec((1, CHUNK, D_MODEL), lambda b, c: (b, c, 0)),
            pl.BlockSpec((1, SSM_HEADS, SSM_HEAD_DIM, SSM_STATE), lambda b, c: (b, 0, 0, 0)),
            pl.BlockSpec((1, CONV_WIDTH - 1, CONV_DIM), lambda b, c: (b, 0, 0)),
        ],
        out_shape=[
            jax.ShapeDtypeStruct((nb, seq, SSM_INNER), BF16),
            jax.ShapeDtypeStruct((nb, seq, D_MODEL), BF16),
            jax.ShapeDtypeStruct((nb, SSM_HEADS, SSM_HEAD_DIM, SSM_STATE), F32),
            jax.ShapeDtypeStruct((nb, CONV_WIDTH - 1, CONV_DIM), F32),
        ],
        scratch_shapes=[
            pltpu.VMEM((SSM_INNER, SSM_STATE), F32),
            pltpu.VMEM((CHUNK + 8, CONV_DIM), F32),
            pltpu.VMEM((CMLP_GROUPS, CHUNK, CHUNK), BF16),
        ],
        compiler_params=_params(("arbitrary", "arbitrary")),
        name="mixer_prompt",
    )(proj, proj, proj, proj, proj, proj_dt, *small)


SAMPLE_TILE = 16


def _mixer_step_kernel(z_ref, u_ref, v_ref, xs_ref, bc_ref, dt_ref, ssm_in_ref, conv_in_ref,
                       cw_ref, cb_ref, dtb_ref, alog_ref, dskip_ref, ng_ref,
                       lng_ref, lnb_ref, wsp0_ref, bsp0_ref,
                       ya_ref, yb_ref, vout_ref, ssm_ref, conv_ref):
    tb = SAMPLE_TILE
    xbc_new = jnp.concatenate([xs_ref[...], bc_ref[...]], axis=1)
    hist = conv_in_ref[...]
    acc = xbc_new * cw_ref[CONV_WIDTH - 1:CONV_WIDTH, :]
    for k in range(CONV_WIDTH - 1):
        acc = acc + hist[:, k * CONV_DIM:(k + 1) * CONV_DIM] * cw_ref[k:k + 1, :]
    conv_ref[:, 0:(CONV_WIDTH - 2) * CONV_DIM] = hist[:, CONV_DIM:]
    conv_ref[:, (CONV_WIDTH - 2) * CONV_DIM:] = xbc_new

    xbc = jax.nn.silu(acc + cb_ref[...])
    xc = xbc[:, 0:SSM_INNER]
    bm = xbc[:, SSM_INNER:SSM_INNER + SSM_GROUPS * SSM_STATE]
    cm = xbc[:, SSM_INNER + SSM_GROUPS * SSM_STATE:CONV_DIM]
    dt = jax.nn.softplus(dt_ref[...] + dtb_ref[...])
    dec = jnp.exp(dt * (-jnp.exp(alog_ref[...])))

    def transposed(a):
        pad = jnp.zeros((LANES - tb, a.shape[1]), F32)
        return jnp.concatenate([a, pad], axis=0).T

    xc_t = transposed(xc)
    dt_t = transposed(dt)
    dec_t = transposed(dec)
    row_id = lax.broadcasted_iota(jnp.int32, (tb, SSM_STATE), 0)

    y_groups = [jnp.zeros((tb, GROUP_WIDTH), F32) for _ in range(SSM_GROUPS)]
    for b in range(tb):
        for g in range(SSM_GROUPS):
            b_row = bm[b:b + 1, g * SSM_STATE:(g + 1) * SSM_STATE]
            c_only = jnp.where(row_id == b, cm[:, g * SSM_STATE:(g + 1) * SSM_STATE], 0.0).astype(BF16)
            new_heads = []
            for r in range(HEADS_PER_GROUP):
                h = g * HEADS_PER_GROUP + r
                x_col = xc_t[h * SSM_HEAD_DIM:(h + 1) * SSM_HEAD_DIM, b:b + 1]
                push = x_col * dt_t[h:h + 1, b:b + 1]
                keep = jnp.broadcast_to(dec_t[h:h + 1, b:b + 1], (SSM_HEAD_DIM, SSM_STATE))
                h_new = ssm_in_ref[b, h] * keep + push * b_row
                ssm_ref[b, h] = h_new
                new_heads.append(h_new.astype(BF16))
            hg = jnp.concatenate(new_heads, axis=0)
            y_groups[g] = y_groups[g] + _dot_nt(c_only, hg)

    y = jnp.concatenate(y_groups, axis=1) + dskip_ref[...] * xc
    ya_ref[...] = _gated_group_norm(y, z_ref[...], ng_ref[...]).astype(BF16)

    ug = jax.nn.gelu(u_ref[...], approximate=True)
    vn = _layernorm(jax.nn.gelu(v_ref[...], approximate=True), lng_ref[...], lnb_ref[...])
    vout_ref[...] = vn
    yb_ref[...] = (ug * (vn * wsp0_ref[...] + bsp0_ref[...])).astype(BF16)


def _mixer_step_call(layer, proj, proj_dt, state_ssm, conv_hist, conv_w, conv_b, dt_bias, a_log,
                     d_skip, norm_g, ln_g, ln_b, w_sp0, b_sp0):
    nseq = proj.shape[0]
    tb = SAMPLE_TILE

    def col(block, width=D_MODEL):
        return pl.BlockSpec((tb, width), lambda i: (i, block))

    def whole(a):
        zeros = (0,) * a.ndim
        return pl.BlockSpec(a.shape, lambda i: zeros)

    small = (conv_w, conv_b, dt_bias, a_log, d_skip, norm_g, ln_g, ln_b, w_sp0, b_sp0)
    hist_width = (CONV_WIDTH - 1) * CONV_DIM
    return pl.pallas_call(
        _mixer_step_kernel,
        grid=(nseq // tb,),
        in_specs=[col(COL_Z), col(COL_U), col(COL_V), col(COL_XS),
                  col(COL_BC_512, 2 * SSM_GROUPS * SSM_STATE),
                  pl.BlockSpec((tb, LANES), lambda i: (i, 0)),
                  pl.BlockSpec((None, tb, SSM_HEADS, SSM_HEAD_DIM, SSM_STATE),
                               lambda i: (layer, i, 0, 0, 0)),
                  pl.BlockSpec((tb, hist_width), lambda i: (i, 0))]
                 + [whole(a) for a in small],
        out_specs=[
            pl.BlockSpec((tb, SSM_INNER), lambda i: (i, 0)),
            pl.BlockSpec((tb, D_MODEL), lambda i: (i, 0)),
            pl.BlockSpec((tb, D_MODEL), lambda i: (i, 0)),
            pl.BlockSpec((tb, SSM_HEADS, SSM_HEAD_DIM, SSM_STATE), lambda i: (i, 0, 0, 0)),
            pl.BlockSpec((tb, hist_width), lambda i: (i, 0)),
        ],
        out_shape=[
            jax.ShapeDtypeStruct((nseq, SSM_INNER), BF16),
            jax.ShapeDtypeStruct((nseq, D_MODEL), BF16),
            jax.ShapeDtypeStruct((nseq, D_MODEL), F32),
            jax.ShapeDtypeStruct((nseq, SSM_HEADS, SSM_HEAD_DIM, SSM_STATE), F32),
            jax.ShapeDtypeStruct((nseq, hist_width), F32),
        ],
        compiler_params=_params(("arbitrary",)),
        name="mixer_sample",
    )(proj, proj, proj, proj, proj, proj_dt, state_ssm, conv_hist, *small)


def _outproj_kernel(ya_ref, yb_ref, ga_ref, gb_ref, x_ref, g1_ref, pg_ref, wa_ref, wb_ref, wo_ref, o_ref):
    merged = (jax.nn.sigmoid(ga_ref[0]) * _dot(ya_ref[0], wa_ref[...])
              + jax.nn.sigmoid(gb_ref[0]) * _dot(yb_ref[0], wb_ref[...]))
    o = _dot(merged.astype(BF16), wo_ref[...])
    o_ref[0] = x_ref[0] + g1_ref[0] * (_rms(o) * pg_ref[...])


def _outproj_call(ya, yb, proj, x, mod, post_g, wa, wb, wo, tm):
    nb, seq, _ = x.shape

    def rows(block=0):
        return pl.BlockSpec((1, tm, D_MODEL), lambda b, i: (b, i, block))

    def whole(a):
        return pl.BlockSpec(a.shape, lambda b, i: (0, 0))

    return pl.pallas_call(
        _outproj_kernel,
        grid=(nb, seq // tm),
        in_specs=[rows(), rows(), rows(COL_GA), rows(COL_GB), rows(),
                  _mod_spec(mod, tm, MOD_G1, 2), whole(post_g), whole(wa), whole(wb), whole(wo)],
        out_specs=rows(),
        out_shape=jax.ShapeDtypeStruct(x.shape, F32),
        compiler_params=_params(("arbitrary", "arbitrary")),
        name="out_proj",
    )(ya, yb, proj, proj, x, mod, post_g, wa, wb, wo)


FFN_CHUNK = 256


def _swiglu_into(hb, wg_ref, wu_ref, wd_ref, acc_ref):
    for f in range(FFN_DIM // FFN_CHUNK):
        cols = slice(f * FFN_CHUNK, (f + 1) * FFN_CHUNK)
        act = jax.nn.silu(_dot(hb, wg_ref[0, :, cols])) * _dot(hb, wu_ref[0, :, cols])
        part = _dot(act.astype(BF16), wd_ref[0, cols, :])
        if f == 0:
            acc_ref[...] = part
        else:
            acc_ref[...] += part


def _premod(x_ref, pre_ref, sh_ref, sc_ref):
    return _rms(x_ref[0]) * pre_ref[...] * (1.0 + sc_ref[0]) + sh_ref[0]


def _dense_ffn_kernel(x_ref, pre_ref, post_ref, sh_ref, sc_ref, g2_ref, wg_ref, wu_ref, wd_ref,
                      o_ref, acc_scr):
    hb = _premod(x_ref, pre_ref, sh_ref, sc_ref).astype(BF16)
    _swiglu_into(hb, wg_ref, wu_ref, wd_ref, acc_scr)
    o_ref[0] = x_ref[0] + g2_ref[0] * (_rms(acc_scr[...]) * post_ref[...])


def _dense_ffn_call(x, mod, pre_g, post_g, wg, wu, wd, tm):
    nb, seq, _ = x.shape

    def whole(a):
        zeros = (0,) * a.ndim
        return pl.BlockSpec(a.shape, lambda b, i: zeros)

    rows = pl.BlockSpec((1, tm, D_MODEL), lambda b, i: (b, i, 0))
    return pl.pallas_call(
        _dense_ffn_kernel,
        grid=(nb, seq // tm),
        in_specs=[rows, whole(pre_g), whole(post_g),
                  _mod_spec(mod, tm, MOD_SH2, 2), _mod_spec(mod, tm, MOD_SC2, 2),
                  _mod_spec(mod, tm, MOD_G2, 2), whole(wg), whole(wu), whole(wd)],
        out_specs=rows,
        out_shape=jax.ShapeDtypeStruct(x.shape, F32),
        scratch_shapes=[pltpu.VMEM((tm, D_MODEL), F32)],
        compiler_params=_params(("arbitrary", "arbitrary")),
        name="dense_ffn",
    )(x, pre_g, post_g, mod, mod, mod, wg, wu, wd)


EXPERT_TILE = 512
HALF = D_MODEL // 2
INFO_I1, INFO_I2, INFO_P1, INFO_P2, INFO_R1, INFO_R2 = range(6)
HI_MASK = 0xFFFF0000


def _route_kernel(x_ref, pre_ref, sh_ref, sc_ref, rw_ref, rb_ref, cnt0_ref,
                  hw_ref, info_ref, cnt_ref, carry):
    @pl.when(jnp.logical_and(pl.program_id(0) == 0, pl.program_id(1) == 0))
    def _():
        carry[...] = cnt0_ref[...]

    h = _premod(x_ref, pre_ref, sh_ref, sc_ref)
    tm = h.shape[0]
    bits = lax.bitcast_convert_type(h.astype(BF16).astype(F32), jnp.uint32)
    hw_ref[0] = (bits[:, HALF:] & jnp.uint32(HI_MASK)) | (bits[:, :HALF] >> jnp.uint32(16))

    logits = jnp.dot(h, rw_ref[...], preferred_element_type=F32,
                     precision=lax.Precision.HIGHEST) + rb_ref[...]
    lane = lax.broadcasted_iota(jnp.int32, logits.shape, 1).astype(F32)
    neg = jnp.float32(-jnp.inf)
    logits = jnp.where(lane < N_EXPERTS, logits, neg)
    m1 = jnp.max(logits, axis=-1, keepdims=True)
    i1 = jnp.min(jnp.where(logits == m1, lane, float(LANES)), axis=-1, keepdims=True)
    rest = jnp.where(lane == i1, neg, logits)
    m2 = jnp.max(rest, axis=-1, keepdims=True)
    i2 = jnp.min(jnp.where(rest == m2, lane, float(LANES)), axis=-1, keepdims=True)
    e2 = jnp.exp(m2 - m1)
    p1 = 1.0 / (1.0 + e2)
    p2 = e2 * p1

    member = jnp.logical_or(lane == i1, lane == i2)
    row = lax.broadcasted_iota(jnp.int32, (tm, tm), 0)
    col = lax.broadcasted_iota(jnp.int32, (tm, tm), 1)
    before = jnp.where(row > col, 1.0, 0.0).astype(BF16)
    ones = jnp.where(member, 1.0, 0.0)
    prior = _dot(before, ones.astype(BF16)) + carry[...]
    r1 = jnp.sum(jnp.where(lane == i1, prior, 0.0), axis=-1, keepdims=True)
    r2 = jnp.sum(jnp.where(lane == i2, prior, 0.0), axis=-1, keepdims=True)
    carry[...] += jnp.sum(ones, axis=0, keepdims=True)
    cnt_ref[...] = carry[...]

    info = jnp.zeros_like(logits)
    for k, v in ((INFO_I1, i1), (INFO_I2, i2), (INFO_P1, p1), (INFO_P2, p2), (INFO_R1, r1), (INFO_R2, r2)):
        info = jnp.where(lane == float(k), v, info)
    info_ref[0] = info


def _route_call(x, mod, pre_g, rw, rb, cnt0, tm):
    nb, seq, _ = x.shape

    def whole(a):
        return pl.BlockSpec(a.shape, lambda b, i: (0, 0))

    return pl.pallas_call(
        _route_kernel,
        grid=(nb, seq // tm),
        in_specs=[pl.BlockSpec((1, tm, D_MODEL), lambda b, i: (b, i, 0)), whole(pre_g),
                  _mod_spec(mod, tm, MOD_SH2, 2), _mod_spec(mod, tm, MOD_SC2, 2),
                  whole(rw), whole(rb), whole(cnt0)],
        out_specs=[pl.BlockSpec((1, tm, HALF), lambda b, i: (b, i, 0)),
                   pl.BlockSpec((1, tm, LANES), lambda b, i: (b, i, 0)),
                   pl.BlockSpec((1, LANES), lambda b, i: (0, 0))],
        out_shape=[jax.ShapeDtypeStruct((nb, seq, HALF), jnp.uint32),
                   jax.ShapeDtypeStruct((nb, seq, LANES), F32),
                   jax.ShapeDtypeStruct((1, LANES), F32)],
        scratch_shapes=[pltpu.VMEM((1, LANES), F32)],
        compiler_params=_params(("arbitrary", "arbitrary")),
        name="moe_route",
    )(x, pre_g, mod, mod, rw, rb, cnt0)


def _row_copy(src, src_row, dst, dst_row, sem):
    return pltpu.make_async_copy(src.at[pl.ds(src_row, 1)], dst.at[pl.ds(dst_row, 1)], sem)


def _dispatch_kernel(d1_ref, d2_ref, hw_ref, xs_in_ref, xs_ref, sem):
    del xs_in_ref
    tm = hw_ref.shape[0]
    base = pl.program_id(0) * tm

    def issue(r, carry):
        _row_copy(hw_ref, r, xs_ref, d1_ref[base + r], sem).start()
        _row_copy(hw_ref, r, xs_ref, d2_ref[base + r], sem).start()
        return carry

    lax.fori_loop(0, tm, issue, 0)
    for _ in range(2):
        pltpu.make_async_copy(hw_ref, xs_ref.at[pl.ds(0, tm)], sem).wait()


def _dispatch_call(dest1, dest2, hw, xs, tm):
    rows = hw.shape[0]
    return pl.pallas_call(
        _dispatch_kernel,
        grid_spec=pltpu.PrefetchScalarGridSpec(
            num_scalar_prefetch=2,
            grid=(rows // tm,),
            in_specs=[pl.BlockSpec((tm, HALF), lambda i, d1, d2: (i, 0)),
                      pl.BlockSpec(memory_space=pl.ANY)],
            out_specs=pl.BlockSpec(memory_space=pl.ANY),
            scratch_shapes=[pltpu.SemaphoreType.DMA(())],
        ),
        out_shape=jax.ShapeDtypeStruct(xs.shape, xs.dtype),
        input_output_aliases={3: 0},
        compiler_params=_params(("arbitrary",)),
        name="moe_dispatch",
    )(dest1, dest2, hw, xs)


def _experts_kernel(exp_ref, nv_ref, xs_ref, wg_ref, wu_ref, wd_ref, y_ref):
    del exp_ref
    used = pl.program_id(0) < nv_ref[0]

    @pl.when(used)
    def _():
        words = xs_ref[...]
        lo = lax.bitcast_convert_type(words << jnp.uint32(16), F32)
        hi = lax.bitcast_convert_type(words & jnp.uint32(HI_MASK), F32)
        hb = jnp.concatenate([lo, hi], axis=1).astype(BF16)
        _swiglu_into(hb, wg_ref, wu_ref, wd_ref, y_ref)

    @pl.when(jnp.logical_not(used))
    def _():
        y_ref[...] = jnp.zeros_like(y_ref)


def _experts_call(tile_exp, n_valid, xs, wg, wu, wd):
    n_tiles = tile_exp.shape[0]
    tm = EXPERT_TILE
    weight = lambda shape: pl.BlockSpec((1,) + shape, lambda i, ex, nv: (ex[i], 0, 0))
    return pl.pallas_call(
        _experts_kernel,
        grid_spec=pltpu.PrefetchScalarGridSpec(
            num_scalar_prefetch=2,
            grid=(n_tiles,),
            in_specs=[pl.BlockSpec((tm, HALF), lambda i, ex, nv: (i, 0)),
                      weight((D_MODEL, FFN_DIM)), weight((D_MODEL, FFN_DIM)), weight((FFN_DIM, D_MODEL))],
            out_specs=pl.BlockSpec((tm, D_MODEL), lambda i, ex, nv: (i, 0)),
        ),
        out_shape=jax.ShapeDtypeStruct((xs.shape[0], D_MODEL), F32),
        compiler_params=_params(("arbitrary",)),
        name="moe_experts",
    )(tile_exp, n_valid, xs, wg, wu, wd)


def _combine_kernel(d1_ref, d2_ref, y_ref, info_ref, x_ref, g2_ref, post_ref, o_ref, ybuf, sem):
    tm = x_ref.shape[1]
    base = (pl.program_id(0) * pl.num_programs(1) + pl.program_id(1)) * tm

    def issue(r, carry):
        _row_copy(y_ref, d1_ref[base + r], ybuf.at[0], r, sem).start()
        _row_copy(y_ref, d2_ref[base + r], ybuf.at[1], r, sem).start()
        return carry

    lax.fori_loop(0, tm, issue, 0)
    for k in range(2):
        pltpu.make_async_copy(y_ref.at[pl.ds(0, tm)], ybuf.at[k], sem).wait()

    info = info_ref[0]
    f = info[:, INFO_P1:INFO_P1 + 1] * ybuf[0] + info[:, INFO_P2:INFO_P2 + 1] * ybuf[1]
    o_ref[0] = x_ref[0] + g2_ref[0] * (_rms(f) * post_ref[...])


def _combine_call(dest1, dest2, y, info, x, mod, post_g, tm):
    nb, seq, _ = x.shape
    per_row = mod.shape[1] != 1
    return pl.pallas_call(
        _combine_kernel,
        grid_spec=pltpu.PrefetchScalarGridSpec(
            num_scalar_prefetch=2,
            grid=(nb, seq // tm),
            in_specs=[pl.BlockSpec(memory_space=pl.ANY),
                      pl.BlockSpec((1, tm, LANES), lambda b, i, d1, d2: (b, i, 0)),
                      pl.BlockSpec((1, tm, D_MODEL), lambda b, i, d1, d2: (b, i, 0)),
                      pl.BlockSpec((1, tm if per_row else 1, D_MODEL),
                                   lambda b, i, d1, d2: (b, i if per_row else 0, MOD_G2)),
                      pl.BlockSpec(post_g.shape, lambda b, i, d1, d2: (0, 0))],
            out_specs=pl.BlockSpec((1, tm, D_MODEL), lambda b, i, d1, d2: (b, i, 0)),
            scratch_shapes=[pltpu.VMEM((2, tm, D_MODEL), F32), pltpu.SemaphoreType.DMA(())],
        ),
        out_shape=jax.ShapeDtypeStruct(x.shape, F32),
        compiler_params=_params(("arbitrary", "arbitrary")),
        name="moe_combine",
    )(dest1, dest2, y, info, x, mod, post_g)


def _moe_layer(xp, xs, mod_p, mod_s, pre_g, post_g, router_w, router_b, wg, wu, wd):
    rw = jnp.pad(router_w, ((0, 0), (0, LANES - N_EXPERTS)))
    rb = _pad_lanes(router_b)
    hw_p, info_p, cnt_p = _route_call(xp, mod_p, pre_g, rw, rb, jnp.zeros((1, LANES), F32), tm=512)
    hw_s, info_s, cnt = _route_call(xs, mod_s, pre_g, rw, rb, cnt_p, tm=xs.shape[1])

    tm = EXPERT_TILE
    n_assign = 2 * (xp.shape[0] * xp.shape[1] + xs.shape[1])
    n_tiles = n_assign // tm + N_EXPERTS
    counts = cnt[0, :N_EXPERTS].astype(jnp.int32)
    group_tiles = (counts + tm - 1) // tm
    tile_end = jnp.cumsum(group_tiles)
    start = (tile_end - group_tiles) * tm
    n_valid = tile_end[-1]
    tile_id = jnp.minimum(jnp.arange(n_tiles, dtype=jnp.int32), n_valid - 1)
    tile_exp = jnp.sum(tile_id[:, None] >= tile_end[None, :], axis=1).astype(jnp.int32)

    def dests(info):
        flat = info.reshape(-1, LANES)
        d = [start[flat[:, i].astype(jnp.int32)] + flat[:, r].astype(jnp.int32)
             for i, r in ((INFO_I1, INFO_R1), (INFO_I2, INFO_R2))]
        return d[0], d[1]

    dp = dests(info_p)
    ds = dests(info_s)
    slots = jnp.zeros((n_tiles * tm, HALF), jnp.uint32)
    slots = _dispatch_call(dp[0], dp[1], hw_p.reshape(-1, HALF), slots, tm=512)
    slots = _dispatch_call(ds[0], ds[1], hw_s.reshape(-1, HALF), slots, tm=xs.shape[1])
    y = _experts_call(tile_exp, n_valid.reshape(1), slots, wg, wu, wd)
    xp = _combine_call(dp[0], dp[1], y, info_p, xp, mod_p, post_g, tm=256)
    xs = _combine_call(ds[0], ds[1], y, info_s, xs, mod_s, post_g, tm=xs.shape[1])
    return xp, xs


def _pad_lanes(v):
    return jnp.pad(v, (0, LANES - v.shape[0])).reshape(1, LANES)


def kernel(x_prompt, x_sample, state_ssm, state_conv, c_prompt, c_sample, w_mod, b_mod, mix_pre_g, mix_post_g, ffn_pre_g, ffn_post_g, w_in, conv_w, conv_b, dt_bias, a_log, d_skip, ssm_norm_g, w_ssd_out, cmlp_ln_g, cmlp_ln_b, w_spatial, b_spatial, w_cmlp_out, w_o, ffn_wg, ffn_wu, ffn_wd, router_w, router_b, exp_wg, exp_wu, exp_wd):
    n_prompt = x_prompt.shape[0]
    n_sample = x_sample.shape[0]

    c_all = jnp.concatenate([c_prompt, c_sample, jnp.zeros((8, D_MODEL), F32)], axis=0)
    mod_all = _mod_call(c_all, w_mod, b_mod)
    mod_p = mod_all[:, :n_prompt].reshape(DEPTH, n_prompt, 1, 6 * D_MODEL)
    mod_s = mod_all[:, n_prompt:n_prompt + n_sample].reshape(DEPTH, 1, n_sample, 6 * D_MODEL)

    xp = x_prompt
    xs = x_sample.reshape(1, n_sample, D_MODEL)
    conv_hist = state_conv.reshape(DEPTH, n_sample, (CONV_WIDTH - 1) * CONV_DIM)

    o_xbc = SSM_INNER
    o_dt = o_xbc + CONV_DIM
    o_u = o_dt + SSM_HEADS

    ssm_p, conv_p, ssm_s, conv_s, v_s = [], [], [], [], []
    for l in range(DEPTH):
        w = w_in[l]
        w_main = jnp.concatenate([w[:, :o_xbc], w[:, o_u:], w[:, o_xbc:o_dt]], axis=1).astype(BF16)
        w_dt = jnp.pad(w[:, o_dt:o_u], ((0, 0), (0, LANES - SSM_HEADS))).astype(BF16)
        row = lambda a: a[l].reshape(1, -1)
        mixer_small = (conv_w[l], row(conv_b), _pad_lanes(dt_bias[l]), _pad_lanes(a_log[l]),
                       jnp.repeat(d_skip[l], SSM_HEAD_DIM).reshape(1, -1), row(ssm_norm_g),
                       row(cmlp_ln_g), row(cmlp_ln_b))
        wa = w_ssd_out[l].astype(BF16)
        wb = w_cmlp_out[l].astype(BF16)
        wo = w_o[l].astype(BF16)

        proj_p, dt_p = _inproj_call(xp, mod_p[l], row(mix_pre_g), w_main, w_dt, tm=1024)
        proj_s, dt_s = _inproj_call(xs, mod_s[l], row(mix_pre_g), w_main, w_dt, tm=n_sample)
        ya_p, yb_p, hs_p, cs_p = _mixer_call(proj_p, dt_p, *mixer_small, w_spatial[l], b_spatial[l].T)
        ya_s, yb_s, vr_s, hs_s, cs_s = _mixer_step_call(
            l, proj_s[0], dt_s[0], state_ssm, conv_hist[l], *mixer_small,
            jnp.repeat(w_spatial[l, :, 0, 0], CMLP_GROUP_DIM).reshape(1, -1),
            jnp.repeat(b_spatial[l, :, 0], CMLP_GROUP_DIM).reshape(1, -1))
        xp = _outproj_call(ya_p, yb_p, proj_p, xp, mod_p[l], row(mix_post_g), wa, wb, wo, tm=512)
        xs = _outproj_call(ya_s[None], yb_s[None], proj_s, xs, mod_s[l], row(mix_post_g), wa, wb, wo,
                           tm=n_sample)

        j = l // 2
        if l % 2 == 0:
            wg, wu, wd = (a[j][None].astype(BF16) for a in (ffn_wg, ffn_wu, ffn_wd))
            xp = _dense_ffn_call(xp, mod_p[l], row(ffn_pre_g), row(ffn_post_g), wg, wu, wd, tm=512)
            xs = _dense_ffn_call(xs, mod_s[l], row(ffn_pre_g), row(ffn_post_g), wg, wu, wd, tm=n_sample)
        else:
            wg, wu, wd = (a[j].astype(BF16) for a in (exp_wg, exp_wu, exp_wd))
            xp, xs = _moe_layer(xp, xs, mod_p[l], mod_s[l], row(ffn_pre_g), row(ffn_post_g),
                                router_w[j], router_b[j], wg, wu, wd)

        ssm_p.append(hs_p)
        conv_p.append(cs_p)
        ssm_s.append(hs_s)
        conv_s.append(cs_s.reshape(n_sample, CONV_WIDTH - 1, CONV_DIM))
        v_s.append(vr_s.reshape(n_sample, 1, D_MODEL))

    return (xp, xs.reshape(n_sample, 1, D_MODEL), jnp.stack(ssm_p), jnp.stack(conv_p),
            jnp.stack(ssm_s), jnp.stack(conv_s), jnp.stack(v_s))
```

```python
import functools

import jax
import jax.numpy as jnp
from jax import lax
from jax.experimental import pallas as pl
from jax.experimental.pallas import tpu as pltpu

F32 = jnp.float32
BF16 = jnp.bfloat16

D_MODEL = 1024
DEPTH = 4
SSM_HEADS = 16
SSM_HEAD_DIM = 64
SSM_GROUPS = 2
SSM_STATE = 128
SSM_INNER = 1024
GROUP_WIDTH = SSM_INNER // SSM_GROUPS
HEADS_PER_GROUP = SSM_HEADS // SSM_GROUPS
CONV_WIDTH = 4
CONV_DIM = 1536
CHUNK = 128
CMLP_GROUPS = 8
CMLP_GROUP_DIM = 128
FFN_DIM = 2816
N_EXPERTS = 8
EPS = 1e-6
LANES = 128
COL_Z, COL_U, COL_V, COL_GA, COL_GB, COL_XS = 0, 1, 2, 3, 4, 5
COL_BC_512 = 12
PROJ_MAIN = 6 * D_MODEL + 2 * SSM_GROUPS * SSM_STATE
PROJ_TILE = 1664
MOD_SH1, MOD_SC1, MOD_G1, MOD_SH2, MOD_SC2, MOD_G2 = range(6)
VMEM_LIMIT = 56 * 1024 * 1024


def _params(semantics, row_dma=False):
    return pltpu.CompilerParams(dimension_semantics=semantics, vmem_limit_bytes=VMEM_LIMIT,
                                disable_bounds_checks=row_dma)


def _rms(x):
    return x * lax.rsqrt(jnp.mean(x * x, axis=-1, keepdims=True) + EPS)


def _dot(a, b):
    return jnp.dot(a, b, preferred_element_type=F32)


def _dot_nt(a, b):
    return lax.dot_general(a, b, (((1,), (1,)), ((), ())), preferred_element_type=F32)


def _dot_tn(a, b):
    return lax.dot_general(a, b, (((0,), (0,)), ((), ())), preferred_element_type=F32)


def _mod_kernel(c_ref, w_ref, b_ref, o_ref):
    a = jax.nn.silu(c_ref[...]).astype(BF16)
    o_ref[0] = _dot(a, w_ref[0].astype(BF16)) + b_ref[0]


def _mod_call(c_all, w_mod, b_mod):
    rows = c_all.shape[0]
    return pl.pallas_call(
        _mod_kernel,
        grid=(DEPTH, 6),
        in_specs=[
            pl.BlockSpec((rows, D_MODEL), lambda l, j: (0, 0)),
            pl.BlockSpec((1, D_MODEL, D_MODEL), lambda l, j: (l, 0, j)),
            pl.BlockSpec((1, 1, D_MODEL), lambda l, j: (l, 0, j)),
        ],
        out_specs=pl.BlockSpec((1, rows, D_MODEL), lambda l, j: (l, 0, j)),
        out_shape=jax.ShapeDtypeStruct((DEPTH, rows, 6 * D_MODEL), F32),
        compiler_params=_params(("arbitrary", "arbitrary")),
        name="adaln_mod",
    )(c_all, w_mod, b_mod.reshape(DEPTH, 1, 6 * D_MODEL))


def _mod_spec(mod, tm, seg, grid_rank):
    per_row = mod.shape[1] != 1
    rows = tm if per_row else 1
    if grid_rank == 2:
        return pl.BlockSpec((1, rows, D_MODEL), lambda b, i: (b, i if per_row else 0, seg))
    return pl.BlockSpec((1, rows, D_MODEL), lambda b, i, j: (b, i if per_row else 0, seg))


def _inproj_kernel(x_ref, g_ref, sh_ref, sc_ref, w_ref, wdt_ref, p_ref, pdt_ref, h_scr):
    @pl.when(pl.program_id(2) == 0)
    def _():
        h = _rms(x_ref[0]) * g_ref[...] * (1.0 + sc_ref[0]) + sh_ref[0]
        hb = h.astype(BF16)
        h_scr[...] = hb
        pdt_ref[0] = _dot(hb, wdt_ref[...])

    p_ref[0] = _dot(h_scr[...], w_ref[...])


def _inproj_call(layer, x, mod, g, w_main, w_dt, tm):
    nb, seq, _ = x.shape
    grid = (nb, seq // tm, PROJ_MAIN // PROJ_TILE)
    return pl.pallas_call(
        _inproj_kernel,
        grid=grid,
        in_specs=[
            pl.BlockSpec((1, tm, D_MODEL), lambda b, i, j: (b, i, 0)),
            pl.BlockSpec((1, D_MODEL), lambda b, i, j: (0, 0)),
            _mod_spec(mod, tm, MOD_SH1, 3),
            _mod_spec(mod, tm, MOD_SC1, 3),
            pl.BlockSpec((None, D_MODEL, PROJ_TILE), lambda b, i, j: (layer, 0, j)),
            pl.BlockSpec((None, D_MODEL, LANES), lambda b, i, j: (layer, 0, 0)),
        ],
        out_specs=[
            pl.BlockSpec((1, tm, PROJ_TILE), lambda b, i, j: (b, i, j)),
            pl.BlockSpec((1, tm, LANES), lambda b, i, j: (b, i, 0)),
        ],
        out_shape=[
            jax.ShapeDtypeStruct((nb, seq, PROJ_MAIN), F32),
            jax.ShapeDtypeStruct((nb, seq, LANES), F32),
        ],
        scratch_shapes=[pltpu.VMEM((tm, D_MODEL), BF16)],
        compiler_params=_params(("arbitrary", "arbitrary", "arbitrary")),
        name="in_proj",
    )(x, g, mod, mod, w_main, w_dt)


def _gated_group_norm(y, z, norm_g):
    y = y * jax.nn.silu(z)
    parts = [_rms(y[:, g * GROUP_WIDTH:(g + 1) * GROUP_WIDTH]) for g in range(SSM_GROUPS)]
    return jnp.concatenate(parts, axis=-1) * norm_g


def _layernorm(x, g, b):
    mu = jnp.mean(x, axis=-1, keepdims=True)
    xc = x - mu
    return xc * lax.rsqrt(jnp.mean(xc * xc, axis=-1, keepdims=True) + EPS) * g + b


def _pair_columns(v, pair, lane_lo):
    h0 = 2 * pair
    return jnp.where(lane_lo, v[:, h0:h0 + 1], v[:, h0 + 1:h0 + 2])


def _mixer_kernel(z_ref, u_ref, v_ref, xs_ref, bc_ref, dt_ref,
                  cw_ref, cb_ref, dtb_ref, alog_ref, dskip_ref, ng_ref,
                  lng_ref, lnb_ref, wsp_ref, bsp_ref,
                  ya_ref, yb_ref, ssm_ref, conv_ref,
                  h_scr, cbuf, wsp_scr):
    c = pl.program_id(1)
    row = lax.broadcasted_iota(jnp.int32, (CHUNK, CHUNK), 0)
    col = lax.broadcasted_iota(jnp.int32, (CHUNK, CHUNK), 1)
    causal = row >= col
    lane_lo = col < SSM_HEAD_DIM

    @pl.when(c == 0)
    def _():
        h_scr[...] = jnp.zeros_like(h_scr)
        cbuf[0:8, :] = jnp.zeros((8, CONV_DIM), F32)
        for g in range(CMLP_GROUPS):
            wsp_scr[g] = jnp.where(causal, wsp_ref[g], 0.0).astype(BF16)

    cbuf[8:8 + CHUNK, 0:SSM_INNER] = xs_ref[0]
    cbuf[8:8 + CHUNK, SSM_INNER:CONV_DIM] = bc_ref[0]
    acc = cbuf[5:5 + CHUNK, :] * cw_ref[0:1, :]
    for k in range(1, CONV_WIDTH):
        acc = acc + cbuf[5 + k:5 + k + CHUNK, :] * cw_ref[k:k + 1, :]
    tail = cbuf[5 + CHUNK:8 + CHUNK, :]
    cbuf[5:8, :] = tail

    @pl.when(c == pl.num_programs(1) - 1)
    def _():
        conv_ref[0] = tail

    xbc = jax.nn.silu(acc + cb_ref[...])
    xc = xbc[:, 0:SSM_INNER]
    bm = xbc[:, SSM_INNER:SSM_INNER + SSM_GROUPS * SSM_STATE].astype(BF16)
    cm = xbc[:, SSM_INNER + SSM_GROUPS * SSM_STATE:CONV_DIM].astype(BF16)

    dt = jax.nn.softplus(dt_ref[0] + dtb_ref[...])
    da = dt * (-jnp.exp(alog_ref[...]))
    cum = jnp.dot(causal.astype(F32), da, preferred_element_type=F32,
                  precision=lax.Precision.HIGHEST)
    last = cum[CHUNK - 1:CHUNK, :]
    ecum = jnp.exp(cum)
    wend = jnp.exp(last - cum) * dt
    elast = jnp.exp(last)
    cum_t = cum.T
    dt_t = dt.T

    y_pairs = []
    for g in range(SSM_GROUPS):
        bg = bm[:, g * SSM_STATE:(g + 1) * SSM_STATE]
        cg = cm[:, g * SSM_STATE:(g + 1) * SSM_STATE]
        cb = _dot_nt(cg, bg)
        hg = h_scr[g * GROUP_WIDTH:(g + 1) * GROUP_WIDTH, :]
        y_state = _dot_nt(cg, hg.astype(BF16))
        xw_parts = []
        for q in range(HEADS_PER_GROUP // 2):
            pair = g * (HEADS_PER_GROUP // 2) + q
            mixes = []
            for h in (2 * pair, 2 * pair + 1):
                seg = cum[:, h:h + 1] - cum_t[h:h + 1, :]
                decay = jnp.where(causal, jnp.exp(seg), 0.0)
                mixes.append((cb * decay * dt_t[h:h + 1, :]).astype(BF16))
            xp = xc[:, pair * LANES:(pair + 1) * LANES]
            rhs = jnp.concatenate([jnp.where(lane_lo, xp, 0.0), jnp.where(lane_lo, 0.0, xp)],
                                  axis=0).astype(BF16)
            y_in = _dot(jnp.concatenate(mixes, axis=1), rhs)
            y_st = y_state[:, q * LANES:(q + 1) * LANES] * _pair_columns(ecum, pair, lane_lo)
            y_pairs.append(y_in + y_st)
            xw_parts.append((xp * _pair_columns(wend, pair, lane_lo)).astype(BF16))
        upd = _dot_tn(jnp.concatenate(xw_parts, axis=1), bg)
        for r in range(HEADS_PER_GROUP):
            h = g * HEADS_PER_GROUP + r
            rows = slice(h * SSM_HEAD_DIM, (h + 1) * SSM_HEAD_DIM)
            scale = jnp.broadcast_to(elast[0:1, h:h + 1], (SSM_HEAD_DIM, SSM_STATE))
            h_scr[rows, :] = h_scr[rows, :] * scale + upd[r * SSM_HEAD_DIM:(r + 1) * SSM_HEAD_DIM, :]

    y = jnp.concatenate(y_pairs, axis=1) + dskip_ref[...] * xc
    ya_ref[0] = _gated_group_norm(y, z_ref[0], ng_ref[...]).astype(BF16)

    @pl.when(c == pl.num_programs(1) - 1)
    def _():
        ssm_ref[0] = h_scr[...].reshape(SSM_HEADS, SSM_HEAD_DIM, SSM_STATE)

    ug = jax.nn.gelu(u_ref[0], approximate=True)
    vn = _layernorm(jax.nn.gelu(v_ref[0], approximate=True), lng_ref[...], lnb_ref[...])
    gates = []
    for g in range(CMLP_GROUPS):
        vg = vn[:, g * CMLP_GROUP_DIM:(g + 1) * CMLP_GROUP_DIM].astype(BF16)
        gates.append(_dot(wsp_scr[g], vg) + bsp_ref[:, g:g + 1])
    yb_ref[0] = (ug * jnp.concatenate(gates, axis=1)).astype(BF16)


def _mixer_call(proj, proj_dt, conv_w, conv_b, dt_bias, a_log, d_skip, norm_g, ln_g, ln_b, w_sp, b_sp_t):
    nb, seq, _ = proj.shape
    nchunks = seq // CHUNK

    def col(block, width=D_MODEL):
        return pl.BlockSpec((1, CHUNK, width), lambda b, c: (b, c, block))

    def whole(a):
        zeros = (0,) * a.ndim
        return pl.BlockSpec(a.shape, lambda b, c: zeros)

    small = (conv_w, conv_b, dt_bias, a_log, d_skip, norm_g, ln_g, ln_b, w_sp, b_sp_t)
    return pl.pallas_call(
        _mixer_kernel,
        grid=(nb, nchunks),
        in_specs=[col(COL_Z), col(COL_U), col(COL_V), col(COL_XS),
                  col(COL_BC_512, 2 * SSM_GROUPS * SSM_STATE),
                  pl.BlockSpec((1, CHUNK, LANES), lambda b, c: (b, c, 0))]
                 + [whole(a) for a in small],
        out_specs=[
            pl.BlockSpec((1, CHUNK, SSM_INNER), lambda b, c: (b, c, 0)),
            pl.BlockSpec((1, CHUNK, D_MODEL), lambda b, c: (b, c, 0)),
            pl.BlockSpec((1, SSM_HEADS, SSM_HEAD_DIM, SSM_STATE), lambda b, c: (b, 0, 0, 0)),
            pl.BlockSpec((1, CONV_WIDTH - 1, CONV_DIM), lambda b, c: (b, 0, 0)),
        ],
        out_shape=[
            jax.ShapeDtypeStruct((nb, seq, SSM_INNER), BF16),
            jax.ShapeDtypeStruct((nb, seq, D_MODEL), BF16),
            jax.ShapeDtypeStruct((nb, SSM_HEADS, SSM_HEAD_DIM, SSM_STATE), F32),
            jax.ShapeDtypeStruct((nb, CONV_WIDTH - 1, CONV_DIM), F32),
        ],
        scratch_shapes=[
            pltpu.VMEM((SSM_INNER, SSM_STATE), F32),
            pltpu.VMEM((CHUNK + 8, CONV_DIM), F32),
            pltpu.VMEM((CMLP_GROUPS, CHUNK, CHUNK), BF16),
        ],
        compiler_params=_params(("arbitrary", "arbitrary")),
        name="mixer_prompt",
    )(proj, proj, proj, proj, proj, proj_dt, *small)


SAMPLE_TILE = 16


def _mixer_step_kernel(z_ref, u_ref, v_ref, xs_ref, bc_ref, dt_ref, ssm_in_ref, conv_in_ref,
                       cw_ref, cb_ref, dtb_ref, alog_ref, dskip_ref, ng_ref,
                       lng_ref, lnb_ref, wsp0_ref, bsp0_ref, ssm_all_ref,
                       ya_ref, yb_ref, vout_ref, ssm_ref, conv_ref):
    del ssm_all_ref
    tb = SAMPLE_TILE
    xbc_new = jnp.concatenate([xs_ref[...], bc_ref[...]], axis=1)
    hist = conv_in_ref[...]
    acc = xbc_new * cw_ref[CONV_WIDTH - 1:CONV_WIDTH, :]
    for k in range(CONV_WIDTH - 1):
        acc = acc + hist[:, k * CONV_DIM:(k + 1) * CONV_DIM] * cw_ref[k:k + 1, :]
    conv_ref[:, 0:(CONV_WIDTH - 2) * CONV_DIM] = hist[:, CONV_DIM:]
    conv_ref[:, (CONV_WIDTH - 2) * CONV_DIM:] = xbc_new

    xbc = jax.nn.silu(acc + cb_ref[...])
    xc = xbc[:, 0:SSM_INNER]
    bm = xbc[:, SSM_INNER:SSM_INNER + SSM_GROUPS * SSM_STATE]
    cm = xbc[:, SSM_INNER + SSM_GROUPS * SSM_STATE:CONV_DIM]
    dt = jax.nn.softplus(dt_ref[...] + dtb_ref[...])
    dec = jnp.exp(dt * (-jnp.exp(alog_ref[...])))

    def transposed(a):
        pad = jnp.zeros((LANES - tb, a.shape[1]), F32)
        return jnp.concatenate([a, pad], axis=0).T

    xc_t = transposed(xc)
    dt_t = transposed(dt)
    dec_t = transposed(dec)
    row_id = lax.broadcasted_iota(jnp.int32, (tb, SSM_STATE), 0)

    y_groups = [jnp.zeros((tb, GROUP_WIDTH), F32) for _ in range(SSM_GROUPS)]
    for b in range(tb):
        for g in range(SSM_GROUPS):
            b_row = bm[b:b + 1, g * SSM_STATE:(g + 1) * SSM_STATE]
            c_only = jnp.where(row_id == b, cm[:, g * SSM_STATE:(g + 1) * SSM_STATE], 0.0).astype(BF16)
            new_heads = []
            for r in range(HEADS_PER_GROUP):
                h = g * HEADS_PER_GROUP + r
                x_col = xc_t[h * SSM_HEAD_DIM:(h + 1) * SSM_HEAD_DIM, b:b + 1]
                push = x_col * dt_t[h:h + 1, b:b + 1]
                keep = jnp.broadcast_to(dec_t[h:h + 1, b:b + 1], (SSM_HEAD_DIM, SSM_STATE))
                h_new = ssm_in_ref[b, h] * keep + push * b_row
                ssm_ref[b, h] = h_new
                new_heads.append(h_new.astype(BF16))
            hg = jnp.concatenate(new_heads, axis=0)
            y_groups[g] = y_groups[g] + _dot_nt(c_only, hg)

    y = jnp.concatenate(y_groups, axis=1) + dskip_ref[...] * xc
    ya_ref[...] = _gated_group_norm(y, z_ref[...], ng_ref[...]).astype(BF16)

    ug = jax.nn.gelu(u_ref[...], approximate=True)
    vn = _layernorm(jax.nn.gelu(v_ref[...], approximate=True), lng_ref[...], lnb_ref[...])
    vout_ref[...] = vn
    yb_ref[...] = (ug * (vn * wsp0_ref[...] + bsp0_ref[...])).astype(BF16)


def _mixer_step_call(layer, proj, proj_dt, state_ssm, conv_hist, ssm_all, conv_w, conv_b, dt_bias, a_log,
                     d_skip, norm_g, ln_g, ln_b, w_sp0, b_sp0):
    nseq = proj.shape[0]
    tb = SAMPLE_TILE

    def col(block, width=D_MODEL):
        return pl.BlockSpec((tb, width), lambda i: (i, block))

    def whole(a):
        zeros = (0,) * a.ndim
        return pl.BlockSpec(a.shape, lambda i: zeros)

    small = (conv_w, conv_b, dt_bias, a_log, d_skip, norm_g, ln_g, ln_b, w_sp0, b_sp0)
    hist_width = (CONV_WIDTH - 1) * CONV_DIM
    return pl.pallas_call(
        _mixer_step_kernel,
        grid=(nseq // tb,),
        in_specs=[col(COL_Z), col(COL_U), col(COL_V), col(COL_XS),
                  col(COL_BC_512, 2 * SSM_GROUPS * SSM_STATE),
                  pl.BlockSpec((tb, LANES), lambda i: (i, 0)),
                  pl.BlockSpec((None, tb, SSM_HEADS, SSM_HEAD_DIM, SSM_STATE),
                               lambda i: (layer, i, 0, 0, 0)),
                  pl.BlockSpec((tb, hist_width), lambda i: (i, 0))]
                 + [whole(a) for a in small] + [pl.BlockSpec(memory_space=pl.ANY)],
        out_specs=[
            pl.BlockSpec((tb, SSM_INNER), lambda i: (i, 0)),
            pl.BlockSpec((tb, D_MODEL), lambda i: (i, 0)),
            pl.BlockSpec((tb, D_MODEL), lambda i: (i, 0)),
            pl.BlockSpec((None, tb, SSM_HEADS, SSM_HEAD_DIM, SSM_STATE), lambda i: (layer, i, 0, 0, 0)),
            pl.BlockSpec((tb, hist_width), lambda i: (i, 0)),
        ],
        out_shape=[
            jax.ShapeDtypeStruct((nseq, SSM_INNER), BF16),
            jax.ShapeDtypeStruct((nseq, D_MODEL), BF16),
            jax.ShapeDtypeStruct((nseq, D_MODEL), F32),
            jax.ShapeDtypeStruct(ssm_all.shape, F32),
            jax.ShapeDtypeStruct((nseq, hist_width), F32),
        ],
        input_output_aliases={8 + len(small): 3},
        compiler_params=_params(("arbitrary",)),
        name="mixer_sample",
    )(proj, proj, proj, proj, proj, proj_dt, state_ssm, conv_hist, *small, ssm_all)


def _outproj_kernel(ya_ref, yb_ref, ga_ref, gb_ref, x_ref, g1_ref, pg_ref, wa_ref, wb_ref, wo_ref, o_ref):
    merged = (jax.nn.sigmoid(ga_ref[0]) * _dot(ya_ref[0], wa_ref[...])
              + jax.nn.sigmoid(gb_ref[0]) * _dot(yb_ref[0], wb_ref[...]))
    o = _dot(merged.astype(BF16), wo_ref[...])
    o_ref[0] = x_ref[0] + g1_ref[0] * (_rms(o) * pg_ref[...])


def _outproj_call(layer, ya, yb, proj, x, mod, post_g, wa, wb, wo, tm):
    nb, seq, _ = x.shape

    def rows(block=0):
        return pl.BlockSpec((1, tm, D_MODEL), lambda b, i: (b, i, block))

    def whole(a):
        return pl.BlockSpec(a.shape, lambda b, i: (0, 0))

    def of_layer(a):
        return pl.BlockSpec((None,) + a.shape[1:], lambda b, i: (layer, 0, 0))

    return pl.pallas_call(
        _outproj_kernel,
        grid=(nb, seq // tm),
        in_specs=[rows(), rows(), rows(COL_GA), rows(COL_GB), rows(),
                  _mod_spec(mod, tm, MOD_G1, 2), whole(post_g), of_layer(wa), of_layer(wb), of_layer(wo)],
        out_specs=rows(),
        out_shape=jax.ShapeDtypeStruct(x.shape, F32),
        compiler_params=_params(("arbitrary", "arbitrary")),
        name="out_proj",
    )(ya, yb, proj, proj, x, mod, post_g, wa, wb, wo)


FFN_CHUNK = 256


def _swiglu_into(hb, wg_ref, wu_ref, wd_ref, acc_ref):
    for f in range(FFN_DIM // FFN_CHUNK):
        cols = slice(f * FFN_CHUNK, (f + 1) * FFN_CHUNK)
        act = jax.nn.silu(_dot(hb, wg_ref[0, :, cols])) * _dot(hb, wu_ref[0, :, cols])
        part = _dot(act.astype(BF16), wd_ref[0, cols, :])
        if f == 0:
            acc_ref[...] = part
        else:
            acc_ref[...] += part


def _premod(x_ref, pre_ref, sh_ref, sc_ref):
    return _rms(x_ref[0]) * pre_ref[...] * (1.0 + sc_ref[0]) + sh_ref[0]


def _dense_ffn_kernel(x_ref, pre_ref, post_ref, sh_ref, sc_ref, g2_ref, wg_ref, wu_ref, wd_ref,
                      o_ref, acc_scr):
    hb = _premod(x_ref, pre_ref, sh_ref, sc_ref).astype(BF16)
    _swiglu_into(hb, wg_ref, wu_ref, wd_ref, acc_scr)
    o_ref[0] = x_ref[0] + g2_ref[0] * (_rms(acc_scr[...]) * post_ref[...])


def _dense_ffn_call(j, x, mod, pre_g, post_g, wg, wu, wd, tm):
    nb, seq, _ = x.shape

    def whole(a):
        zeros = (0,) * a.ndim
        return pl.BlockSpec(a.shape, lambda b, i: zeros)

    def of_layer(a):
        return pl.BlockSpec((1,) + a.shape[1:], lambda b, i: (j, 0, 0))

    rows = pl.BlockSpec((1, tm, D_MODEL), lambda b, i: (b, i, 0))
    return pl.pallas_call(
        _dense_ffn_kernel,
        grid=(nb, seq // tm),
        in_specs=[rows, whole(pre_g), whole(post_g),
                  _mod_spec(mod, tm, MOD_SH2, 2), _mod_spec(mod, tm, MOD_SC2, 2),
                  _mod_spec(mod, tm, MOD_G2, 2), of_layer(wg), of_layer(wu), of_layer(wd)],
        out_specs=rows,
        out_shape=jax.ShapeDtypeStruct(x.shape, F32),
        scratch_shapes=[pltpu.VMEM((tm, D_MODEL), F32)],
        compiler_params=_params(("arbitrary", "arbitrary")),
        name="dense_ffn",
    )(x, pre_g, post_g, mod, mod, mod, wg, wu, wd)


EXPERT_TILE = 512
HALF = D_MODEL // 2
INFO_I1, INFO_I2, INFO_P1, INFO_P2, INFO_R1, INFO_R2 = range(6)
HI_MASK = 0xFFFF0000
ISSUE_UNROLL = 8


def _route_kernel(x_ref, pre_ref, sh_ref, sc_ref, rw_ref, rb_ref, cnt0_ref,
                  hw_ref, info_ref, cnt_ref, carry):
    @pl.when(jnp.logical_and(pl.program_id(0) == 0, pl.program_id(1) == 0))
    def _():
        carry[...] = cnt0_ref[...]

    hb = _premod(x_ref, pre_ref, sh_ref, sc_ref).astype(BF16)
    tm = hb.shape[0]
    bits = lax.bitcast_convert_type(hb.astype(F32), jnp.uint32)
    hw_ref[0] = (bits[:, HALF:] & jnp.uint32(HI_MASK)) | (bits[:, :HALF] >> jnp.uint32(16))

    logits = _dot(hb, rw_ref[...].astype(BF16)) + rb_ref[...]
    lane = lax.broadcasted_iota(jnp.int32, logits.shape, 1).astype(F32)
    neg = jnp.float32(-jnp.inf)
    logits = jnp.where(lane < N_EXPERTS, logits, neg)
    m1 = jnp.max(logits, axis=-1, keepdims=True)
    i1 = jnp.min(jnp.where(logits == m1, lane, float(LANES)), axis=-1, keepdims=True)
    rest = jnp.where(lane == i1, neg, logits)
    m2 = jnp.max(rest, axis=-1, keepdims=True)
    i2 = jnp.min(jnp.where(rest == m2, lane, float(LANES)), axis=-1, keepdims=True)
    e2 = jnp.exp(m2 - m1)
    p1 = 1.0 / (1.0 + e2)
    p2 = e2 * p1

    member = jnp.logical_or(lane == i1, lane == i2)
    row = lax.broadcasted_iota(jnp.int32, (tm, tm), 0)
    col = lax.broadcasted_iota(jnp.int32, (tm, tm), 1)
    before = jnp.where(row > col, 1.0, 0.0).astype(BF16)
    ones = jnp.where(member, 1.0, 0.0)
    prior = _dot(before, ones.astype(BF16)) + carry[...]
    r1 = jnp.sum(jnp.where(lane == i1, prior, 0.0), axis=-1, keepdims=True)
    r2 = jnp.sum(jnp.where(lane == i2, prior, 0.0), axis=-1, keepdims=True)
    carry[...] += jnp.sum(ones, axis=0, keepdims=True)
    cnt_ref[...] = carry[...]

    info = jnp.zeros_like(logits)
    for k, v in ((INFO_I1, i1), (INFO_I2, i2), (INFO_P1, p1), (INFO_P2, p2), (INFO_R1, r1), (INFO_R2, r2)):
        info = jnp.where(lane == float(k), v, info)
    info_ref[0] = info


def _route_call(x, mod, pre_g, rw, rb, cnt0, tm):
    nb, seq, _ = x.shape

    def whole(a):
        return pl.BlockSpec(a.shape, lambda b, i: (0, 0))

    return pl.pallas_call(
        _route_kernel,
        grid=(nb, seq // tm),
        in_specs=[pl.BlockSpec((1, tm, D_MODEL), lambda b, i: (b, i, 0)), whole(pre_g),
                  _mod_spec(mod, tm, MOD_SH2, 2), _mod_spec(mod, tm, MOD_SC2, 2),
                  whole(rw), whole(rb), whole(cnt0)],
        out_specs=[pl.BlockSpec((1, tm, HALF), lambda b, i: (b, i, 0)),
                   pl.BlockSpec((1, tm, LANES), lambda b, i: (b, i, 0)),
                   pl.BlockSpec((1, LANES), lambda b, i: (0, 0))],
        out_shape=[jax.ShapeDtypeStruct((nb, seq, HALF), jnp.uint32),
                   jax.ShapeDtypeStruct((nb, seq, LANES), F32),
                   jax.ShapeDtypeStruct((1, LANES), F32)],
        scratch_shapes=[pltpu.VMEM((1, LANES), F32)],
        compiler_params=_params(("arbitrary", "arbitrary")),
        name="moe_route",
    )(x, pre_g, mod, mod, rw, rb, cnt0)


def _row_copy(src, src_row, dst, dst_row, sem):
    return pltpu.make_async_copy(src.at[pl.ds(src_row, 1)], dst.at[pl.ds(dst_row, 1)], sem)


def _dispatch_kernel(d1_ref, d2_ref, hw_ref, xs_in_ref, xs_ref, sem):
    del xs_in_ref
    tm = hw_ref.shape[0]
    base = pl.program_id(0) * tm

    def issue(r, carry):
        _row_copy(hw_ref, r, xs_ref, d1_ref[base + r], sem).start(priority=0)
        _row_copy(hw_ref, r, xs_ref, d2_ref[base + r], sem).start(priority=1)
        return carry

    lax.fori_loop(0, tm, issue, 0, unroll=ISSUE_UNROLL)
    for _ in range(2):
        pltpu.make_async_copy(hw_ref, xs_ref.at[pl.ds(0, tm)], sem).wait()


def _dispatch_call(dest1, dest2, hw, xs, tm):
    rows = hw.shape[0]
    return pl.pallas_call(
        _dispatch_kernel,
        grid_spec=pltpu.PrefetchScalarGridSpec(
            num_scalar_prefetch=2,
            grid=(rows // tm,),
            in_specs=[pl.BlockSpec((tm, HALF), lambda i, d1, d2: (i, 0)),
                      pl.BlockSpec(memory_space=pl.ANY)],
            out_specs=pl.BlockSpec(memory_space=pl.ANY),
            scratch_shapes=[pltpu.SemaphoreType.DMA(())],
        ),
        out_shape=jax.ShapeDtypeStruct(xs.shape, xs.dtype),
        input_output_aliases={3: 0},
        compiler_params=_params(("arbitrary",), row_dma=True),
        name="moe_dispatch",
    )(dest1, dest2, hw, xs)


def _experts_kernel(exp_ref, nv_ref, xs_ref, wg_ref, wu_ref, wd_ref, y_ref):
    del exp_ref
    used = pl.program_id(0) < nv_ref[0]

    @pl.when(used)
    def _():
        words = xs_ref[...]
        lo = lax.bitcast_convert_type(words << jnp.uint32(16), F32)
        hi = lax.bitcast_convert_type(words & jnp.uint32(HI_MASK), F32)
        hb = jnp.concatenate([lo, hi], axis=1).astype(BF16)
        _swiglu_into(hb, wg_ref, wu_ref, wd_ref, y_ref)

    @pl.when(jnp.logical_not(used))
    def _():
        y_ref[...] = jnp.zeros_like(y_ref)


def _experts_call(j, tile_exp, n_valid, xs, wg, wu, wd):
    n_tiles = tile_exp.shape[0]
    tm = EXPERT_TILE
    weight = lambda shape: pl.BlockSpec((None, 1) + shape, lambda i, ex, nv: (j, ex[i], 0, 0))
    return pl.pallas_call(
        _experts_kernel,
        grid_spec=pltpu.PrefetchScalarGridSpec(
            num_scalar_prefetch=2,
            grid=(n_tiles,),
            in_specs=[pl.BlockSpec((tm, HALF), lambda i, ex, nv: (i, 0)),
                      weight((D_MODEL, FFN_DIM)), weight((D_MODEL, FFN_DIM)), weight((FFN_DIM, D_MODEL))],
            out_specs=pl.BlockSpec((tm, D_MODEL), lambda i, ex, nv: (i, 0)),
        ),
        out_shape=jax.ShapeDtypeStruct((xs.shape[0], D_MODEL), F32),
        compiler_params=_params(("arbitrary",)),
        name="moe_experts",
    )(tile_exp, n_valid, xs, wg, wu, wd)


def _combine_kernel(d1_ref, d2_ref, y_ref, info_ref, x_ref, g2_ref, post_ref, o_ref, ybuf, sem):
    tm = x_ref.shape[1]
    base = (pl.program_id(0) * pl.num_programs(1) + pl.program_id(1)) * tm

    def issue(r, carry):
        _row_copy(y_ref, d1_ref[base + r], ybuf.at[0], r, sem).start(priority=0)
        _row_copy(y_ref, d2_ref[base + r], ybuf.at[1], r, sem).start(priority=1)
        return carry

    lax.fori_loop(0, tm, issue, 0, unroll=ISSUE_UNROLL)
    for k in range(2):
        pltpu.make_async_copy(y_ref.at[pl.ds(0, tm)], ybuf.at[k], sem).wait()

    info = info_ref[0]
    f = info[:, INFO_P1:INFO_P1 + 1] * ybuf[0] + info[:, INFO_P2:INFO_P2 + 1] * ybuf[1]
    o_ref[0] = x_ref[0] + g2_ref[0] * (_rms(f) * post_ref[...])


def _combine_call(dest1, dest2, y, info, x, mod, post_g, tm):
    nb, seq, _ = x.shape
    per_row = mod.shape[1] != 1
    return pl.pallas_call(
        _combine_kernel,
        grid_spec=pltpu.PrefetchScalarGridSpec(
            num_scalar_prefetch=2,
            grid=(nb, seq // tm),
            in_specs=[pl.BlockSpec(memory_space=pl.ANY),
                      pl.BlockSpec((1, tm, LANES), lambda b, i, d1, d2: (b, i, 0)),
                      pl.BlockSpec((1, tm, D_MODEL), lambda b, i, d1, d2: (b, i, 0)),
                      pl.BlockSpec((1, tm if per_row else 1, D_MODEL),
                                   lambda b, i, d1, d2: (b, i if per_row else 0, MOD_G2)),
                      pl.BlockSpec(post_g.shape, lambda b, i, d1, d2: (0, 0))],
            out_specs=pl.BlockSpec((1, tm, D_MODEL), lambda b, i, d1, d2: (b, i, 0)),
            scratch_shapes=[pltpu.VMEM((2, tm, D_MODEL), F32), pltpu.SemaphoreType.DMA(())],
        ),
        out_shape=jax.ShapeDtypeStruct(x.shape, F32),
        compiler_params=_params(("arbitrary", "arbitrary"), row_dma=True),
        name="moe_combine",
    )(dest1, dest2, y, info, x, mod, post_g)


def _moe_layer(j, xp, xs, mod_p, mod_s, pre_g, post_g, router_w, router_b, wg, wu, wd):
    rw = jnp.pad(router_w, ((0, 0), (0, LANES - N_EXPERTS)))
    rb = _pad_lanes(router_b)
    hw_p, info_p, cnt_p = _route_call(xp, mod_p, pre_g, rw, rb, jnp.zeros((1, LANES), F32), tm=512)
    hw_s, info_s, cnt = _route_call(xs, mod_s, pre_g, rw, rb, cnt_p, tm=xs.shape[1])

    tm = EXPERT_TILE
    n_assign = 2 * (xp.shape[0] * xp.shape[1] + xs.shape[1])
    n_tiles = n_assign // tm + N_EXPERTS
    counts = cnt[0, :N_EXPERTS].astype(jnp.int32)
    group_tiles = (counts + tm - 1) // tm
    tile_end = jnp.cumsum(group_tiles)
    start = (tile_end - group_tiles) * tm
    n_valid = tile_end[-1]
    tile_id = jnp.minimum(jnp.arange(n_tiles, dtype=jnp.int32), n_valid - 1)
    tile_exp = jnp.sum(tile_id[:, None] >= tile_end[None, :], axis=1).astype(jnp.int32)

    def dests(info):
        flat = info.reshape(-1, LANES)
        d = [start[flat[:, i].astype(jnp.int32)] + flat[:, r].astype(jnp.int32)
             for i, r in ((INFO_I1, INFO_R1), (INFO_I2, INFO_R2))]
        return d[0], d[1]

    dp = dests(info_p)
    ds = dests(info_s)
    slots = jnp.zeros((n_tiles * tm, HALF), jnp.uint32)
    slots = _dispatch_call(dp[0], dp[1], hw_p.reshape(-1, HALF), slots, tm=512)
    slots = _dispatch_call(ds[0], ds[1], hw_s.reshape(-1, HALF), slots, tm=xs.shape[1])
    y = _experts_call(j, tile_exp, n_valid.reshape(1), slots, wg, wu, wd)
    xp = _combine_call(dp[0], dp[1], y, info_p, xp, mod_p, post_g, tm=256)
    xs = _combine_call(ds[0], ds[1], y, info_s, xs, mod_s, post_g, tm=xs.shape[1])
    return xp, xs


def _pad_lanes(v):
    return jnp.pad(v, (0, LANES - v.shape[0])).reshape(1, LANES)


def kernel(x_prompt, x_sample, state_ssm, state_conv, c_prompt, c_sample, w_mod, b_mod, mix_pre_g, mix_post_g, ffn_pre_g, ffn_post_g, w_in, conv_w, conv_b, dt_bias, a_log, d_skip, ssm_norm_g, w_ssd_out, cmlp_ln_g, cmlp_ln_b, w_spatial, b_spatial, w_cmlp_out, w_o, ffn_wg, ffn_wu, ffn_wd, router_w, router_b, exp_wg, exp_wu, exp_wd):
    n_prompt = x_prompt.shape[0]
    n_sample = x_sample.shape[0]

    c_all = jnp.concatenate([c_prompt, c_sample, jnp.zeros((8, D_MODEL), F32)], axis=0)
    mod_all = _mod_call(c_all, w_mod, b_mod)
    mod_p = mod_all[:, :n_prompt].reshape(DEPTH, n_prompt, 1, 6 * D_MODEL)
    mod_s = mod_all[:, n_prompt:n_prompt + n_sample].reshape(DEPTH, 1, n_sample, 6 * D_MODEL)

    xp = x_prompt
    xs = x_sample.reshape(1, n_sample, D_MODEL)
    conv_hist = state_conv.reshape(DEPTH, n_sample, (CONV_WIDTH - 1) * CONV_DIM)

    o_xbc = SSM_INNER
    o_dt = o_xbc + CONV_DIM
    o_u = o_dt + SSM_HEADS

    w_main = jnp.concatenate([w_in[:, :, :o_xbc], w_in[:, :, o_u:], w_in[:, :, o_xbc:o_dt]], axis=2).astype(BF16)
    w_dt = jnp.pad(w_in[:, :, o_dt:o_u], ((0, 0), (0, 0), (0, LANES - SSM_HEADS))).astype(BF16)
    wa, wb, wo = (a.astype(BF16) for a in (w_ssd_out, w_cmlp_out, w_o))
    dense_w = tuple(a.astype(BF16) for a in (ffn_wg, ffn_wu, ffn_wd))
    expert_w = tuple(a.astype(BF16) for a in (exp_wg, exp_wu, exp_wd))

    ssm_s = jnp.zeros(state_ssm.shape, F32)
    ssm_p, conv_p, conv_s, v_s = [], [], [], []
    for l in range(DEPTH):
        row = lambda a: a[l].reshape(1, -1)
        mixer_small = (conv_w[l], row(conv_b), _pad_lanes(dt_bias[l]), _pad_lanes(a_log[l]),
                       jnp.repeat(d_skip[l], SSM_HEAD_DIM).reshape(1, -1), row(ssm_norm_g),
                       row(cmlp_ln_g), row(cmlp_ln_b))

        proj_p, dt_p = _inproj_call(l, xp, mod_p[l], row(mix_pre_g), w_main, w_dt, tm=1024)
        proj_s, dt_s = _inproj_call(l, xs, mod_s[l], row(mix_pre_g), w_main, w_dt, tm=n_sample)
        ya_p, yb_p, hs_p, cs_p = _mixer_call(proj_p, dt_p, *mixer_small, w_spatial[l], b_spatial[l].T)
        ya_s, yb_s, vr_s, ssm_s, cs_s = _mixer_step_call(
            l, proj_s[0], dt_s[0], state_ssm, conv_hist[l], ssm_s, *mixer_small,
            jnp.repeat(w_spatial[l, :, 0, 0], CMLP_GROUP_DIM).reshape(1, -1),
            jnp.repeat(b_spatial[l, :, 0], CMLP_GROUP_DIM).reshape(1, -1))
        xp = _outproj_call(l, ya_p, yb_p, proj_p, xp, mod_p[l], row(mix_post_g), wa, wb, wo, tm=512)
        xs = _outproj_call(l, ya_s[None], yb_s[None], proj_s, xs, mod_s[l], row(mix_post_g), wa, wb, wo,
                           tm=n_sample)

        j = l // 2
        if l % 2 == 0:
            xp = _dense_ffn_call(j, xp, mod_p[l], row(ffn_pre_g), row(ffn_post_g), *dense_w, tm=512)
            xs = _dense_ffn_call(j, xs, mod_s[l], row(ffn_pre_g), row(ffn_post_g), *dense_w, tm=n_sample)
        else:
            xp, xs = _moe_layer(j, xp, xs, mod_p[l], mod_s[l], row(ffn_pre_g), row(ffn_post_g),
                                router_w[j], router_b[j], *expert_w)

        ssm_p.append(hs_p)
        conv_p.append(cs_p)
        conv_s.append(cs_s.reshape(n_sample, CONV_WIDTH - 1, CONV_DIM))
        v_s.append(vr_s.reshape(n_sample, 1, D_MODEL))

    return (xp, xs.reshape(n_sample, 1, D_MODEL), jnp.stack(ssm_p), jnp.stack(conv_p),
            ssm_s, jnp.stack(conv_s), jnp.stack(v_s))
```

```python
import functools

import jax
import jax.numpy as jnp
from jax import lax
from jax.experimental import pallas as pl
from jax.experimental.pallas import tpu as pltpu

F32 = jnp.float32
BF16 = jnp.bfloat16

D_MODEL = 1024
DEPTH = 4
SSM_HEADS = 16
SSM_HEAD_DIM = 64
SSM_GROUPS = 2
SSM_STATE = 128
SSM_INNER = 1024
GROUP_WIDTH = SSM_INNER // SSM_GROUPS
HEADS_PER_GROUP = SSM_HEADS // SSM_GROUPS
CONV_WIDTH = 4
CONV_DIM = 1536
CHUNK = 128
CMLP_GROUPS = 8
CMLP_GROUP_DIM = 128
FFN_DIM = 2816
N_EXPERTS = 8
EPS = 1e-6
LANES = 128
COL_Z, COL_U, COL_V, COL_GA, COL_GB, COL_XS = 0, 1, 2, 3, 4, 5
COL_BC_512 = 12
PROJ_MAIN = 6 * D_MODEL + 2 * SSM_GROUPS * SSM_STATE
PROJ_TILE = 1664
MOD_SH1, MOD_SC1, MOD_G1, MOD_SH2, MOD_SC2, MOD_G2 = range(6)
VMEM_LIMIT = 56 * 1024 * 1024


def _params(semantics, row_dma=False):
    return pltpu.CompilerParams(dimension_semantics=semantics, vmem_limit_bytes=VMEM_LIMIT,
                                disable_bounds_checks=row_dma)


def _rms(x):
    return x * lax.rsqrt(jnp.mean(x * x, axis=-1, keepdims=True) + EPS)


def _dot(a, b):
    return jnp.dot(a, b, preferred_element_type=F32)


def _dot_nt(a, b):
    return lax.dot_general(a, b, (((1,), (1,)), ((), ())), preferred_element_type=F32)


def _dot_tn(a, b):
    return lax.dot_general(a, b, (((0,), (0,)), ((), ())), preferred_element_type=F32)


def _mod_kernel(c_ref, w_ref, b_ref, o_ref):
    a = jax.nn.silu(c_ref[...]).astype(BF16)
    o_ref[0] = _dot(a, w_ref[0].astype(BF16)) + b_ref[0]


def _mod_call(c_all, w_mod, b_mod):
    rows = c_all.shape[0]
    return pl.pallas_call(
        _mod_kernel,
        grid=(DEPTH, 6),
        in_specs=[
            pl.BlockSpec((rows, D_MODEL), lambda l, j: (0, 0)),
            pl.BlockSpec((1, D_MODEL, D_MODEL), lambda l, j: (l, 0, j)),
            pl.BlockSpec((1, 1, D_MODEL), lambda l, j: (l, 0, j)),
        ],
        out_specs=pl.BlockSpec((1, rows, D_MODEL), lambda l, j: (l, 0, j)),
        out_shape=jax.ShapeDtypeStruct((DEPTH, rows, 6 * D_MODEL), F32),
        compiler_params=_params(("arbitrary", "arbitrary")),
        name="adaln_mod",
    )(c_all, w_mod, b_mod.reshape(DEPTH, 1, 6 * D_MODEL))


def _mod_spec(mod, tm, seg, grid_rank):
    per_row = mod.shape[1] != 1
    rows = tm if per_row else 1
    if grid_rank == 2:
        return pl.BlockSpec((1, rows, D_MODEL), lambda b, i: (b, i if per_row else 0, seg))
    return pl.BlockSpec((1, rows, D_MODEL), lambda b, i, j: (b, i if per_row else 0, seg))


def _inproj_kernel(x_ref, g_ref, sh_ref, sc_ref, w_ref, wdt_ref, p_ref, pdt_ref, h_scr):
    @pl.when(pl.program_id(2) == 0)
    def _():
        h = _rms(x_ref[0]) * g_ref[...] * (1.0 + sc_ref[0]) + sh_ref[0]
        hb = h.astype(BF16)
        h_scr[...] = hb
        pdt_ref[0] = _dot(hb, wdt_ref[...])

    p_ref[0] = _dot(h_scr[...], w_ref[...])


def _inproj_call(layer, x, mod, g, w_main, w_dt, tm):
    nb, seq, _ = x.shape
    grid = (nb, seq // tm, PROJ_MAIN // PROJ_TILE)
    return pl.pallas_call(
        _inproj_kernel,
        grid=grid,
        in_specs=[
            pl.BlockSpec((1, tm, D_MODEL), lambda b, i, j: (b, i, 0)),
            pl.BlockSpec((1, D_MODEL), lambda b, i, j: (0, 0)),
            _mod_spec(mod, tm, MOD_SH1, 3),
            _mod_spec(mod, tm, MOD_SC1, 3),
            pl.BlockSpec((None, D_MODEL, PROJ_TILE), lambda b, i, j: (layer, 0, j)),
            pl.BlockSpec((None, D_MODEL, LANES), lambda b, i, j: (layer, 0, 0)),
        ],
        out_specs=[
            pl.BlockSpec((1, tm, PROJ_TILE), lambda b, i, j: (b, i, j)),
            pl.BlockSpec((1, tm, LANES), lambda b, i, j: (b, i, 0)),
        ],
        out_shape=[
            jax.ShapeDtypeStruct((nb, seq, PROJ_MAIN), F32),
            jax.ShapeDtypeStruct((nb, seq, LANES), F32),
        ],
        scratch_shapes=[pltpu.VMEM((tm, D_MODEL), BF16)],
        compiler_params=_params(("arbitrary", "arbitrary", "arbitrary")),
        name="in_proj",
    )(x, g, mod, mod, w_main, w_dt)


def _gated_group_norm(y, z, norm_g):
    y = y * jax.nn.silu(z)
    parts = [_rms(y[:, g * GROUP_WIDTH:(g + 1) * GROUP_WIDTH]) for g in range(SSM_GROUPS)]
    return jnp.concatenate(parts, axis=-1) * norm_g


def _layernorm(x, g, b):
    mu = jnp.mean(x, axis=-1, keepdims=True)
    xc = x - mu
    return xc * lax.rsqrt(jnp.mean(xc * xc, axis=-1, keepdims=True) + EPS) * g + b


def _pair_columns(v, pair, lane_lo):
    h0 = 2 * pair
    return jnp.where(lane_lo, v[:, h0:h0 + 1], v[:, h0 + 1:h0 + 2])


MIX_CHUNKS = 4


def _mixer_chunk(rows, z_ref, u_ref, v_ref, xs_ref, bc_ref, dt_ref,
                 cw_ref, cb_ref, dtb_ref, alog_ref, dskip_ref, ng_ref, lng_ref, lnb_ref, bsp_ref,
                 h_scr, cbuf, wsp_scr, causal, lane_lo):
    cbuf[8:8 + CHUNK, 0:SSM_INNER] = xs_ref[0, rows, :]
    cbuf[8:8 + CHUNK, SSM_INNER:CONV_DIM] = bc_ref[0, rows, :]
    acc = cbuf[5:5 + CHUNK, :] * cw_ref[0:1, :]
    for k in range(1, CONV_WIDTH):
        acc = acc + cbuf[5 + k:5 + k + CHUNK, :] * cw_ref[k:k + 1, :]
    cbuf[5:8, :] = cbuf[5 + CHUNK:8 + CHUNK, :]

    xbc = jax.nn.silu(acc + cb_ref[...])
    xc = xbc[:, 0:SSM_INNER]
    bm = xbc[:, SSM_INNER:SSM_INNER + SSM_GROUPS * SSM_STATE].astype(BF16)
    cm = xbc[:, SSM_INNER + SSM_GROUPS * SSM_STATE:CONV_DIM].astype(BF16)

    dt = jax.nn.softplus(dt_ref[0, rows, :] + dtb_ref[...])
    da = dt * (-jnp.exp(alog_ref[...]))
    cum = jnp.dot(causal.astype(F32), da, preferred_element_type=F32,
                  precision=lax.Precision.HIGHEST)
    last = cum[CHUNK - 1:CHUNK, :]
    ecum = jnp.exp(cum)
    wend = jnp.exp(last - cum) * dt
    elast = jnp.exp(last)
    cum_t = cum.T
    dt_t = dt.T

    y_pairs = []
    for g in range(SSM_GROUPS):
        bg = bm[:, g * SSM_STATE:(g + 1) * SSM_STATE]
        cg = cm[:, g * SSM_STATE:(g + 1) * SSM_STATE]
        cb = _dot_nt(cg, bg)
        hg = h_scr[g * GROUP_WIDTH:(g + 1) * GROUP_WIDTH, :]
        y_state = _dot_nt(cg, hg.astype(BF16))
        xw_parts = []
        for q in range(HEADS_PER_GROUP // 2):
            pair = g * (HEADS_PER_GROUP // 2) + q
            mixes = []
            for h in (2 * pair, 2 * pair + 1):
                seg = cum[:, h:h + 1] - cum_t[h:h + 1, :]
                decay = jnp.where(causal, jnp.exp(seg), 0.0)
                mixes.append((cb * decay * dt_t[h:h + 1, :]).astype(BF16))
            xp = xc[:, pair * LANES:(pair + 1) * LANES]
            rhs = jnp.concatenate([jnp.where(lane_lo, xp, 0.0), jnp.where(lane_lo, 0.0, xp)],
                                  axis=0).astype(BF16)
            y_in = _dot(jnp.concatenate(mixes, axis=1), rhs)
            y_st = y_state[:, q * LANES:(q + 1) * LANES] * _pair_columns(ecum, pair, lane_lo)
            y_pairs.append(y_in + y_st)
            xw_parts.append((xp * _pair_columns(wend, pair, lane_lo)).astype(BF16))
        upd = _dot_tn(jnp.concatenate(xw_parts, axis=1), bg)
        for r in range(HEADS_PER_GROUP):
            h = g * HEADS_PER_GROUP + r
            head = slice(h * SSM_HEAD_DIM, (h + 1) * SSM_HEAD_DIM)
            scale = jnp.broadcast_to(elast[0:1, h:h + 1], (SSM_HEAD_DIM, SSM_STATE))
            h_scr[head, :] = h_scr[head, :] * scale + upd[r * SSM_HEAD_DIM:(r + 1) * SSM_HEAD_DIM, :]

    y = jnp.concatenate(y_pairs, axis=1) + dskip_ref[...] * xc
    ya = _gated_group_norm(y, z_ref[0, rows, :], ng_ref[...]).astype(BF16)

    ug = jax.nn.gelu(u_ref[0, rows, :], approximate=True)
    vn = _layernorm(jax.nn.gelu(v_ref[0, rows, :], approximate=True), lng_ref[...], lnb_ref[...])
    gates = []
    for g in range(CMLP_GROUPS):
        vg = vn[:, g * CMLP_GROUP_DIM:(g + 1) * CMLP_GROUP_DIM].astype(BF16)
        gates.append(_dot(wsp_scr[g], vg) + bsp_ref[:, g:g + 1])
    yb = (ug * jnp.concatenate(gates, axis=1)).astype(BF16)
    return ya, yb


def _merge_project(ya, yb, ga, gb, x, g1, post_g, wa_ref, wb_ref, wo_ref):
    merged = (jax.nn.sigmoid(ga) * _dot(ya, wa_ref[...]) + jax.nn.sigmoid(gb) * _dot(yb, wb_ref[...]))
    o = _dot(merged.astype(BF16), wo_ref[...])
    return x + g1 * (_rms(o) * post_g)


def _mixer_kernel(z_ref, u_ref, v_ref, xs_ref, bc_ref, dt_ref, ga_ref, gb_ref, x_ref, g1_ref,
                  cw_ref, cb_ref, dtb_ref, alog_ref, dskip_ref, ng_ref, lng_ref, lnb_ref, wsp_ref, bsp_ref,
                  pg_ref, wa_ref, wb_ref, wo_ref,
                  o_ref, ssm_ref, conv_ref,
                  h_scr, cbuf, wsp_scr):
    c = pl.program_id(1)
    row = lax.broadcasted_iota(jnp.int32, (CHUNK, CHUNK), 0)
    col = lax.broadcasted_iota(jnp.int32, (CHUNK, CHUNK), 1)
    causal = row >= col
    lane_lo = col < SSM_HEAD_DIM

    @pl.when(c == 0)
    def _():
        h_scr[...] = jnp.zeros_like(h_scr)
        cbuf[0:8, :] = jnp.zeros((8, CONV_DIM), F32)
        for g in range(CMLP_GROUPS):
            wsp_scr[g] = jnp.where(causal, wsp_ref[g], 0.0).astype(BF16)

    for k in range(MIX_CHUNKS):
        rows = slice(k * CHUNK, (k + 1) * CHUNK)
        ya, yb = _mixer_chunk(rows, z_ref, u_ref, v_ref, xs_ref, bc_ref, dt_ref,
                              cw_ref, cb_ref, dtb_ref, alog_ref, dskip_ref, ng_ref, lng_ref, lnb_ref, bsp_ref,
                              h_scr, cbuf, wsp_scr, causal, lane_lo)
        o_ref[0, rows, :] = _merge_project(ya, yb, ga_ref[0, rows, :], gb_ref[0, rows, :], x_ref[0, rows, :],
                                           g1_ref[0], pg_ref[...], wa_ref, wb_ref, wo_ref)

    @pl.when(c == pl.num_programs(1) - 1)
    def _():
        conv_ref[0] = cbuf[5:8, :]
        ssm_ref[0] = h_scr[...].reshape(SSM_HEADS, SSM_HEAD_DIM, SSM_STATE)


def _mixer_call(layer, proj, proj_dt, x, mod, post_g, wa, wb, wo,
                conv_w, conv_b, dt_bias, a_log, d_skip, norm_g, ln_g, ln_b, w_sp, b_sp_t):
    nb, seq, _ = proj.shape
    tm = MIX_CHUNKS * CHUNK

    def col(block, width=D_MODEL):
        return pl.BlockSpec((1, tm, width), lambda b, c: (b, c, block))

    def whole(a):
        zeros = (0,) * a.ndim
        return pl.BlockSpec(a.shape, lambda b, c: zeros)

    def of_layer(a):
        return pl.BlockSpec((None,) + a.shape[1:], lambda b, c: (layer, 0, 0))

    small = (conv_w, conv_b, dt_bias, a_log, d_skip, norm_g, ln_g, ln_b, w_sp, b_sp_t)
    return pl.pallas_call(
        _mixer_kernel,
        grid=(nb, seq // tm),
        in_specs=[col(COL_Z), col(COL_U), col(COL_V), col(COL_XS),
                  col(COL_BC_512, 2 * SSM_GROUPS * SSM_STATE),
                  pl.BlockSpec((1, tm, LANES), lambda b, c: (b, c, 0)),
                  col(COL_GA), col(COL_GB), col(0), _mod_spec(mod, tm, MOD_G1, 2)]
                 + [whole(a) for a in small]
                 + [whole(post_g), of_layer(wa), of_layer(wb), of_layer(wo)],
        out_specs=[
            pl.BlockSpec((1, tm, D_MODEL), lambda b, c: (b, c, 0)),
            pl.BlockSpec((1, SSM_HEADS, SSM_HEAD_DIM, SSM_STATE), lambda b, c: (b, 0, 0, 0)),
            pl.BlockSpec((1, CONV_WIDTH - 1, CONV_DIM), lambda b, c: (b, 0, 0)),
        ],
        out_shape=[
            jax.ShapeDtypeStruct(x.shape, F32),
            jax.ShapeDtypeStruct((nb, SSM_HEADS, SSM_HEAD_DIM, SSM_STATE), F32),
            jax.ShapeDtypeStruct((nb, CONV_WIDTH - 1, CONV_DIM), F32),
        ],
        scratch_shapes=[
            pltpu.VMEM((SSM_INNER, SSM_STATE), F32),
            pltpu.VMEM((CHUNK + 8, CONV_DIM), F32),
            pltpu.VMEM((CMLP_GROUPS, CHUNK, CHUNK), BF16),
        ],
        compiler_params=_params(("arbitrary", "arbitrary")),
        name="mixer_prompt",
    )(proj, proj, proj, proj, proj, proj_dt, proj, proj, x, mod, *small, post_g, wa, wb, wo)


SAMPLE_TILE = 16


def _mixer_step_kernel(z_ref, u_ref, v_ref, xs_ref, bc_ref, dt_ref, ssm_in_ref, conv_in_ref,
                       cw_ref, cb_ref, dtb_ref, alog_ref, dskip_ref, ng_ref,
                       lng_ref, lnb_ref, wsp0_ref, bsp0_ref, ssm_all_ref,
                       ya_ref, yb_ref, vout_ref, ssm_ref, conv_ref):
    del ssm_all_ref
    tb = SAMPLE_TILE
    xbc_new = jnp.concatenate([xs_ref[...], bc_ref[...]], axis=1)
    hist = conv_in_ref[...]
    acc = xbc_new * cw_ref[CONV_WIDTH - 1:CONV_WIDTH, :]
    for k in range(CONV_WIDTH - 1):
        acc = acc + hist[:, k * CONV_DIM:(k + 1) * CONV_DIM] * cw_ref[k:k + 1, :]
    conv_ref[:, 0:(CONV_WIDTH - 2) * CONV_DIM] = hist[:, CONV_DIM:]
    conv_ref[:, (CONV_WIDTH - 2) * CONV_DIM:] = xbc_new

    xbc = jax.nn.silu(acc + cb_ref[...])
    xc = xbc[:, 0:SSM_INNER]
    bm = xbc[:, SSM_INNER:SSM_INNER + SSM_GROUPS * SSM_STATE]
    cm = xbc[:, SSM_INNER + SSM_GROUPS * SSM_STATE:CONV_DIM]
    dt = jax.nn.softplus(dt_ref[...] + dtb_ref[...])
    dec = jnp.exp(dt * (-jnp.exp(alog_ref[...])))

    def transposed(a):
        pad = jnp.zeros((LANES - tb, a.shape[1]), F32)
        return jnp.concatenate([a, pad], axis=0).T

    xc_t = transposed(xc)
    dt_t = transposed(dt)
    dec_t = transposed(dec)
    row_id = lax.broadcasted_iota(jnp.int32, (tb, SSM_STATE), 0)

    y_groups = [jnp.zeros((tb, GROUP_WIDTH), F32) for _ in range(SSM_GROUPS)]
    for b in range(tb):
        for g in range(SSM_GROUPS):
            b_row = bm[b:b + 1, g * SSM_STATE:(g + 1) * SSM_STATE]
            c_only = jnp.where(row_id == b, cm[:, g * SSM_STATE:(g + 1) * SSM_STATE], 0.0).astype(BF16)
            new_heads = []
            for r in range(HEADS_PER_GROUP):
                h = g * HEADS_PER_GROUP + r
                x_col = xc_t[h * SSM_HEAD_DIM:(h + 1) * SSM_HEAD_DIM, b:b + 1]
                push = x_col * dt_t[h:h + 1, b:b + 1]
                keep = jnp.broadcast_to(dec_t[h:h + 1, b:b + 1], (SSM_HEAD_DIM, SSM_STATE))
                h_new = ssm_in_ref[b, h] * keep + push * b_row
                ssm_ref[b, h] = h_new
                new_heads.append(h_new.astype(BF16))
            hg = jnp.concatenate(new_heads, axis=0)
            y_groups[g] = y_groups[g] + _dot_nt(c_only, hg)

    y = jnp.concatenate(y_groups, axis=1) + dskip_ref[...] * xc
    ya_ref[...] = _gated_group_norm(y, z_ref[...], ng_ref[...]).astype(BF16)

    ug = jax.nn.gelu(u_ref[...], approximate=True)
    vn = _layernorm(jax.nn.gelu(v_ref[...], approximate=True), lng_ref[...], lnb_ref[...])
    vout_ref[...] = vn
    yb_ref[...] = (ug * (vn * wsp0_ref[...] + bsp0_ref[...])).astype(BF16)


def _mixer_step_call(layer, proj, proj_dt, state_ssm, conv_hist, ssm_all, conv_w, conv_b, dt_bias, a_log,
                     d_skip, norm_g, ln_g, ln_b, w_sp0, b_sp0):
    nseq = proj.shape[0]
    tb = SAMPLE_TILE

    def col(block, width=D_MODEL):
        return pl.BlockSpec((tb, width), lambda i: (i, block))

    def whole(a):
        zeros = (0,) * a.ndim
        return pl.BlockSpec(a.shape, lambda i: zeros)

    small = (conv_w, conv_b, dt_bias, a_log, d_skip, norm_g, ln_g, ln_b, w_sp0, b_sp0)
    hist_width = (CONV_WIDTH - 1) * CONV_DIM
    return pl.pallas_call(
        _mixer_step_kernel,
        grid=(nseq // tb,),
        in_specs=[col(COL_Z), col(COL_U), col(COL_V), col(COL_XS),
                  col(COL_BC_512, 2 * SSM_GROUPS * SSM_STATE),
                  pl.BlockSpec((tb, LANES), lambda i: (i, 0)),
                  pl.BlockSpec((None, tb, SSM_HEADS, SSM_HEAD_DIM, SSM_STATE),
                               lambda i: (layer, i, 0, 0, 0)),
                  pl.BlockSpec((tb, hist_width), lambda i: (i, 0))]
                 + [whole(a) for a in small] + [pl.BlockSpec(memory_space=pl.ANY)],
        out_specs=[
            pl.BlockSpec((tb, SSM_INNER), lambda i: (i, 0)),
            pl.BlockSpec((tb, D_MODEL), lambda i: (i, 0)),
            pl.BlockSpec((tb, D_MODEL), lambda i: (i, 0)),
            pl.BlockSpec((None, tb, SSM_HEADS, SSM_HEAD_DIM, SSM_STATE), lambda i: (layer, i, 0, 0, 0)),
            pl.BlockSpec((tb, hist_width), lambda i: (i, 0)),
        ],
        out_shape=[
            jax.ShapeDtypeStruct((nseq, SSM_INNER), BF16),
            jax.ShapeDtypeStruct((nseq, D_MODEL), BF16),
            jax.ShapeDtypeStruct((nseq, D_MODEL), F32),
            jax.ShapeDtypeStruct(ssm_all.shape, F32),
            jax.ShapeDtypeStruct((nseq, hist_width), F32),
        ],
        input_output_aliases={8 + len(small): 3},
        compiler_params=_params(("arbitrary",)),
        name="mixer_sample",
    )(proj, proj, proj, proj, proj, proj_dt, state_ssm, conv_hist, *small, ssm_all)


def _outproj_kernel(ya_ref, yb_ref, ga_ref, gb_ref, x_ref, g1_ref, pg_ref, wa_ref, wb_ref, wo_ref, o_ref):
    o_ref[0] = _merge_project(ya_ref[0], yb_ref[0], ga_ref[0], gb_ref[0], x_ref[0], g1_ref[0], pg_ref[...],
                              wa_ref, wb_ref, wo_ref)


def _outproj_call(layer, ya, yb, proj, x, mod, post_g, wa, wb, wo, tm):
    nb, seq, _ = x.shape

    def rows(block=0):
        return pl.BlockSpec((1, tm, D_MODEL), lambda b, i: (b, i, block))

    def whole(a):
        return pl.BlockSpec(a.shape, lambda b, i: (0, 0))

    def of_layer(a):
        return pl.BlockSpec((None,) + a.shape[1:], lambda b, i: (layer, 0, 0))

    return pl.pallas_call(
        _outproj_kernel,
        grid=(nb, seq // tm),
        in_specs=[rows(), rows(), rows(COL_GA), rows(COL_GB), rows(),
                  _mod_spec(mod, tm, MOD_G1, 2), whole(post_g), of_layer(wa), of_layer(wb), of_layer(wo)],
        out_specs=rows(),
        out_shape=jax.ShapeDtypeStruct(x.shape, F32),
        compiler_params=_params(("arbitrary", "arbitrary")),
        name="out_proj",
    )(ya, yb, proj, proj, x, mod, post_g, wa, wb, wo)


FFN_CHUNK = 256


def _swiglu_into(hb, wg_ref, wu_ref, wd_ref, acc_ref):
    for f in range(FFN_DIM // FFN_CHUNK):
        cols = slice(f * FFN_CHUNK, (f + 1) * FFN_CHUNK)
        act = jax.nn.silu(_dot(hb, wg_ref[0, :, cols])) * _dot(hb, wu_ref[0, :, cols])
        part = _dot(act.astype(BF16), wd_ref[0, cols, :])
        if f == 0:
            acc_ref[...] = part
        else:
            acc_ref[...] += part


def _premod(x_ref, pre_ref, sh_ref, sc_ref):
    return _rms(x_ref[0]) * pre_ref[...] * (1.0 + sc_ref[0]) + sh_ref[0]


def _dense_ffn_kernel(x_ref, pre_ref, post_ref, sh_ref, sc_ref, g2_ref, wg_ref, wu_ref, wd_ref,
                      o_ref, acc_scr):
    hb = _premod(x_ref, pre_ref, sh_ref, sc_ref).astype(BF16)
    _swiglu_into(hb, wg_ref, wu_ref, wd_ref, acc_scr)
    o_ref[0] = x_ref[0] + g2_ref[0] * (_rms(acc_scr[...]) * post_ref[...])


def _dense_ffn_call(j, x, mod, pre_g, post_g, wg, wu, wd, tm):
    nb, seq, _ = x.shape

    def whole(a):
        zeros = (0,) * a.ndim
        return pl.BlockSpec(a.shape, lambda b, i: zeros)

    def of_layer(a):
        return pl.BlockSpec((1,) + a.shape[1:], lambda b, i: (j, 0, 0))

    rows = pl.BlockSpec((1, tm, D_MODEL), lambda b, i: (b, i, 0))
    return pl.pallas_call(
        _dense_ffn_kernel,
        grid=(nb, seq // tm),
        in_specs=[rows, whole(pre_g), whole(post_g),
                  _mod_spec(mod, tm, MOD_SH2, 2), _mod_spec(mod, tm, MOD_SC2, 2),
                  _mod_spec(mod, tm, MOD_G2, 2), of_layer(wg), of_layer(wu), of_layer(wd)],
        out_specs=rows,
        out_shape=jax.ShapeDtypeStruct(x.shape, F32),
        scratch_shapes=[pltpu.VMEM((tm, D_MODEL), F32)],
        compiler_params=_params(("arbitrary", "arbitrary")),
        name="dense_ffn",
    )(x, pre_g, post_g, mod, mod, mod, wg, wu, wd)


EXPERT_TILE = 512
HALF = D_MODEL // 2
INFO_I1, INFO_I2, INFO_P1, INFO_P2, INFO_R1, INFO_R2 = range(6)
HI_MASK = 0xFFFF0000
ISSUE_UNROLL = 8


def _route_kernel(x_ref, pre_ref, sh_ref, sc_ref, rw_ref, rb_ref, cnt0_ref,
                  hw_ref, info_ref, cnt_ref, carry):
    @pl.when(jnp.logical_and(pl.program_id(0) == 0, pl.program_id(1) == 0))
    def _():
        carry[...] = cnt0_ref[...]

    hb = _premod(x_ref, pre_ref, sh_ref, sc_ref).astype(BF16)
    tm = hb.shape[0]
    bits = lax.bitcast_convert_type(hb.astype(F32), jnp.uint32)
    hw_ref[0] = (bits[:, HALF:] & jnp.uint32(HI_MASK)) | (bits[:, :HALF] >> jnp.uint32(16))

    logits = _dot(hb, rw_ref[...].astype(BF16)) + rb_ref[...]
    lane = lax.broadcasted_iota(jnp.int32, logits.shape, 1).astype(F32)
    neg = jnp.float32(-jnp.inf)
    logits = jnp.where(lane < N_EXPERTS, logits, neg)
    m1 = jnp.max(logits, axis=-1, keepdims=True)
    i1 = jnp.min(jnp.where(logits == m1, lane, float(LANES)), axis=-1, keepdims=True)
    rest = jnp.where(lane == i1, neg, logits)
    m2 = jnp.max(rest, axis=-1, keepdims=True)
    i2 = jnp.min(jnp.where(rest == m2, lane, float(LANES)), axis=-1, keepdims=True)
    e2 = jnp.exp(m2 - m1)
    p1 = 1.0 / (1.0 + e2)
    p2 = e2 * p1

    member = jnp.logical_or(lane == i1, lane == i2)
    row = lax.broadcasted_iota(jnp.int32, (tm, tm), 0)
    col = lax.broadcasted_iota(jnp.int32, (tm, tm), 1)
    before = jnp.where(row > col, 1.0, 0.0).astype(BF16)
    ones = jnp.where(member, 1.0, 0.0)
    prior = _dot(before, ones.astype(BF16)) + carry[...]
    r1 = jnp.sum(jnp.where(lane == i1, prior, 0.0), axis=-1, keepdims=True)
    r2 = jnp.sum(jnp.where(lane == i2, prior, 0.0), axis=-1, keepdims=True)
    carry[...] += jnp.sum(ones, axis=0, keepdims=True)
    cnt_ref[...] = carry[...]

    info = jnp.zeros_like(logits)
    for k, v in ((INFO_I1, i1), (INFO_I2, i2), (INFO_P1, p1), (INFO_P2, p2), (INFO_R1, r1), (INFO_R2, r2)):
        info = jnp.where(lane == float(k), v, info)
    info_ref[0] = info


def _route_call(x, mod, pre_g, rw, rb, cnt0, tm):
    nb, seq, _ = x.shape

    def whole(a):
        return pl.BlockSpec(a.shape, lambda b, i: (0, 0))

    return pl.pallas_call(
        _route_kernel,
        grid=(nb, seq // tm),
        in_specs=[pl.BlockSpec((1, tm, D_MODEL), lambda b, i: (b, i, 0)), whole(pre_g),
                  _mod_spec(mod, tm, MOD_SH2, 2), _mod_spec(mod, tm, MOD_SC2, 2),
                  whole(rw), whole(rb), whole(cnt0)],
        out_specs=[pl.BlockSpec((1, tm, HALF), lambda b, i: (b, i, 0)),
                   pl.BlockSpec((1, tm, LANES), lambda b, i: (b, i, 0)),
                   pl.BlockSpec((1, LANES), lambda b, i: (0, 0))],
        out_shape=[jax.ShapeDtypeStruct((nb, seq, HALF), jnp.uint32),
                   jax.ShapeDtypeStruct((nb, seq, LANES), F32),
                   jax.ShapeDtypeStruct((1, LANES), F32)],
        scratch_shapes=[pltpu.VMEM((1, LANES), F32)],
        compiler_params=_params(("arbitrary", "arbitrary")),
        name="moe_route",
    )(x, pre_g, mod, mod, rw, rb, cnt0)


def _row_copy(src, src_row, dst, dst_row, sem):
    return pltpu.make_async_copy(src.at[pl.ds(src_row, 1)], dst.at[pl.ds(dst_row, 1)], sem)


def _dispatch_kernel(d1_ref, d2_ref, hw_ref, xs_in_ref, xs_ref, sem):
    del xs_in_ref
    tm = hw_ref.shape[0]
    base = pl.program_id(0) * tm

    def issue(r, carry):
        _row_copy(hw_ref, r, xs_ref, d1_ref[base + r], sem).start(priority=0)
        _row_copy(hw_ref, r, xs_ref, d2_ref[base + r], sem).start(priority=1)
        return carry

    lax.fori_loop(0, tm, issue, 0, unroll=ISSUE_UNROLL)
    for _ in range(2):
        pltpu.make_async_copy(hw_ref, xs_ref.at[pl.ds(0, tm)], sem).wait()


def _dispatch_call(dest1, dest2, hw, xs, tm):
    rows = hw.shape[0]
    return pl.pallas_call(
        _dispatch_kernel,
        grid_spec=pltpu.PrefetchScalarGridSpec(
            num_scalar_prefetch=2,
            grid=(rows // tm,),
            in_specs=[pl.BlockSpec((tm, HALF), lambda i, d1, d2: (i, 0)),
                      pl.BlockSpec(memory_space=pl.ANY)],
            out_specs=pl.BlockSpec(memory_space=pl.ANY),
            scratch_shapes=[pltpu.SemaphoreType.DMA(())],
        ),
        out_shape=jax.ShapeDtypeStruct(xs.shape, xs.dtype),
        input_output_aliases={3: 0},
        compiler_params=_params(("arbitrary",), row_dma=True),
        name="moe_dispatch",
    )(dest1, dest2, hw, xs)


def _experts_kernel(exp_ref, nv_ref, xs_ref, wg_ref, wu_ref, wd_ref, y_ref):
    del exp_ref
    used = pl.program_id(0) < nv_ref[0]

    @pl.when(used)
    def _():
        words = xs_ref[...]
        lo = lax.bitcast_convert_type(words << jnp.uint32(16), F32)
        hi = lax.bitcast_convert_type(words & jnp.uint32(HI_MASK), F32)
        hb = jnp.concatenate([lo, hi], axis=1).astype(BF16)
        _swiglu_into(hb, wg_ref, wu_ref, wd_ref, y_ref)

    @pl.when(jnp.logical_not(used))
    def _():
        y_ref[...] = jnp.zeros_like(y_ref)


def _experts_call(j, tile_exp, n_valid, xs, wg, wu, wd):
    n_tiles = tile_exp.shape[0]
    tm = EXPERT_TILE
    weight = lambda shape: pl.BlockSpec((None, 1) + shape, lambda i, ex, nv: (j, ex[i], 0, 0))
    return pl.pallas_call(
        _experts_kernel,
        grid_spec=pltpu.PrefetchScalarGridSpec(
            num_scalar_prefetch=2,
            grid=(n_tiles,),
            in_specs=[pl.BlockSpec((tm, HALF), lambda i, ex, nv: (i, 0)),
                      weight((D_MODEL, FFN_DIM)), weight((D_MODEL, FFN_DIM)), weight((FFN_DIM, D_MODEL))],
            out_specs=pl.BlockSpec((tm, D_MODEL), lambda i, ex, nv: (i, 0)),
        ),
        out_shape=jax.ShapeDtypeStruct((xs.shape[0], D_MODEL), F32),
        compiler_params=_params(("arbitrary",)),
        name="moe_experts",
    )(tile_exp, n_valid, xs, wg, wu, wd)


def _combine_kernel(d1_ref, d2_ref, y_ref, info_ref, x_ref, g2_ref, post_ref, o_ref, ybuf, sem):
    tm = x_ref.shape[1]
    base = (pl.program_id(0) * pl.num_programs(1) + pl.program_id(1)) * tm

    def issue(r, carry):
        _row_copy(y_ref, d1_ref[base + r], ybuf.at[0], r, sem).start(priority=0)
        _row_copy(y_ref, d2_ref[base + r], ybuf.at[1], r, sem).start(priority=1)
        return carry

    lax.fori_loop(0, tm, issue, 0, unroll=ISSUE_UNROLL)
    for k in range(2):
        pltpu.make_async_copy(y_ref.at[pl.ds(0, tm)], ybuf.at[k], sem).wait()

    info = info_ref[0]
    f = info[:, INFO_P1:INFO_P1 + 1] * ybuf[0] + info[:, INFO_P2:INFO_P2 + 1] * ybuf[1]
    o_ref[0] = x_ref[0] + g2_ref[0] * (_rms(f) * post_ref[...])


def _combine_call(dest1, dest2, y, info, x, mod, post_g, tm):
    nb, seq, _ = x.shape
    per_row = mod.shape[1] != 1
    return pl.pallas_call(
        _combine_kernel,
        grid_spec=pltpu.PrefetchScalarGridSpec(
            num_scalar_prefetch=2,
            grid=(nb, seq // tm),
            in_specs=[pl.BlockSpec(memory_space=pl.ANY),
                      pl.BlockSpec((1, tm, LANES), lambda b, i, d1, d2: (b, i, 0)),
                      pl.BlockSpec((1, tm, D_MODEL), lambda b, i, d1, d2: (b, i, 0)),
                      pl.BlockSpec((1, tm if per_row else 1, D_MODEL),
                                   lambda b, i, d1, d2: (b, i if per_row else 0, MOD_G2)),
                      pl.BlockSpec(post_g.shape, lambda b, i, d1, d2: (0, 0))],
            out_specs=pl.BlockSpec((1, tm, D_MODEL), lambda b, i, d1, d2: (b, i, 0)),
            scratch_shapes=[pltpu.VMEM((2, tm, D_MODEL), F32), pltpu.SemaphoreType.DMA(())],
        ),
        out_shape=jax.ShapeDtypeStruct(x.shape, F32),
        compiler_params=_params(("arbitrary", "arbitrary"), row_dma=True),
        name="moe_combine",
    )(dest1, dest2, y, info, x, mod, post_g)


def _moe_layer(j, xp, xs, mod_p, mod_s, pre_g, post_g, router_w, router_b, wg, wu, wd):
    rw = jnp.pad(router_w, ((0, 0), (0, LANES - N_EXPERTS)))
    rb = _pad_lanes(router_b)
    hw_p, info_p, cnt_p = _route_call(xp, mod_p, pre_g, rw, rb, jnp.zeros((1, LANES), F32), tm=512)
    hw_s, info_s, cnt = _route_call(xs, mod_s, pre_g, rw, rb, cnt_p, tm=xs.shape[1])

    tm = EXPERT_TILE
    n_assign = 2 * (xp.shape[0] * xp.shape[1] + xs.shape[1])
    n_tiles = n_assign // tm + N_EXPERTS
    counts = cnt[0, :N_EXPERTS].astype(jnp.int32)
    group_tiles = (counts + tm - 1) // tm
    tile_end = jnp.cumsum(group_tiles)
    start = (tile_end - group_tiles) * tm
    n_valid = tile_end[-1]
    tile_id = jnp.minimum(jnp.arange(n_tiles, dtype=jnp.int32), n_valid - 1)
    tile_exp = jnp.sum(tile_id[:, None] >= tile_end[None, :], axis=1).astype(jnp.int32)

    def dests(info):
        flat = info.reshape(-1, LANES)
        d = [start[flat[:, i].astype(jnp.int32)] + flat[:, r].astype(jnp.int32)
             for i, r in ((INFO_I1, INFO_R1), (INFO_I2, INFO_R2))]
        return d[0], d[1]

    dp = dests(info_p)
    ds = dests(info_s)
    slots = jnp.zeros((n_tiles * tm, HALF), jnp.uint32)
    slots = _dispatch_call(dp[0], dp[1], hw_p.reshape(-1, HALF), slots, tm=1024)
    slots = _dispatch_call(ds[0], ds[1], hw_s.reshape(-1, HALF), slots, tm=xs.shape[1])
    y = _experts_call(j, tile_exp, n_valid.reshape(1), slots, wg, wu, wd)
    xp = _combine_call(dp[0], dp[1], y, info_p, xp, mod_p, post_g, tm=512)
    xs = _combine_call(ds[0], ds[1], y, info_s, xs, mod_s, post_g, tm=xs.shape[1])
    return xp, xs


def _pad_lanes(v):
    return jnp.pad(v, (0, LANES - v.shape[0])).reshape(1, LANES)


def kernel(x_prompt, x_sample, state_ssm, state_conv, c_prompt, c_sample, w_mod, b_mod, mix_pre_g, mix_post_g, ffn_pre_g, ffn_post_g, w_in, conv_w, conv_b, dt_bias, a_log, d_skip, ssm_norm_g, w_ssd_out, cmlp_ln_g, cmlp_ln_b, w_spatial, b_spatial, w_cmlp_out, w_o, ffn_wg, ffn_wu, ffn_wd, router_w, router_b, exp_wg, exp_wu, exp_wd):
    n_prompt = x_prompt.shape[0]
    n_sample = x_sample.shape[0]

    c_all = jnp.concatenate([c_prompt, c_sample, jnp.zeros((8, D_MODEL), F32)], axis=0)
    mod_all = _mod_call(c_all, w_mod, b_mod)
    mod_p = mod_all[:, :n_prompt].reshape(DEPTH, n_prompt, 1, 6 * D_MODEL)
    mod_s = mod_all[:, n_prompt:n_prompt + n_sample].reshape(DEPTH, 1, n_sample, 6 * D_MODEL)

    xp = x_prompt
    xs = x_sample.reshape(1, n_sample, D_MODEL)
    conv_hist = state_conv.reshape(DEPTH, n_sample, (CONV_WIDTH - 1) * CONV_DIM)

    o_xbc = SSM_INNER
    o_dt = o_xbc + CONV_DIM
    o_u = o_dt + SSM_HEADS

    w_main = jnp.concatenate([w_in[:, :, :o_xbc], w_in[:, :, o_u:], w_in[:, :, o_xbc:o_dt]], axis=2).astype(BF16)
    w_dt = jnp.pad(w_in[:, :, o_dt:o_u], ((0, 0), (0, 0), (0, LANES - SSM_HEADS))).astype(BF16)
    wa, wb, wo = (a.astype(BF16) for a in (w_ssd_out, w_cmlp_out, w_o))
    dense_w = tuple(a.astype(BF16) for a in (ffn_wg, ffn_wu, ffn_wd))
    expert_w = tuple(a.astype(BF16) for a in (exp_wg, exp_wu, exp_wd))

    ssm_s = jnp.zeros(state_ssm.shape, F32)
    ssm_p, conv_p, conv_s, v_s = [], [], [], []
    for l in range(DEPTH):
        row = lambda a: a[l].reshape(1, -1)
        mixer_small = (conv_w[l], row(conv_b), _pad_lanes(dt_bias[l]), _pad_lanes(a_log[l]),
                       jnp.repeat(d_skip[l], SSM_HEAD_DIM).reshape(1, -1), row(ssm_norm_g),
                       row(cmlp_ln_g), row(cmlp_ln_b))

        proj_p, dt_p = _inproj_call(l, xp, mod_p[l], row(mix_pre_g), w_main, w_dt, tm=1024)
        proj_s, dt_s = _inproj_call(l, xs, mod_s[l], row(mix_pre_g), w_main, w_dt, tm=n_sample)
        xp, hs_p, cs_p = _mixer_call(l, proj_p, dt_p, xp, mod_p[l], row(mix_post_g), wa, wb, wo,
                                     *mixer_small, w_spatial[l], b_spatial[l].T)
        ya_s, yb_s, vr_s, ssm_s, cs_s = _mixer_step_call(
            l, proj_s[0], dt_s[0], state_ssm, conv_hist[l], ssm_s, *mixer_small,
            jnp.repeat(w_spatial[l, :, 0, 0], CMLP_GROUP_DIM).reshape(1, -1),
            jnp.repeat(b_spatial[l, :, 0], CMLP_GROUP_DIM).reshape(1, -1))
        xs = _outproj_call(l, ya_s[None], yb_s[None], proj_s, xs, mod_s[l], row(mix_post_g), wa, wb, wo,
                           tm=n_sample)

        j = l // 2
        if l % 2 == 0:
            xp = _dense_ffn_call(j, xp, mod_p[l], row(ffn_pre_g), row(ffn_post_g), *dense_w, tm=512)
            xs = _dense_ffn_call(j, xs, mod_s[l], row(ffn_pre_g), row(ffn_post_g), *dense_w, tm=n_sample)
        else:
            xp, xs = _moe_layer(j, xp, xs, mod_p[l], mod_s[l], row(ffn_pre_g), row(ffn_post_g),
                                router_w[j], router_b[j], *expert_w)

        ssm_p.append(hs_p)
        conv_p.append(cs_p)
        conv_s.append(cs_s.reshape(n_sample, CONV_WIDTH - 1, CONV_DIM))
        v_s.append(vr_s.reshape(n_sample, 1, D_MODEL))

    return (xp, xs.reshape(n_sample, 1, D_MODEL), jnp.stack(ssm_p), jnp.stack(conv_p),
            ssm_s, jnp.stack(conv_s), jnp.stack(v_s))
```

```python
import functools

import jax
import jax.numpy as jnp
from jax import lax
from jax.experimental import pallas as pl
from jax.experimental.pallas import tpu as pltpu

F32 = jnp.float32
BF16 = jnp.bfloat16

D_MODEL = 1024
DEPTH = 4
SSM_HEADS = 16
SSM_HEAD_DIM = 64
SSM_GROUPS = 2
SSM_STATE = 128
SSM_INNER = 1024
GROUP_WIDTH = SSM_INNER // SSM_GROUPS
HEADS_PER_GROUP = SSM_HEADS // SSM_GROUPS
CONV_WIDTH = 4
CONV_DIM = 1536
CHUNK = 128
CMLP_GROUPS = 8
CMLP_GROUP_DIM = 128
FFN_DIM = 2816
N_EXPERTS = 8
EPS = 1e-6
LANES = 128
COL_Z, COL_U, COL_V, COL_GA, COL_GB, COL_XS = 0, 1, 2, 3, 4, 5
COL_BC_512 = 12
PROJ_MAIN = 6 * D_MODEL + 2 * SSM_GROUPS * SSM_STATE
PROJ_TILE = 1664
MOD_SH1, MOD_SC1, MOD_G1, MOD_SH2, MOD_SC2, MOD_G2 = range(6)
VMEM_LIMIT = 56 * 1024 * 1024


def _params(semantics, row_dma=False):
    return pltpu.CompilerParams(dimension_semantics=semantics, vmem_limit_bytes=VMEM_LIMIT,
                                disable_bounds_checks=row_dma)


def _rms(x):
    return x * lax.rsqrt(jnp.mean(x * x, axis=-1, keepdims=True) + EPS)


def _dot(a, b):
    return jnp.dot(a, b, preferred_element_type=F32)


def _dot_nt(a, b):
    return lax.dot_general(a, b, (((1,), (1,)), ((), ())), preferred_element_type=F32)


def _dot_tn(a, b):
    return lax.dot_general(a, b, (((0,), (0,)), ((), ())), preferred_element_type=F32)


def _mod_kernel(c_ref, w_ref, b_ref, o_ref):
    a = jax.nn.silu(c_ref[...]).astype(BF16)
    o_ref[0] = _dot(a, w_ref[0].astype(BF16)) + b_ref[0]


def _mod_call(c_all, w_mod, b_mod):
    rows = c_all.shape[0]
    return pl.pallas_call(
        _mod_kernel,
        grid=(DEPTH, 6),
        in_specs=[
            pl.BlockSpec((rows, D_MODEL), lambda l, j: (0, 0)),
            pl.BlockSpec((1, D_MODEL, D_MODEL), lambda l, j: (l, 0, j)),
            pl.BlockSpec((1, 1, D_MODEL), lambda l, j: (l, 0, j)),
        ],
        out_specs=pl.BlockSpec((1, rows, D_MODEL), lambda l, j: (l, 0, j)),
        out_shape=jax.ShapeDtypeStruct((DEPTH, rows, 6 * D_MODEL), F32),
        compiler_params=_params(("arbitrary", "arbitrary")),
        name="adaln_mod",
    )(c_all, w_mod, b_mod.reshape(DEPTH, 1, 6 * D_MODEL))


def _mod_spec(mod, tm, seg, grid_rank):
    per_row = mod.shape[1] != 1
    rows = tm if per_row else 1
    if grid_rank == 2:
        return pl.BlockSpec((1, rows, D_MODEL), lambda b, i: (b, i if per_row else 0, seg))
    return pl.BlockSpec((1, rows, D_MODEL), lambda b, i, j: (b, i if per_row else 0, seg))


def _inproj_kernel(x_ref, g_ref, sh_ref, sc_ref, w_ref, wdt_ref, p_ref, pdt_ref, h_scr):
    @pl.when(pl.program_id(2) == 0)
    def _():
        h = _rms(x_ref[0]) * g_ref[...] * (1.0 + sc_ref[0]) + sh_ref[0]
        hb = h.astype(BF16)
        h_scr[...] = hb
        pdt_ref[0] = _dot(hb, wdt_ref[...])

    p_ref[0] = _dot(h_scr[...], w_ref[...])


def _inproj_call(layer, x, mod, g, w_main, w_dt, tm):
    nb, seq, _ = x.shape
    grid = (nb, seq // tm, PROJ_MAIN // PROJ_TILE)
    return pl.pallas_call(
        _inproj_kernel,
        grid=grid,
        in_specs=[
            pl.BlockSpec((1, tm, D_MODEL), lambda b, i, j: (b, i, 0)),
            pl.BlockSpec((1, D_MODEL), lambda b, i, j: (0, 0)),
            _mod_spec(mod, tm, MOD_SH1, 3),
            _mod_spec(mod, tm, MOD_SC1, 3),
            pl.BlockSpec((None, D_MODEL, PROJ_TILE), lambda b, i, j: (layer, 0, j)),
            pl.BlockSpec((None, D_MODEL, LANES), lambda b, i, j: (layer, 0, 0)),
        ],
        out_specs=[
            pl.BlockSpec((1, tm, PROJ_TILE), lambda b, i, j: (b, i, j)),
            pl.BlockSpec((1, tm, LANES), lambda b, i, j: (b, i, 0)),
        ],
        out_shape=[
            jax.ShapeDtypeStruct((nb, seq, PROJ_MAIN), F32),
            jax.ShapeDtypeStruct((nb, seq, LANES), F32),
        ],
        scratch_shapes=[pltpu.VMEM((tm, D_MODEL), BF16)],
        compiler_params=_params(("arbitrary", "arbitrary", "arbitrary")),
        name="in_proj",
    )(x, g, mod, mod, w_main, w_dt)


def _gated_group_norm(y, z, norm_g):
    y = y * jax.nn.silu(z)
    parts = [_rms(y[:, g * GROUP_WIDTH:(g + 1) * GROUP_WIDTH]) for g in range(SSM_GROUPS)]
    return jnp.concatenate(parts, axis=-1) * norm_g


def _layernorm(x, g, b):
    mu = jnp.mean(x, axis=-1, keepdims=True)
    xc = x - mu
    return xc * lax.rsqrt(jnp.mean(xc * xc, axis=-1, keepdims=True) + EPS) * g + b


def _pair_columns(v, pair, lane_lo):
    h0 = 2 * pair
    return jnp.where(lane_lo, v[:, h0:h0 + 1], v[:, h0 + 1:h0 + 2])


MIX_CHUNKS = 4


def _mixer_chunk(z, u, v, xs, bc, dt_raw,
                 cw_ref, cb_ref, dtb_ref, alog_ref, dskip_ref, ng_ref, lng_ref, lnb_ref, bsp_ref,
                 h_scr, cbuf, wsp_scr, causal, lane_lo, fill):
    cbuf[8:8 + CHUNK, 0:SSM_INNER] = xs
    cbuf[8:8 + CHUNK, SSM_INNER:CONV_DIM] = bc
    acc = cbuf[5:5 + CHUNK, :] * cw_ref[0:1, :]
    for k in range(1, CONV_WIDTH):
        acc = acc + cbuf[5 + k:5 + k + CHUNK, :] * cw_ref[k:k + 1, :]
    cbuf[5:8, :] = cbuf[5 + CHUNK:8 + CHUNK, :]
    fill()

    xbc = jax.nn.silu(acc + cb_ref[...])
    xc = xbc[:, 0:SSM_INNER]
    bm = xbc[:, SSM_INNER:SSM_INNER + SSM_GROUPS * SSM_STATE].astype(BF16)
    cm = xbc[:, SSM_INNER + SSM_GROUPS * SSM_STATE:CONV_DIM].astype(BF16)
    fill()

    dt = jax.nn.softplus(dt_raw + dtb_ref[...])
    da = dt * (-jnp.exp(alog_ref[...]))
    cum = jnp.dot(causal.astype(F32), da, preferred_element_type=F32,
                  precision=lax.Precision.HIGHEST)
    last = cum[CHUNK - 1:CHUNK, :]
    ecum = jnp.exp(cum)
    wend = jnp.exp(last - cum) * dt
    elast = jnp.exp(last)
    cum_t = cum.T
    dt_t = dt.T
    fill()

    y_pairs = []
    for g in range(SSM_GROUPS):
        bg = bm[:, g * SSM_STATE:(g + 1) * SSM_STATE]
        cg = cm[:, g * SSM_STATE:(g + 1) * SSM_STATE]
        cb = _dot_nt(cg, bg)
        hg = h_scr[g * GROUP_WIDTH:(g + 1) * GROUP_WIDTH, :]
        y_state = _dot_nt(cg, hg.astype(BF16))
        xw_parts = []
        for q in range(HEADS_PER_GROUP // 2):
            pair = g * (HEADS_PER_GROUP // 2) + q
            mixes = []
            for h in (2 * pair, 2 * pair + 1):
                seg = cum[:, h:h + 1] - cum_t[h:h + 1, :]
                decay = jnp.where(causal, jnp.exp(seg), 0.0)
                mixes.append((cb * decay * dt_t[h:h + 1, :]).astype(BF16))
            xp = xc[:, pair * LANES:(pair + 1) * LANES]
            rhs = jnp.concatenate([jnp.where(lane_lo, xp, 0.0), jnp.where(lane_lo, 0.0, xp)],
                                  axis=0).astype(BF16)
            y_in = _dot(jnp.concatenate(mixes, axis=1), rhs)
            y_st = y_state[:, q * LANES:(q + 1) * LANES] * _pair_columns(ecum, pair, lane_lo)
            y_pairs.append(y_in + y_st)
            xw_parts.append((xp * _pair_columns(wend, pair, lane_lo)).astype(BF16))
            fill()
        upd = _dot_tn(jnp.concatenate(xw_parts, axis=1), bg)
        for r in range(HEADS_PER_GROUP):
            h = g * HEADS_PER_GROUP + r
            head = slice(h * SSM_HEAD_DIM, (h + 1) * SSM_HEAD_DIM)
            scale = jnp.broadcast_to(elast[0:1, h:h + 1], (SSM_HEAD_DIM, SSM_STATE))
            h_scr[head, :] = h_scr[head, :] * scale + upd[r * SSM_HEAD_DIM:(r + 1) * SSM_HEAD_DIM, :]

    y = jnp.concatenate(y_pairs, axis=1) + dskip_ref[...] * xc
    ya = _gated_group_norm(y, z, ng_ref[...]).astype(BF16)
    fill()

    ug = jax.nn.gelu(u, approximate=True)
    vn = _layernorm(jax.nn.gelu(v, approximate=True), lng_ref[...], lnb_ref[...])
    fill()
    gates = []
    for g in range(CMLP_GROUPS):
        vg = vn[:, g * CMLP_GROUP_DIM:(g + 1) * CMLP_GROUP_DIM].astype(BF16)
        gates.append(_dot(wsp_scr[g], vg) + bsp_ref[:, g:g + 1])
    yb = (ug * jnp.concatenate(gates, axis=1)).astype(BF16)
    return ya, yb


def _merge_project(ya, yb, ga, gb, x, g1, post_g, wa_ref, wb_ref, wo_ref):
    merged = (jax.nn.sigmoid(ga) * _dot(ya, wa_ref[...]) + jax.nn.sigmoid(gb) * _dot(yb, wb_ref[...]))
    o = _dot(merged.astype(BF16), wo_ref[...])
    return x + g1 * (_rms(o) * post_g)


def _mixer_kernel(x_ref, pre_ref, sh_ref, sc_ref, g1_ref, win_ref, wdt_ref,
                  cw_ref, cb_ref, dtb_ref, alog_ref, dskip_ref, ng_ref, lng_ref, lnb_ref, wsp_ref, bsp_ref,
                  pg_ref, wa_ref, wb_ref, wo_ref,
                  o_ref, ssm_ref, conv_ref,
                  h_scr, cbuf, wsp_scr):
    c = pl.program_id(1)
    row = lax.broadcasted_iota(jnp.int32, (CHUNK, CHUNK), 0)
    col = lax.broadcasted_iota(jnp.int32, (CHUNK, CHUNK), 1)
    causal = row >= col
    lane_lo = col < SSM_HEAD_DIM

    @pl.when(c == 0)
    def _():
        h_scr[...] = jnp.zeros_like(h_scr)
        cbuf[0:8, :] = jnp.zeros((8, CONV_DIM), F32)
        for g in range(CMLP_GROUPS):
            wsp_scr[g] = jnp.where(causal, wsp_ref[g], 0.0).astype(BF16)

    def project(hb, block, width=D_MODEL):
        return _dot(hb, win_ref[:, block * D_MODEL:block * D_MODEL + width])

    def projection_pieces(k):
        rows = slice(k * CHUNK, (k + 1) * CHUNK)
        p = {"rows": rows}

        def prep():
            p["x"] = x_ref[0, rows, :]
            p["hb"] = (_rms(p["x"]) * pre_ref[...] * (1.0 + sc_ref[0]) + sh_ref[0]).astype(BF16)

        def piece(name, block, width=D_MODEL):
            return lambda: p.__setitem__(name, project(p["hb"], block, width))

        pieces = [prep, piece("z", COL_Z), piece("u", COL_U), piece("v", COL_V), piece("xs", COL_XS),
                  piece("bc", COL_XS + 1, 2 * SSM_GROUPS * SSM_STATE),
                  lambda: p.__setitem__("dt", _dot(p["hb"], wdt_ref[...])),
                  piece("ga", COL_GA), piece("gb", COL_GB)]
        return p, pieces

    def out_projection(p, ya, yb):
        o_ref[0, p["rows"], :] = _merge_project(ya, yb, p["ga"], p["gb"], p["x"], g1_ref[0], pg_ref[...],
                                                wa_ref, wb_ref, wo_ref)

    cur, pending = projection_pieces(0)
    for piece in pending:
        piece()
    pending = []
    for k in range(MIX_CHUNKS):
        if k + 1 < MIX_CHUNKS:
            nxt, more = projection_pieces(k + 1)
            pending = pending + more

        def fill():
            if pending:
                pending.pop(0)()

        ya, yb = _mixer_chunk(cur["z"], cur["u"], cur["v"], cur["xs"], cur["bc"], cur["dt"],
                              cw_ref, cb_ref, dtb_ref, alog_ref, dskip_ref, ng_ref, lng_ref, lnb_ref, bsp_ref,
                              h_scr, cbuf, wsp_scr, causal, lane_lo, fill)
        while pending:
            fill()
        pending = [functools.partial(out_projection, cur, ya, yb)]
        if k + 1 < MIX_CHUNKS:
            cur = nxt
    pending.pop(0)()

    @pl.when(c == pl.num_programs(1) - 1)
    def _():
        conv_ref[0] = cbuf[5:8, :]
        ssm_ref[0] = h_scr[...].reshape(SSM_HEADS, SSM_HEAD_DIM, SSM_STATE)


def _mixer_call(x, mod, pre_g, post_g, w_main, w_dt, wa, wb, wo,
                conv_w, conv_b, dt_bias, a_log, d_skip, norm_g, ln_g, ln_b, w_sp, b_sp_t):
    nb, seq, _ = x.shape
    tm = MIX_CHUNKS * CHUNK

    def whole(a):
        zeros = (0,) * a.ndim
        return pl.BlockSpec(a.shape, lambda b, c: zeros)

    small = (conv_w, conv_b, dt_bias, a_log, d_skip, norm_g, ln_g, ln_b, w_sp, b_sp_t)
    return pl.pallas_call(
        _mixer_kernel,
        grid=(nb, seq // tm),
        in_specs=[pl.BlockSpec((1, tm, D_MODEL), lambda b, c: (b, c, 0)), whole(pre_g),
                  _mod_spec(mod, tm, MOD_SH1, 2), _mod_spec(mod, tm, MOD_SC1, 2), _mod_spec(mod, tm, MOD_G1, 2),
                  whole(w_main), whole(w_dt)]
                 + [whole(a) for a in small]
                 + [whole(post_g), whole(wa), whole(wb), whole(wo)],
        out_specs=[
            pl.BlockSpec((1, tm, D_MODEL), lambda b, c: (b, c, 0)),
            pl.BlockSpec((1, SSM_HEADS, SSM_HEAD_DIM, SSM_STATE), lambda b, c: (b, 0, 0, 0)),
            pl.BlockSpec((1, CONV_WIDTH - 1, CONV_DIM), lambda b, c: (b, 0, 0)),
        ],
        out_shape=[
            jax.ShapeDtypeStruct(x.shape, F32),
            jax.ShapeDtypeStruct((nb, SSM_HEADS, SSM_HEAD_DIM, SSM_STATE), F32),
            jax.ShapeDtypeStruct((nb, CONV_WIDTH - 1, CONV_DIM), F32),
        ],
        scratch_shapes=[
            pltpu.VMEM((SSM_INNER, SSM_STATE), F32),
            pltpu.VMEM((CHUNK + 8, CONV_DIM), F32),
            pltpu.VMEM((CMLP_GROUPS, CHUNK, CHUNK), BF16),
        ],
        compiler_params=_params(("arbitrary", "arbitrary")),
        name="mixer_prompt",
    )(x, pre_g, mod, mod, mod, w_main, w_dt, *small, post_g, wa, wb, wo)


SAMPLE_TILE = 16


def _mixer_step_kernel(z_ref, u_ref, v_ref, xs_ref, bc_ref, dt_ref, ssm_in_ref, conv_in_ref,
                       cw_ref, cb_ref, dtb_ref, alog_ref, dskip_ref, ng_ref,
                       lng_ref, lnb_ref, wsp0_ref, bsp0_ref, ssm_all_ref,
                       ya_ref, yb_ref, vout_ref, ssm_ref, conv_ref):
    del ssm_all_ref
    tb = SAMPLE_TILE
    xbc_new = jnp.concatenate([xs_ref[...], bc_ref[...]], axis=1)
    hist = conv_in_ref[...]
    acc = xbc_new * cw_ref[CONV_WIDTH - 1:CONV_WIDTH, :]
    for k in range(CONV_WIDTH - 1):
        acc = acc + hist[:, k * CONV_DIM:(k + 1) * CONV_DIM] * cw_ref[k:k + 1, :]
    conv_ref[:, 0:(CONV_WIDTH - 2) * CONV_DIM] = hist[:, CONV_DIM:]
    conv_ref[:, (CONV_WIDTH - 2) * CONV_DIM:] = xbc_new

    xbc = jax.nn.silu(acc + cb_ref[...])
    xc = xbc[:, 0:SSM_INNER]
    bm = xbc[:, SSM_INNER:SSM_INNER + SSM_GROUPS * SSM_STATE]
    cm = xbc[:, SSM_INNER + SSM_GROUPS * SSM_STATE:CONV_DIM]
    dt = jax.nn.softplus(dt_ref[...] + dtb_ref[...])
    dec = jnp.exp(dt * (-jnp.exp(alog_ref[...])))

    def transposed(a):
        pad = jnp.zeros((LANES - tb, a.shape[1]), F32)
        return jnp.concatenate([a, pad], axis=0).T

    xc_t = transposed(xc)
    dt_t = transposed(dt)
    dec_t = transposed(dec)
    row_id = lax.broadcasted_iota(jnp.int32, (tb, SSM_STATE), 0)

    y_groups = [jnp.zeros((tb, GROUP_WIDTH), F32) for _ in range(SSM_GROUPS)]
    for b in range(tb):
        for g in range(SSM_GROUPS):
            b_row = bm[b:b + 1, g * SSM_STATE:(g + 1) * SSM_STATE]
            c_only = jnp.where(row_id == b, cm[:, g * SSM_STATE:(g + 1) * SSM_STATE], 0.0).astype(BF16)
            new_heads = []
            for r in range(HEADS_PER_GROUP):
                h = g * HEADS_PER_GROUP + r
                x_col = xc_t[h * SSM_HEAD_DIM:(h + 1) * SSM_HEAD_DIM, b:b + 1]
                push = x_col * dt_t[h:h + 1, b:b + 1]
                keep = jnp.broadcast_to(dec_t[h:h + 1, b:b + 1], (SSM_HEAD_DIM, SSM_STATE))
                h_new = ssm_in_ref[b, h] * keep + push * b_row
                ssm_ref[b, h] = h_new
                new_heads.append(h_new.astype(BF16))
            hg = jnp.concatenate(new_heads, axis=0)
            y_groups[g] = y_groups[g] + _dot_nt(c_only, hg)

    y = jnp.concatenate(y_groups, axis=1) + dskip_ref[...] * xc
    ya_ref[...] = _gated_group_norm(y, z_ref[...], ng_ref[...]).astype(BF16)

    ug = jax.nn.gelu(u_ref[...], approximate=True)
    vn = _layernorm(jax.nn.gelu(v_ref[...], approximate=True), lng_ref[...], lnb_ref[...])
    vout_ref[...] = vn
    yb_ref[...] = (ug * (vn * wsp0_ref[...] + bsp0_ref[...])).astype(BF16)


def _mixer_step_call(layer, proj, proj_dt, state_ssm, conv_hist, ssm_all, conv_w, conv_b, dt_bias, a_log,
                     d_skip, norm_g, ln_g, ln_b, w_sp0, b_sp0):
    nseq = proj.shape[0]
    tb = SAMPLE_TILE

    def col(block, width=D_MODEL):
        return pl.BlockSpec((tb, width), lambda i: (i, block))

    def whole(a):
        zeros = (0,) * a.ndim
        return pl.BlockSpec(a.shape, lambda i: zeros)

    small = (conv_w, conv_b, dt_bias, a_log, d_skip, norm_g, ln_g, ln_b, w_sp0, b_sp0)
    hist_width = (CONV_WIDTH - 1) * CONV_DIM
    return pl.pallas_call(
        _mixer_step_kernel,
        grid=(nseq // tb,),
        in_specs=[col(COL_Z), col(COL_U), col(COL_V), col(COL_XS),
                  col(COL_BC_512, 2 * SSM_GROUPS * SSM_STATE),
                  pl.BlockSpec((tb, LANES), lambda i: (i, 0)),
                  pl.BlockSpec((None, tb, SSM_HEADS, SSM_HEAD_DIM, SSM_STATE),
                               lambda i: (layer, i, 0, 0, 0)),
                  pl.BlockSpec((tb, hist_width), lambda i: (i, 0))]
                 + [whole(a) for a in small] + [pl.BlockSpec(memory_space=pl.ANY)],
        out_specs=[
            pl.BlockSpec((tb, SSM_INNER), lambda i: (i, 0)),
            pl.BlockSpec((tb, D_MODEL), lambda i: (i, 0)),
            pl.BlockSpec((tb, D_MODEL), lambda i: (i, 0)),
            pl.BlockSpec((None, tb, SSM_HEADS, SSM_HEAD_DIM, SSM_STATE), lambda i: (layer, i, 0, 0, 0)),
            pl.BlockSpec((tb, hist_width), lambda i: (i, 0)),
        ],
        out_shape=[
            jax.ShapeDtypeStruct((nseq, SSM_INNER), BF16),
            jax.ShapeDtypeStruct((nseq, D_MODEL), BF16),
            jax.ShapeDtypeStruct((nseq, D_MODEL), F32),
            jax.ShapeDtypeStruct(ssm_all.shape, F32),
            jax.ShapeDtypeStruct((nseq, hist_width), F32),
        ],
        input_output_aliases={8 + len(small): 3},
        compiler_params=_params(("arbitrary",)),
        name="mixer_sample",
    )(proj, proj, proj, proj, proj, proj_dt, state_ssm, conv_hist, *small, ssm_all)


def _outproj_kernel(ya_ref, yb_ref, ga_ref, gb_ref, x_ref, g1_ref, pg_ref, wa_ref, wb_ref, wo_ref, o_ref):
    o_ref[0] = _merge_project(ya_ref[0], yb_ref[0], ga_ref[0], gb_ref[0], x_ref[0], g1_ref[0], pg_ref[...],
                              wa_ref, wb_ref, wo_ref)


def _outproj_call(layer, ya, yb, proj, x, mod, post_g, wa, wb, wo, tm):
    nb, seq, _ = x.shape

    def rows(block=0):
        return pl.BlockSpec((1, tm, D_MODEL), lambda b, i: (b, i, block))

    def whole(a):
        return pl.BlockSpec(a.shape, lambda b, i: (0, 0))

    def of_layer(a):
        return pl.BlockSpec((None,) + a.shape[1:], lambda b, i: (layer, 0, 0))

    return pl.pallas_call(
        _outproj_kernel,
        grid=(nb, seq // tm),
        in_specs=[rows(), rows(), rows(COL_GA), rows(COL_GB), rows(),
                  _mod_spec(mod, tm, MOD_G1, 2), whole(post_g), of_layer(wa), of_layer(wb), of_layer(wo)],
        out_specs=rows(),
        out_shape=jax.ShapeDtypeStruct(x.shape, F32),
        compiler_params=_params(("arbitrary", "arbitrary")),
        name="out_proj",
    )(ya, yb, proj, proj, x, mod, post_g, wa, wb, wo)


FFN_CHUNK = 256


def _swiglu_into(hb, wg_ref, wu_ref, wd_ref, acc_ref):
    for f in range(FFN_DIM // FFN_CHUNK):
        cols = slice(f * FFN_CHUNK, (f + 1) * FFN_CHUNK)
        act = jax.nn.silu(_dot(hb, wg_ref[0, :, cols])) * _dot(hb, wu_ref[0, :, cols])
        part = _dot(act.astype(BF16), wd_ref[0, cols, :])
        if f == 0:
            acc_ref[...] = part
        else:
            acc_ref[...] += part


def _premod(x_ref, pre_ref, sh_ref, sc_ref):
    return _rms(x_ref[0]) * pre_ref[...] * (1.0 + sc_ref[0]) + sh_ref[0]


def _dense_ffn_kernel(x_ref, pre_ref, post_ref, sh_ref, sc_ref, g2_ref, wg_ref, wu_ref, wd_ref,
                      o_ref, acc_scr):
    hb = _premod(x_ref, pre_ref, sh_ref, sc_ref).astype(BF16)
    _swiglu_into(hb, wg_ref, wu_ref, wd_ref, acc_scr)
    o_ref[0] = x_ref[0] + g2_ref[0] * (_rms(acc_scr[...]) * post_ref[...])


def _dense_ffn_call(j, x, mod, pre_g, post_g, wg, wu, wd, tm):
    nb, seq, _ = x.shape

    def whole(a):
        zeros = (0,) * a.ndim
        return pl.BlockSpec(a.shape, lambda b, i: zeros)

    def of_layer(a):
        return pl.BlockSpec((1,) + a.shape[1:], lambda b, i: (j, 0, 0))

    rows = pl.BlockSpec((1, tm, D_MODEL), lambda b, i: (b, i, 0))
    return pl.pallas_call(
        _dense_ffn_kernel,
        grid=(nb, seq // tm),
        in_specs=[rows, whole(pre_g), whole(post_g),
                  _mod_spec(mod, tm, MOD_SH2, 2), _mod_spec(mod, tm, MOD_SC2, 2),
                  _mod_spec(mod, tm, MOD_G2, 2), of_layer(wg), of_layer(wu), of_layer(wd)],
        out_specs=rows,
        out_shape=jax.ShapeDtypeStruct(x.shape, F32),
        scratch_shapes=[pltpu.VMEM((tm, D_MODEL), F32)],
        compiler_params=_params(("arbitrary", "arbitrary")),
        name="dense_ffn",
    )(x, pre_g, post_g, mod, mod, mod, wg, wu, wd)


EXPERT_TILE = 512
HALF = D_MODEL // 2
INFO_I1, INFO_I2, INFO_P1, INFO_P2, INFO_R1, INFO_R2 = range(6)
HI_MASK = 0xFFFF0000
ISSUE_UNROLL = 8


def _route_kernel(x_ref, pre_ref, sh_ref, sc_ref, rw_ref, rb_ref, cnt0_ref,
                  hw_ref, info_ref, cnt_ref, carry):
    @pl.when(jnp.logical_and(pl.program_id(0) == 0, pl.program_id(1) == 0))
    def _():
        carry[...] = cnt0_ref[...]

    hb = _premod(x_ref, pre_ref, sh_ref, sc_ref).astype(BF16)
    tm = hb.shape[0]
    bits = lax.bitcast_convert_type(hb.astype(F32), jnp.uint32)
    hw_ref[0] = (bits[:, HALF:] & jnp.uint32(HI_MASK)) | (bits[:, :HALF] >> jnp.uint32(16))

    logits = _dot(hb, rw_ref[...].astype(BF16)) + rb_ref[...]
    lane = lax.broadcasted_iota(jnp.int32, logits.shape, 1).astype(F32)
    neg = jnp.float32(-jnp.inf)
    logits = jnp.where(lane < N_EXPERTS, logits, neg)
    m1 = jnp.max(logits, axis=-1, keepdims=True)
    i1 = jnp.min(jnp.where(logits == m1, lane, float(LANES)), axis=-1, keepdims=True)
    rest = jnp.where(lane == i1, neg, logits)
    m2 = jnp.max(rest, axis=-1, keepdims=True)
    i2 = jnp.min(jnp.where(rest == m2, lane, float(LANES)), axis=-1, keepdims=True)
    e2 = jnp.exp(m2 - m1)
    p1 = 1.0 / (1.0 + e2)
    p2 = e2 * p1

    member = jnp.logical_or(lane == i1, lane == i2)
    row = lax.broadcasted_iota(jnp.int32, (tm, tm), 0)
    col = lax.broadcasted_iota(jnp.int32, (tm, tm), 1)
    before = jnp.where(row > col, 1.0, 0.0).astype(BF16)
    ones = jnp.where(member, 1.0, 0.0)
    prior = _dot(before, ones.astype(BF16)) + carry[...]
    r1 = jnp.sum(jnp.where(lane == i1, prior, 0.0), axis=-1, keepdims=True)
    r2 = jnp.sum(jnp.where(lane == i2, prior, 0.0), axis=-1, keepdims=True)
    carry[...] += jnp.sum(ones, axis=0, keepdims=True)
    cnt_ref[...] = carry[...]

    info = jnp.zeros_like(logits)
    for k, v in ((INFO_I1, i1), (INFO_I2, i2), (INFO_P1, p1), (INFO_P2, p2), (INFO_R1, r1), (INFO_R2, r2)):
        info = jnp.where(lane == float(k), v, info)
    info_ref[0] = info


def _route_call(x, mod, pre_g, rw, rb, cnt0, tm):
    nb, seq, _ = x.shape

    def whole(a):
        return pl.BlockSpec(a.shape, lambda b, i: (0, 0))

    return pl.pallas_call(
        _route_kernel,
        grid=(nb, seq // tm),
        in_specs=[pl.BlockSpec((1, tm, D_MODEL), lambda b, i: (b, i, 0)), whole(pre_g),
                  _mod_spec(mod, tm, MOD_SH2, 2), _mod_spec(mod, tm, MOD_SC2, 2),
                  whole(rw), whole(rb), whole(cnt0)],
        out_specs=[pl.BlockSpec((1, tm, HALF), lambda b, i: (b, i, 0)),
                   pl.BlockSpec((1, tm, LANES), lambda b, i: (b, i, 0)),
                   pl.BlockSpec((1, LANES), lambda b, i: (0, 0))],
        out_shape=[jax.ShapeDtypeStruct((nb, seq, HALF), jnp.uint32),
                   jax.ShapeDtypeStruct((nb, seq, LANES), F32),
                   jax.ShapeDtypeStruct((1, LANES), F32)],
        scratch_shapes=[pltpu.VMEM((1, LANES), F32)],
        compiler_params=_params(("arbitrary", "arbitrary")),
        name="moe_route",
    )(x, pre_g, mod, mod, rw, rb, cnt0)


def _row_copy(src, src_row, dst, dst_row, sem):
    return pltpu.make_async_copy(src.at[pl.ds(src_row, 1)], dst.at[pl.ds(dst_row, 1)], sem)


def _dispatch_kernel(d1_ref, d2_ref, hw_ref, xs_in_ref, xs_ref, sem):
    del xs_in_ref
    tm = hw_ref.shape[0]
    base = pl.program_id(0) * tm

    def issue(r, carry):
        _row_copy(hw_ref, r, xs_ref, d1_ref[base + r], sem).start(priority=0)
        _row_copy(hw_ref, r, xs_ref, d2_ref[base + r], sem).start(priority=1)
        return carry

    lax.fori_loop(0, tm, issue, 0, unroll=ISSUE_UNROLL)
    for _ in range(2):
        pltpu.make_async_copy(hw_ref, xs_ref.at[pl.ds(0, tm)], sem).wait()


def _dispatch_call(dest1, dest2, hw, xs, tm):
    rows = hw.shape[0]
    return pl.pallas_call(
        _dispatch_kernel,
        grid_spec=pltpu.PrefetchScalarGridSpec(
            num_scalar_prefetch=2,
            grid=(rows // tm,),
            in_specs=[pl.BlockSpec((tm, HALF), lambda i, d1, d2: (i, 0)),
                      pl.BlockSpec(memory_space=pl.ANY)],
            out_specs=pl.BlockSpec(memory_space=pl.ANY),
            scratch_shapes=[pltpu.SemaphoreType.DMA(())],
        ),
        out_shape=jax.ShapeDtypeStruct(xs.shape, xs.dtype),
        input_output_aliases={3: 0},
        compiler_params=_params(("arbitrary",), row_dma=True),
        name="moe_dispatch",
    )(dest1, dest2, hw, xs)


def _experts_kernel(exp_ref, nv_ref, xs_ref, wg_ref, wu_ref, wd_ref, y_ref):
    del exp_ref
    used = pl.program_id(0) < nv_ref[0]

    @pl.when(used)
    def _():
        words = xs_ref[...]
        lo = lax.bitcast_convert_type(words << jnp.uint32(16), F32)
        hi = lax.bitcast_convert_type(words & jnp.uint32(HI_MASK), F32)
        hb = jnp.concatenate([lo, hi], axis=1).astype(BF16)
        _swiglu_into(hb, wg_ref, wu_ref, wd_ref, y_ref)

    @pl.when(jnp.logical_not(used))
    def _():
        y_ref[...] = jnp.zeros_like(y_ref)


def _experts_call(j, tile_exp, n_valid, xs, wg, wu, wd):
    n_tiles = tile_exp.shape[0]
    tm = EXPERT_TILE
    weight = lambda shape: pl.BlockSpec((None, 1) + shape, lambda i, ex, nv: (j, ex[i], 0, 0))
    return pl.pallas_call(
        _experts_kernel,
        grid_spec=pltpu.PrefetchScalarGridSpec(
            num_scalar_prefetch=2,
            grid=(n_tiles,),
            in_specs=[pl.BlockSpec((tm, HALF), lambda i, ex, nv: (i, 0)),
                      weight((D_MODEL, FFN_DIM)), weight((D_MODEL, FFN_DIM)), weight((FFN_DIM, D_MODEL))],
            out_specs=pl.BlockSpec((tm, D_MODEL), lambda i, ex, nv: (i, 0)),
        ),
        out_shape=jax.ShapeDtypeStruct((xs.shape[0], D_MODEL), F32),
        compiler_params=_params(("arbitrary",)),
        name="moe_experts",
    )(tile_exp, n_valid, xs, wg, wu, wd)


def _combine_kernel(d1_ref, d2_ref, y_ref, info_ref, x_ref, g2_ref, post_ref, o_ref, ybuf, sem):
    tm = x_ref.shape[1]
    base = (pl.program_id(0) * pl.num_programs(1) + pl.program_id(1)) * tm

    def issue(r, carry):
        _row_copy(y_ref, d1_ref[base + r], ybuf.at[0], r, sem).start(priority=0)
        _row_copy(y_ref, d2_ref[base + r], ybuf.at[1], r, sem).start(priority=1)
        return carry

    lax.fori_loop(0, tm, issue, 0, unroll=ISSUE_UNROLL)
    for k in range(2):
        pltpu.make_async_copy(y_ref.at[pl.ds(0, tm)], ybuf.at[k], sem).wait()

    info = info_ref[0]
    f = info[:, INFO_P1:INFO_P1 + 1] * ybuf[0] + info[:, INFO_P2:INFO_P2 + 1] * ybuf[1]
    o_ref[0] = x_ref[0] + g2_ref[0] * (_rms(f) * post_ref[...])


def _combine_call(dest1, dest2, y, info, x, mod, post_g, tm):
    nb, seq, _ = x.shape
    per_row = mod.shape[1] != 1
    return pl.pallas_call(
        _combine_kernel,
        grid_spec=pltpu.PrefetchScalarGridSpec(
            num_scalar_prefetch=2,
            grid=(nb, seq // tm),
            in_specs=[pl.BlockSpec(memory_space=pl.ANY),
                      pl.BlockSpec((1, tm, LANES), lambda b, i, d1, d2: (b, i, 0)),
                      pl.BlockSpec((1, tm, D_MODEL), lambda b, i, d1, d2: (b, i, 0)),
                      pl.BlockSpec((1, tm if per_row else 1, D_MODEL),
                                   lambda b, i, d1, d2: (b, i if per_row else 0, MOD_G2)),
                      pl.BlockSpec(post_g.shape, lambda b, i, d1, d2: (0, 0))],
            out_specs=pl.BlockSpec((1, tm, D_MODEL), lambda b, i, d1, d2: (b, i, 0)),
            scratch_shapes=[pltpu.VMEM((2, tm, D_MODEL), F32), pltpu.SemaphoreType.DMA(())],
        ),
        out_shape=jax.ShapeDtypeStruct(x.shape, F32),
        compiler_params=_params(("arbitrary", "arbitrary"), row_dma=True),
        name="moe_combine",
    )(dest1, dest2, y, info, x, mod, post_g)


def _moe_layer(j, xp, xs, mod_p, mod_s, pre_g, post_g, router_w, router_b, wg, wu, wd):
    rw = jnp.pad(router_w, ((0, 0), (0, LANES - N_EXPERTS)))
    rb = _pad_lanes(router_b)
    hw_p, info_p, cnt_p = _route_call(xp, mod_p, pre_g, rw, rb, jnp.zeros((1, LANES), F32), tm=512)
    hw_s, info_s, cnt = _route_call(xs, mod_s, pre_g, rw, rb, cnt_p, tm=xs.shape[1])

    tm = EXPERT_TILE
    n_assign = 2 * (xp.shape[0] * xp.shape[1] + xs.shape[1])
    n_tiles = n_assign // tm + N_EXPERTS
    counts = cnt[0, :N_EXPERTS].astype(jnp.int32)
    group_tiles = (counts + tm - 1) // tm
    tile_end = jnp.cumsum(group_tiles)
    start = (tile_end - group_tiles) * tm
    n_valid = tile_end[-1]
    tile_id = jnp.minimum(jnp.arange(n_tiles, dtype=jnp.int32), n_valid - 1)
    tile_exp = jnp.sum(tile_id[:, None] >= tile_end[None, :], axis=1).astype(jnp.int32)

    def dests(info):
        flat = info.reshape(-1, LANES)
        d = [start[flat[:, i].astype(jnp.int32)] + flat[:, r].astype(jnp.int32)
             for i, r in ((INFO_I1, INFO_R1), (INFO_I2, INFO_R2))]
        return d[0], d[1]

    dp = dests(info_p)
    ds = dests(info_s)
    slots = jnp.zeros((n_tiles * tm, HALF), jnp.uint32)
    slots = _dispatch_call(dp[0], dp[1], hw_p.reshape(-1, HALF), slots, tm=1024)
    slots = _dispatch_call(ds[0], ds[1], hw_s.reshape(-1, HALF), slots, tm=xs.shape[1])
    y = _experts_call(j, tile_exp, n_valid.reshape(1), slots, wg, wu, wd)
    xp = _combine_call(dp[0], dp[1], y, info_p, xp, mod_p, post_g, tm=512)
    xs = _combine_call(ds[0], ds[1], y, info_s, xs, mod_s, post_g, tm=xs.shape[1])
    return xp, xs


def _pad_lanes(v):
    return jnp.pad(v, (0, LANES - v.shape[0])).reshape(1, LANES)


def kernel(x_prompt, x_sample, state_ssm, state_conv, c_prompt, c_sample, w_mod, b_mod, mix_pre_g, mix_post_g, ffn_pre_g, ffn_post_g, w_in, conv_w, conv_b, dt_bias, a_log, d_skip, ssm_norm_g, w_ssd_out, cmlp_ln_g, cmlp_ln_b, w_spatial, b_spatial, w_cmlp_out, w_o, ffn_wg, ffn_wu, ffn_wd, router_w, router_b, exp_wg, exp_wu, exp_wd):
    n_prompt = x_prompt.shape[0]
    n_sample = x_sample.shape[0]

    c_all = jnp.concatenate([c_prompt, c_sample, jnp.zeros((8, D_MODEL), F32)], axis=0)
    mod_all = _mod_call(c_all, w_mod, b_mod)
    mod_p = mod_all[:, :n_prompt].reshape(DEPTH, n_prompt, 1, 6 * D_MODEL)
    mod_s = mod_all[:, n_prompt:n_prompt + n_sample].reshape(DEPTH, 1, n_sample, 6 * D_MODEL)

    xp = x_prompt
    xs = x_sample.reshape(1, n_sample, D_MODEL)
    conv_hist = state_conv.reshape(DEPTH, n_sample, (CONV_WIDTH - 1) * CONV_DIM)

    o_xbc = SSM_INNER
    o_dt = o_xbc + CONV_DIM
    o_u = o_dt + SSM_HEADS

    w_main = jnp.concatenate([w_in[:, :, :o_xbc], w_in[:, :, o_u:], w_in[:, :, o_xbc:o_dt]], axis=2).astype(BF16)
    w_dt = jnp.pad(w_in[:, :, o_dt:o_u], ((0, 0), (0, 0), (0, LANES - SSM_HEADS))).astype(BF16)
    wa, wb, wo = (a.astype(BF16) for a in (w_ssd_out, w_cmlp_out, w_o))
    dense_w = tuple(a.astype(BF16) for a in (ffn_wg, ffn_wu, ffn_wd))
    expert_w = tuple(a.astype(BF16) for a in (exp_wg, exp_wu, exp_wd))

    ssm_s = jnp.zeros(state_ssm.shape, F32)
    ssm_p, conv_p, conv_s, v_s = [], [], [], []
    for l in range(DEPTH):
        row = lambda a: a[l].reshape(1, -1)
        mixer_small = (conv_w[l], row(conv_b), _pad_lanes(dt_bias[l]), _pad_lanes(a_log[l]),
                       jnp.repeat(d_skip[l], SSM_HEAD_DIM).reshape(1, -1), row(ssm_norm_g),
                       row(cmlp_ln_g), row(cmlp_ln_b))

        proj_s, dt_s = _inproj_call(l, xs, mod_s[l], row(mix_pre_g), w_main, w_dt, tm=n_sample)
        xp, hs_p, cs_p = _mixer_call(xp, mod_p[l], row(mix_pre_g), row(mix_post_g), w_main[l], w_dt[l],
                                     wa[l], wb[l], wo[l], *mixer_small, w_spatial[l], b_spatial[l].T)
        ya_s, yb_s, vr_s, ssm_s, cs_s = _mixer_step_call(
            l, proj_s[0], dt_s[0], state_ssm, conv_hist[l], ssm_s, *mixer_small,
            jnp.repeat(w_spatial[l, :, 0, 0], CMLP_GROUP_DIM).reshape(1, -1),
            jnp.repeat(b_spatial[l, :, 0], CMLP_GROUP_DIM).reshape(1, -1))
        xs = _outproj_call(l, ya_s[None], yb_s[None], proj_s, xs, mod_s[l], row(mix_post_g), wa, wb, wo,
                           tm=n_sample)

        j = l // 2
        if l % 2 == 0:
            xp = _dense_ffn_call(j, xp, mod_p[l], row(ffn_pre_g), row(ffn_post_g), *dense_w, tm=512)
            xs = _dense_ffn_call(j, xs, mod_s[l], row(ffn_pre_g), row(ffn_post_g), *dense_w, tm=n_sample)
        else:
            xp, xs = _moe_layer(j, xp, xs, mod_p[l], mod_s[l], row(ffn_pre_g), row(ffn_post_g),
                                router_w[j], router_b[j], *expert_w)

        ssm_p.append(hs_p)
        conv_p.append(cs_p)
        conv_s.append(cs_s.reshape(n_sample, CONV_WIDTH - 1, CONV_DIM))
        v_s.append(vr_s.reshape(n_sample, 1, D_MODEL))

    return (xp, xs.reshape(n_sample, 1, D_MODEL), jnp.stack(ssm_p), jnp.stack(conv_p),
            ssm_s, jnp.stack(conv_s), jnp.stack(v_s))
```

```python
import functools

import jax
import jax.numpy as jnp
from jax import lax
from jax.experimental import pallas as pl
from jax.experimental.pallas import tpu as pltpu

F32 = jnp.float32
BF16 = jnp.bfloat16

D_MODEL = 1024
DEPTH = 4
SSM_HEADS = 16
SSM_HEAD_DIM = 64
SSM_GROUPS = 2
SSM_STATE = 128
SSM_INNER = 1024
GROUP_WIDTH = SSM_INNER // SSM_GROUPS
HEADS_PER_GROUP = SSM_HEADS // SSM_GROUPS
CONV_WIDTH = 4
CONV_DIM = 1536
CHUNK = 128
CMLP_GROUPS = 8
CMLP_GROUP_DIM = 128
FFN_DIM = 2816
N_EXPERTS = 8
EPS = 1e-6
LANES = 128
COL_Z, COL_U, COL_V, COL_GA, COL_GB, COL_XS = 0, 1, 2, 3, 4, 5
COL_BC_512 = 12
PROJ_MAIN = 6 * D_MODEL + 2 * SSM_GROUPS * SSM_STATE
PROJ_TILE = 1664
MOD_SH1, MOD_SC1, MOD_G1, MOD_SH2, MOD_SC2, MOD_G2 = range(6)
VMEM_LIMIT = 56 * 1024 * 1024


def _params(semantics, row_dma=False):
    return pltpu.CompilerParams(dimension_semantics=semantics, vmem_limit_bytes=VMEM_LIMIT,
                                disable_bounds_checks=row_dma)


def _rms(x):
    return x * lax.rsqrt(jnp.mean(x * x, axis=-1, keepdims=True) + EPS)


def _dot(a, b):
    return jnp.dot(a, b, preferred_element_type=F32)


def _dot_nt(a, b):
    return lax.dot_general(a, b, (((1,), (1,)), ((), ())), preferred_element_type=F32)


def _dot_tn(a, b):
    return lax.dot_general(a, b, (((0,), (0,)), ((), ())), preferred_element_type=F32)


def _mod_kernel(c_ref, w_ref, b_ref, o_ref):
    a = jax.nn.silu(c_ref[...]).astype(BF16)
    o_ref[0] = _dot(a, w_ref[0].astype(BF16)) + b_ref[0]


def _mod_call(c_all, w_mod, b_mod):
    rows = c_all.shape[0]
    return pl.pallas_call(
        _mod_kernel,
        grid=(DEPTH, 6),
        in_specs=[
            pl.BlockSpec((rows, D_MODEL), lambda l, j: (0, 0)),
            pl.BlockSpec((1, D_MODEL, D_MODEL), lambda l, j: (l, 0, j)),
            pl.BlockSpec((1, 1, D_MODEL), lambda l, j: (l, 0, j)),
        ],
        out_specs=pl.BlockSpec((1, rows, D_MODEL), lambda l, j: (l, 0, j)),
        out_shape=jax.ShapeDtypeStruct((DEPTH, rows, 6 * D_MODEL), F32),
        compiler_params=_params(("arbitrary", "arbitrary")),
        name="adaln_mod",
    )(c_all, w_mod, b_mod.reshape(DEPTH, 1, 6 * D_MODEL))


def _mod_spec(mod, tm, seg, grid_rank):
    per_row = mod.shape[1] != 1
    rows = tm if per_row else 1
    if grid_rank == 2:
        return pl.BlockSpec((1, rows, D_MODEL), lambda b, i: (b, i if per_row else 0, seg))
    return pl.BlockSpec((1, rows, D_MODEL), lambda b, i, j: (b, i if per_row else 0, seg))


def _inproj_kernel(x_ref, g_ref, sh_ref, sc_ref, w_ref, wdt_ref, p_ref, pdt_ref, h_scr):
    @pl.when(pl.program_id(2) == 0)
    def _():
        h = _rms(x_ref[0]) * g_ref[...] * (1.0 + sc_ref[0]) + sh_ref[0]
        hb = h.astype(BF16)
        h_scr[...] = hb
        pdt_ref[0] = _dot(hb, wdt_ref[...])

    p_ref[0] = _dot(h_scr[...], w_ref[...])


def _inproj_call(layer, x, mod, g, w_main, w_dt, tm):
    nb, seq, _ = x.shape
    grid = (nb, seq // tm, PROJ_MAIN // PROJ_TILE)
    return pl.pallas_call(
        _inproj_kernel,
        grid=grid,
        in_specs=[
            pl.BlockSpec((1, tm, D_MODEL), lambda b, i, j: (b, i, 0)),
            pl.BlockSpec((1, D_MODEL), lambda b, i, j: (0, 0)),
            _mod_spec(mod, tm, MOD_SH1, 3),
            _mod_spec(mod, tm, MOD_SC1, 3),
            pl.BlockSpec((None, D_MODEL, PROJ_TILE), lambda b, i, j: (layer, 0, j)),
            pl.BlockSpec((None, D_MODEL, LANES), lambda b, i, j: (layer, 0, 0)),
        ],
        out_specs=[
            pl.BlockSpec((1, tm, PROJ_TILE), lambda b, i, j: (b, i, j)),
            pl.BlockSpec((1, tm, LANES), lambda b, i, j: (b, i, 0)),
        ],
        out_shape=[
            jax.ShapeDtypeStruct((nb, seq, PROJ_MAIN), F32),
            jax.ShapeDtypeStruct((nb, seq, LANES), F32),
        ],
        scratch_shapes=[pltpu.VMEM((tm, D_MODEL), BF16)],
        compiler_params=_params(("arbitrary", "arbitrary", "arbitrary")),
        name="in_proj",
    )(x, g, mod, mod, w_main, w_dt)


def _gated_group_norm(y, z, norm_g):
    y = y * jax.nn.silu(z)
    parts = [_rms(y[:, g * GROUP_WIDTH:(g + 1) * GROUP_WIDTH]) for g in range(SSM_GROUPS)]
    return jnp.concatenate(parts, axis=-1) * norm_g


def _layernorm(x, g, b):
    mu = jnp.mean(x, axis=-1, keepdims=True)
    xc = x - mu
    return xc * lax.rsqrt(jnp.mean(xc * xc, axis=-1, keepdims=True) + EPS) * g + b


def _pair_columns(v, pair, lane_lo):
    h0 = 2 * pair
    return jnp.where(lane_lo, v[:, h0:h0 + 1], v[:, h0 + 1:h0 + 2])


MIX_CHUNKS = 4
P0_COLS = {"z": (COL_Z * D_MODEL, D_MODEL), "u": (COL_U * D_MODEL, D_MODEL), "v": (COL_V * D_MODEL, D_MODEL),
           "ga": (COL_GA * D_MODEL, D_MODEL), "gb": (COL_GB * D_MODEL, D_MODEL),
           "xs": (COL_XS * D_MODEL, D_MODEL), "bc": (COL_BC_512 * 512, 2 * SSM_GROUPS * SSM_STATE),
           "dt": (PROJ_MAIN, LANES)}


def _mixer_chunk(z, u, v, xs, bc, dt_raw,
                 cw_ref, cb_ref, dtb_ref, alog_ref, dskip_ref, ng_ref, lng_ref, lnb_ref, bsp_ref,
                 h_scr, cbuf, wsp_scr, causal, lane_lo, fill):
    cbuf[8:8 + CHUNK, 0:SSM_INNER] = xs
    cbuf[8:8 + CHUNK, SSM_INNER:CONV_DIM] = bc
    acc = cbuf[5:5 + CHUNK, :] * cw_ref[0:1, :]
    for k in range(1, CONV_WIDTH):
        acc = acc + cbuf[5 + k:5 + k + CHUNK, :] * cw_ref[k:k + 1, :]
    cbuf[5:8, :] = cbuf[5 + CHUNK:8 + CHUNK, :]
    fill()

    xbc = jax.nn.silu(acc + cb_ref[...])
    xc = xbc[:, 0:SSM_INNER]
    bm = xbc[:, SSM_INNER:SSM_INNER + SSM_GROUPS * SSM_STATE].astype(BF16)
    cm = xbc[:, SSM_INNER + SSM_GROUPS * SSM_STATE:CONV_DIM].astype(BF16)
    fill()

    dt = jax.nn.softplus(dt_raw + dtb_ref[...])
    da = dt * (-jnp.exp(alog_ref[...]))
    cum = jnp.dot(causal.astype(F32), da, preferred_element_type=F32,
                  precision=lax.Precision.HIGHEST)
    last = cum[CHUNK - 1:CHUNK, :]
    ecum = jnp.exp(cum)
    wend = jnp.exp(last - cum) * dt
    elast = jnp.exp(last)
    cum_t = cum.T
    dt_t = dt.T
    fill()

    y_pairs = []
    for g in range(SSM_GROUPS):
        bg = bm[:, g * SSM_STATE:(g + 1) * SSM_STATE]
        cg = cm[:, g * SSM_STATE:(g + 1) * SSM_STATE]
        cb = _dot_nt(cg, bg)
        hg = h_scr[g * GROUP_WIDTH:(g + 1) * GROUP_WIDTH, :]
        y_state = _dot_nt(cg, hg.astype(BF16))
        xw_parts = []
        for q in range(HEADS_PER_GROUP // 2):
            pair = g * (HEADS_PER_GROUP // 2) + q
            mixes = []
            for h in (2 * pair, 2 * pair + 1):
                seg = cum[:, h:h + 1] - cum_t[h:h + 1, :]
                decay = jnp.where(causal, jnp.exp(seg), 0.0)
                mixes.append((cb * decay * dt_t[h:h + 1, :]).astype(BF16))
            xp = xc[:, pair * LANES:(pair + 1) * LANES]
            rhs = jnp.concatenate([jnp.where(lane_lo, xp, 0.0), jnp.where(lane_lo, 0.0, xp)],
                                  axis=0).astype(BF16)
            y_in = _dot(jnp.concatenate(mixes, axis=1), rhs)
            y_st = y_state[:, q * LANES:(q + 1) * LANES] * _pair_columns(ecum, pair, lane_lo)
            y_pairs.append(y_in + y_st)
            xw_parts.append((xp * _pair_columns(wend, pair, lane_lo)).astype(BF16))
            fill()
        upd = _dot_tn(jnp.concatenate(xw_parts, axis=1), bg)
        for r in range(HEADS_PER_GROUP):
            h = g * HEADS_PER_GROUP + r
            head = slice(h * SSM_HEAD_DIM, (h + 1) * SSM_HEAD_DIM)
            scale = jnp.broadcast_to(elast[0:1, h:h + 1], (SSM_HEAD_DIM, SSM_STATE))
            h_scr[head, :] = h_scr[head, :] * scale + upd[r * SSM_HEAD_DIM:(r + 1) * SSM_HEAD_DIM, :]

    y = jnp.concatenate(y_pairs, axis=1) + dskip_ref[...] * xc
    ya = _gated_group_norm(y, z, ng_ref[...]).astype(BF16)
    fill()

    ug = jax.nn.gelu(u, approximate=True)
    vn = _layernorm(jax.nn.gelu(v, approximate=True), lng_ref[...], lnb_ref[...])
    fill()
    gates = []
    for g in range(CMLP_GROUPS):
        vg = vn[:, g * CMLP_GROUP_DIM:(g + 1) * CMLP_GROUP_DIM].astype(BF16)
        gates.append(_dot(wsp_scr[g], vg) + bsp_ref[:, g:g + 1])
    yb = (ug * jnp.concatenate(gates, axis=1)).astype(BF16)
    return ya, yb


def _merge_project(ya, yb, ga, gb, x, g1, post_g, wa_ref, wb_ref, wo_ref):
    merged = (jax.nn.sigmoid(ga) * _dot(ya, wa_ref[...]) + jax.nn.sigmoid(gb) * _dot(yb, wb_ref[...]))
    o = _dot(merged.astype(BF16), wo_ref[...])
    return x + g1 * (_rms(o) * post_g)


def _mixer_kernel(x_ref, pre_ref, sh_ref, sc_ref, g1_ref, xn_ref, shn_ref, scn_ref, win_ref, wdt_ref,
                  cw_ref, cb_ref, dtb_ref, alog_ref, dskip_ref, ng_ref, lng_ref, lnb_ref, wsp_ref, bsp_ref,
                  pg_ref, wa_ref, wb_ref, wo_ref,
                  o_ref, ssm_ref, conv_ref,
                  h_scr, cbuf, wsp_scr, p0_scr):
    c = pl.program_id(1)
    row = lax.broadcasted_iota(jnp.int32, (CHUNK, CHUNK), 0)
    col = lax.broadcasted_iota(jnp.int32, (CHUNK, CHUNK), 1)
    causal = row >= col
    lane_lo = col < SSM_HEAD_DIM

    @pl.when(c == 0)
    def _():
        h_scr[...] = jnp.zeros_like(h_scr)
        cbuf[0:8, :] = jnp.zeros((8, CONV_DIM), F32)
        for g in range(CMLP_GROUPS):
            wsp_scr[g] = jnp.where(causal, wsp_ref[g], 0.0).astype(BF16)

    def project(hb, block, width=D_MODEL):
        return _dot(hb, win_ref[:, block * D_MODEL:block * D_MODEL + width])

    def projection_pieces(load_x, load_sh, load_sc, store):
        t = {}

        def prep():
            x = load_x()
            store("x", x)
            t["hb"] = (_rms(x) * pre_ref[...] * (1.0 + load_sc()) + load_sh()).astype(BF16)

        pieces = [prep]
        for name, block in (("z", COL_Z), ("u", COL_U), ("v", COL_V), ("xs", COL_XS), ("bc", COL_XS + 1),
                            ("ga", COL_GA), ("gb", COL_GB)):
            width = P0_COLS[name][1]
            pieces.append(lambda name=name, block=block, width=width:
                          store(name, project(t["hb"], block, width)))
        pieces.append(lambda: store("dt", _dot(t["hb"], wdt_ref[...])))
        return pieces

    def chunk_pieces(k):
        rows = slice(k * CHUNK, (k + 1) * CHUNK)
        p = {"rows": rows}
        return p, projection_pieces(lambda: x_ref[0, rows, :], lambda: sh_ref[0], lambda: sc_ref[0],
                                    p.__setitem__)

    def store_first(name, value):
        if name != "x":
            start, width = P0_COLS[name]
            p0_scr[:, start:start + width] = value

    def out_projection(p, ya, yb):
        o_ref[0, p["rows"], :] = _merge_project(ya, yb, p["ga"], p["gb"], p["x"], g1_ref[0], pg_ref[...],
                                                wa_ref, wb_ref, wo_ref)

    @pl.when(jnp.logical_and(pl.program_id(0) == 0, c == 0))
    def _():
        for piece in projection_pieces(lambda: x_ref[0, 0:CHUNK, :], lambda: sh_ref[0], lambda: sc_ref[0],
                                       store_first):
            piece()

    cur = {"rows": slice(0, CHUNK), "x": x_ref[0, 0:CHUNK, :]}
    for name, (start, width) in P0_COLS.items():
        cur[name] = p0_scr[:, start:start + width]
    pending = []
    for k in range(MIX_CHUNKS):
        if k + 1 < MIX_CHUNKS:
            nxt, more = chunk_pieces(k + 1)
        else:
            more = projection_pieces(lambda: xn_ref[0], lambda: shn_ref[0], lambda: scn_ref[0], store_first)
        pending = pending + more

        def fill():
            if pending:
                pending.pop(0)()

        ya, yb = _mixer_chunk(cur["z"], cur["u"], cur["v"], cur["xs"], cur["bc"], cur["dt"],
                              cw_ref, cb_ref, dtb_ref, alog_ref, dskip_ref, ng_ref, lng_ref, lnb_ref, bsp_ref,
                              h_scr, cbuf, wsp_scr, causal, lane_lo, fill)
        while pending:
            fill()
        pending = [functools.partial(out_projection, cur, ya, yb)]
        if k + 1 < MIX_CHUNKS:
            cur = nxt
    pending.pop(0)()

    @pl.when(c == pl.num_programs(1) - 1)
    def _():
        conv_ref[0] = cbuf[5:8, :]
        ssm_ref[0] = h_scr[...].reshape(SSM_HEADS, SSM_HEAD_DIM, SSM_STATE)


def _mixer_call(x, mod, pre_g, post_g, w_main, w_dt, wa, wb, wo,
                conv_w, conv_b, dt_bias, a_log, d_skip, norm_g, ln_g, ln_b, w_sp, b_sp_t):
    nb, seq, _ = x.shape
    tm = MIX_CHUNKS * CHUNK

    def whole(a):
        zeros = (0,) * a.ndim
        return pl.BlockSpec(a.shape, lambda b, c: zeros)

    nsteps = seq // tm

    def following(b, c):
        lin = jnp.minimum(b * nsteps + c + 1, nb * nsteps - 1)
        return lin // nsteps, lin % nsteps

    def next_x(b, c):
        bn, cn = following(b, c)
        return bn, cn * MIX_CHUNKS, 0

    def next_mod(seg):
        return pl.BlockSpec((1, 1, D_MODEL), lambda b, c: (following(b, c)[0], 0, seg))

    small = (conv_w, conv_b, dt_bias, a_log, d_skip, norm_g, ln_g, ln_b, w_sp, b_sp_t)
    return pl.pallas_call(
        _mixer_kernel,
        grid=(nb, nsteps),
        in_specs=[pl.BlockSpec((1, tm, D_MODEL), lambda b, c: (b, c, 0)), whole(pre_g),
                  _mod_spec(mod, tm, MOD_SH1, 2), _mod_spec(mod, tm, MOD_SC1, 2), _mod_spec(mod, tm, MOD_G1, 2),
                  pl.BlockSpec((1, CHUNK, D_MODEL), next_x), next_mod(MOD_SH1), next_mod(MOD_SC1),
                  whole(w_main), whole(w_dt)]
                 + [whole(a) for a in small]
                 + [whole(post_g), whole(wa), whole(wb), whole(wo)],
        out_specs=[
            pl.BlockSpec((1, tm, D_MODEL), lambda b, c: (b, c, 0)),
            pl.BlockSpec((1, SSM_HEADS, SSM_HEAD_DIM, SSM_STATE), lambda b, c: (b, 0, 0, 0)),
            pl.BlockSpec((1, CONV_WIDTH - 1, CONV_DIM), lambda b, c: (b, 0, 0)),
        ],
        out_shape=[
            jax.ShapeDtypeStruct(x.shape, F32),
            jax.ShapeDtypeStruct((nb, SSM_HEADS, SSM_HEAD_DIM, SSM_STATE), F32),
            jax.ShapeDtypeStruct((nb, CONV_WIDTH - 1, CONV_DIM), F32),
        ],
        scratch_shapes=[
            pltpu.VMEM((SSM_INNER, SSM_STATE), F32),
            pltpu.VMEM((CHUNK + 8, CONV_DIM), F32),
            pltpu.VMEM((CMLP_GROUPS, CHUNK, CHUNK), BF16),
            pltpu.VMEM((CHUNK, PROJ_MAIN + LANES), F32),
        ],
        compiler_params=_params(("arbitrary", "arbitrary")),
        name="mixer_prompt",
    )(x, pre_g, mod, mod, mod, x, mod, mod, w_main, w_dt, *small, post_g, wa, wb, wo)


SAMPLE_TILE = 16


def _mixer_step_kernel(z_ref, u_ref, v_ref, xs_ref, bc_ref, dt_ref, ssm_in_ref, conv_in_ref,
                       cw_ref, cb_ref, dtb_ref, alog_ref, dskip_ref, ng_ref,
                       lng_ref, lnb_ref, wsp0_ref, bsp0_ref, ssm_all_ref,
                       ya_ref, yb_ref, vout_ref, ssm_ref, conv_ref):
    del ssm_all_ref
    tb = SAMPLE_TILE
    xbc_new = jnp.concatenate([xs_ref[...], bc_ref[...]], axis=1)
    hist = conv_in_ref[...]
    acc = xbc_new * cw_ref[CONV_WIDTH - 1:CONV_WIDTH, :]
    for k in range(CONV_WIDTH - 1):
        acc = acc + hist[:, k * CONV_DIM:(k + 1) * CONV_DIM] * cw_ref[k:k + 1, :]
    conv_ref[:, 0:(CONV_WIDTH - 2) * CONV_DIM] = hist[:, CONV_DIM:]
    conv_ref[:, (CONV_WIDTH - 2) * CONV_DIM:] = xbc_new

    xbc = jax.nn.silu(acc + cb_ref[...])
    xc = xbc[:, 0:SSM_INNER]
    bm = xbc[:, SSM_INNER:SSM_INNER + SSM_GROUPS * SSM_STATE]
    cm = xbc[:, SSM_INNER + SSM_GROUPS * SSM_STATE:CONV_DIM]
    dt = jax.nn.softplus(dt_ref[...] + dtb_ref[...])
    dec = jnp.exp(dt * (-jnp.exp(alog_ref[...])))

    def transposed(a):
        pad = jnp.zeros((LANES - tb, a.shape[1]), F32)
        return jnp.concatenate([a, pad], axis=0).T

    xc_t = transposed(xc)
    dt_t = transposed(dt)
    dec_t = transposed(dec)
    row_id = lax.broadcasted_iota(jnp.int32, (tb, SSM_STATE), 0)

    y_groups = [jnp.zeros((tb, GROUP_WIDTH), F32) for _ in range(SSM_GROUPS)]
    for b in range(tb):
        for g in range(SSM_GROUPS):
            b_row = bm[b:b + 1, g * SSM_STATE:(g + 1) * SSM_STATE]
            c_only = jnp.where(row_id == b, cm[:, g * SSM_STATE:(g + 1) * SSM_STATE], 0.0).astype(BF16)
            new_heads = []
            for r in range(HEADS_PER_GROUP):
                h = g * HEADS_PER_GROUP + r
                x_col = xc_t[h * SSM_HEAD_DIM:(h + 1) * SSM_HEAD_DIM, b:b + 1]
                push = x_col * dt_t[h:h + 1, b:b + 1]
                keep = jnp.broadcast_to(dec_t[h:h + 1, b:b + 1], (SSM_HEAD_DIM, SSM_STATE))
                h_new = ssm_in_ref[b, h] * keep + push * b_row
                ssm_ref[b, h] = h_new
                new_heads.append(h_new.astype(BF16))
            hg = jnp.concatenate(new_heads, axis=0)
            y_groups[g] = y_groups[g] + _dot_nt(c_only, hg)

    y = jnp.concatenate(y_groups, axis=1) + dskip_ref[...] * xc
    ya_ref[...] = _gated_group_norm(y, z_ref[...], ng_ref[...]).astype(BF16)

    ug = jax.nn.gelu(u_ref[...], approximate=True)
    vn = _layernorm(jax.nn.gelu(v_ref[...], approximate=True), lng_ref[...], lnb_ref[...])
    vout_ref[...] = vn
    yb_ref[...] = (ug * (vn * wsp0_ref[...] + bsp0_ref[...])).astype(BF16)


def _mixer_step_call(layer, proj, proj_dt, state_ssm, conv_hist, ssm_all, conv_w, conv_b, dt_bias, a_log,
                     d_skip, norm_g, ln_g, ln_b, w_sp0, b_sp0):
    nseq = proj.shape[0]
    tb = SAMPLE_TILE

    def col(block, width=D_MODEL):
        return pl.BlockSpec((tb, width), lambda i: (i, block))

    def whole(a):
        zeros = (0,) * a.ndim
        return pl.BlockSpec(a.shape, lambda i: zeros)

    small = (conv_w, conv_b, dt_bias, a_log, d_skip, norm_g, ln_g, ln_b, w_sp0, b_sp0)
    hist_width = (CONV_WIDTH - 1) * CONV_DIM
    return pl.pallas_call(
        _mixer_step_kernel,
        grid=(nseq // tb,),
        in_specs=[col(COL_Z), col(COL_U), col(COL_V), col(COL_XS),
                  col(COL_BC_512, 2 * SSM_GROUPS * SSM_STATE),
                  pl.BlockSpec((tb, LANES), lambda i: (i, 0)),
                  pl.BlockSpec((None, tb, SSM_HEADS, SSM_HEAD_DIM, SSM_STATE),
                               lambda i: (layer, i, 0, 0, 0)),
                  pl.BlockSpec((tb, hist_width), lambda i: (i, 0))]
                 + [whole(a) for a in small] + [pl.BlockSpec(memory_space=pl.ANY)],
        out_specs=[
            pl.BlockSpec((tb, SSM_INNER), lambda i: (i, 0)),
            pl.BlockSpec((tb, D_MODEL), lambda i: (i, 0)),
            pl.BlockSpec((tb, D_MODEL), lambda i: (i, 0)),
            pl.BlockSpec((None, tb, SSM_HEADS, SSM_HEAD_DIM, SSM_STATE), lambda i: (layer, i, 0, 0, 0)),
            pl.BlockSpec((tb, hist_width), lambda i: (i, 0)),
        ],
        out_shape=[
            jax.ShapeDtypeStruct((nseq, SSM_INNER), BF16),
            jax.ShapeDtypeStruct((nseq, D_MODEL), BF16),
            jax.ShapeDtypeStruct((nseq, D_MODEL), F32),
            jax.ShapeDtypeStruct(ssm_all.shape, F32),
            jax.ShapeDtypeStruct((nseq, hist_width), F32),
        ],
        input_output_aliases={8 + len(small): 3},
        compiler_params=_params(("arbitrary",)),
        name="mixer_sample",
    )(proj, proj, proj, proj, proj, proj_dt, state_ssm, conv_hist, *small, ssm_all)


def _outproj_kernel(ya_ref, yb_ref, ga_ref, gb_ref, x_ref, g1_ref, pg_ref, wa_ref, wb_ref, wo_ref, o_ref):
    o_ref[0] = _merge_project(ya_ref[0], yb_ref[0], ga_ref[0], gb_ref[0], x_ref[0], g1_ref[0], pg_ref[...],
                              wa_ref, wb_ref, wo_ref)


def _outproj_call(layer, ya, yb, proj, x, mod, post_g, wa, wb, wo, tm):
    nb, seq, _ = x.shape

    def rows(block=0):
        return pl.BlockSpec((1, tm, D_MODEL), lambda b, i: (b, i, block))

    def whole(a):
        return pl.BlockSpec(a.shape, lambda b, i: (0, 0))

    def of_layer(a):
        return pl.BlockSpec((None,) + a.shape[1:], lambda b, i: (layer, 0, 0))

    return pl.pallas_call(
        _outproj_kernel,
        grid=(nb, seq // tm),
        in_specs=[rows(), rows(), rows(COL_GA), rows(COL_GB), rows(),
                  _mod_spec(mod, tm, MOD_G1, 2), whole(post_g), of_layer(wa), of_layer(wb), of_layer(wo)],
        out_specs=rows(),
        out_shape=jax.ShapeDtypeStruct(x.shape, F32),
        compiler_params=_params(("arbitrary", "arbitrary")),
        name="out_proj",
    )(ya, yb, proj, proj, x, mod, post_g, wa, wb, wo)


FFN_CHUNK = 256


def _swiglu_into(hb, wg_ref, wu_ref, wd_ref, acc_ref):
    for f in range(FFN_DIM // FFN_CHUNK):
        cols = slice(f * FFN_CHUNK, (f + 1) * FFN_CHUNK)
        act = jax.nn.silu(_dot(hb, wg_ref[0, :, cols])) * _dot(hb, wu_ref[0, :, cols])
        part = _dot(act.astype(BF16), wd_ref[0, cols, :])
        if f == 0:
            acc_ref[...] = part
        else:
            acc_ref[...] += part


def _premod(x_ref, pre_ref, sh_ref, sc_ref):
    return _rms(x_ref[0]) * pre_ref[...] * (1.0 + sc_ref[0]) + sh_ref[0]


def _dense_ffn_kernel(x_ref, pre_ref, post_ref, sh_ref, sc_ref, g2_ref, wg_ref, wu_ref, wd_ref,
                      o_ref, acc_scr):
    hb = _premod(x_ref, pre_ref, sh_ref, sc_ref).astype(BF16)
    _swiglu_into(hb, wg_ref, wu_ref, wd_ref, acc_scr)
    o_ref[0] = x_ref[0] + g2_ref[0] * (_rms(acc_scr[...]) * post_ref[...])


def _dense_ffn_call(j, x, mod, pre_g, post_g, wg, wu, wd, tm):
    nb, seq, _ = x.shape

    def whole(a):
        zeros = (0,) * a.ndim
        return pl.BlockSpec(a.shape, lambda b, i: zeros)

    def of_layer(a):
        return pl.BlockSpec((1,) + a.shape[1:], lambda b, i: (j, 0, 0))

    rows = pl.BlockSpec((1, tm, D_MODEL), lambda b, i: (b, i, 0))
    return pl.pallas_call(
        _dense_ffn_kernel,
        grid=(nb, seq // tm),
        in_specs=[rows, whole(pre_g), whole(post_g),
                  _mod_spec(mod, tm, MOD_SH2, 2), _mod_spec(mod, tm, MOD_SC2, 2),
                  _mod_spec(mod, tm, MOD_G2, 2), of_layer(wg), of_layer(wu), of_layer(wd)],
        out_specs=rows,
        out_shape=jax.ShapeDtypeStruct(x.shape, F32),
        scratch_shapes=[pltpu.VMEM((tm, D_MODEL), F32)],
        compiler_params=_params(("arbitrary", "arbitrary")),
        name="dense_ffn",
    )(x, pre_g, post_g, mod, mod, mod, wg, wu, wd)


EXPERT_TILE = 512
HALF = D_MODEL // 2
INFO_I1, INFO_I2, INFO_P1, INFO_P2, INFO_R1, INFO_R2 = range(6)
HI_MASK = 0xFFFF0000
ISSUE_UNROLL = 8


def _route_kernel(x_ref, pre_ref, sh_ref, sc_ref, rw_ref, rb_ref, cnt0_ref,
                  hw_ref, info_ref, cnt_ref, carry):
    @pl.when(jnp.logical_and(pl.program_id(0) == 0, pl.program_id(1) == 0))
    def _():
        carry[...] = cnt0_ref[...]

    hb = _premod(x_ref, pre_ref, sh_ref, sc_ref).astype(BF16)
    tm = hb.shape[0]
    bits = lax.bitcast_convert_type(hb.astype(F32), jnp.uint32)
    hw_ref[0] = (bits[:, HALF:] & jnp.uint32(HI_MASK)) | (bits[:, :HALF] >> jnp.uint32(16))

    logits = _dot(hb, rw_ref[...].astype(BF16)) + rb_ref[...]
    lane = lax.broadcasted_iota(jnp.int32, logits.shape, 1).astype(F32)
    neg = jnp.float32(-jnp.inf)
    logits = jnp.where(lane < N_EXPERTS, logits, neg)
    m1 = jnp.max(logits, axis=-1, keepdims=True)
    i1 = jnp.min(jnp.where(logits == m1, lane, float(LANES)), axis=-1, keepdims=True)
    rest = jnp.where(lane == i1, neg, logits)
    m2 = jnp.max(rest, axis=-1, keepdims=True)
    i2 = jnp.min(jnp.where(rest == m2, lane, float(LANES)), axis=-1, keepdims=True)
    e2 = jnp.exp(m2 - m1)
    p1 = 1.0 / (1.0 + e2)
    p2 = e2 * p1

    member = jnp.logical_or(lane == i1, lane == i2)
    row = lax.broadcasted_iota(jnp.int32, (tm, tm), 0)
    col = lax.broadcasted_iota(jnp.int32, (tm, tm), 1)
    before = jnp.where(row > col, 1.0, 0.0).astype(BF16)
    ones = jnp.where(member, 1.0, 0.0)
    prior = _dot(before, ones.astype(BF16)) + carry[...]
    r1 = jnp.sum(jnp.where(lane == i1, prior, 0.0), axis=-1, keepdims=True)
    r2 = jnp.sum(jnp.where(lane == i2, prior, 0.0), axis=-1, keepdims=True)
    carry[...] += jnp.sum(ones, axis=0, keepdims=True)
    cnt_ref[...] = carry[...]

    info = jnp.zeros_like(logits)
    for k, v in ((INFO_I1, i1), (INFO_I2, i2), (INFO_P1, p1), (INFO_P2, p2), (INFO_R1, r1), (INFO_R2, r2)):
        info = jnp.where(lane == float(k), v, info)
    info_ref[0] = info


def _route_call(x, mod, pre_g, rw, rb, cnt0, tm):
    nb, seq, _ = x.shape

    def whole(a):
        return pl.BlockSpec(a.shape, lambda b, i: (0, 0))

    return pl.pallas_call(
        _route_kernel,
        grid=(nb, seq // tm),
        in_specs=[pl.BlockSpec((1, tm, D_MODEL), lambda b, i: (b, i, 0)), whole(pre_g),
                  _mod_spec(mod, tm, MOD_SH2, 2), _mod_spec(mod, tm, MOD_SC2, 2),
                  whole(rw), whole(rb), whole(cnt0)],
        out_specs=[pl.BlockSpec((1, tm, HALF), lambda b, i: (b, i, 0)),
                   pl.BlockSpec((1, tm, LANES), lambda b, i: (b, i, 0)),
                   pl.BlockSpec((1, LANES), lambda b, i: (0, 0))],
        out_shape=[jax.ShapeDtypeStruct((nb, seq, HALF), jnp.uint32),
                   jax.ShapeDtypeStruct((nb, seq, LANES), F32),
                   jax.ShapeDtypeStruct((1, LANES), F32)],
        scratch_shapes=[pltpu.VMEM((1, LANES), F32)],
        compiler_params=_params(("arbitrary", "arbitrary")),
        name="moe_route",
    )(x, pre_g, mod, mod, rw, rb, cnt0)


def _row_copy(src, src_row, dst, dst_row, sem):
    return pltpu.make_async_copy(src.at[pl.ds(src_row, 1)], dst.at[pl.ds(dst_row, 1)], sem)


def _dispatch_kernel(d1_ref, d2_ref, hw_ref, xs_in_ref, xs_ref, sem):
    del xs_in_ref
    tm = hw_ref.shape[0]
    base = pl.program_id(0) * tm

    def issue(r, carry):
        _row_copy(hw_ref, r, xs_ref, d1_ref[base + r], sem).start(priority=0)
        _row_copy(hw_ref, r, xs_ref, d2_ref[base + r], sem).start(priority=1)
        return carry

    lax.fori_loop(0, tm, issue, 0, unroll=ISSUE_UNROLL)
    for _ in range(2):
        pltpu.make_async_copy(hw_ref, xs_ref.at[pl.ds(0, tm)], sem).wait()


def _dispatch_call(dest1, dest2, hw, xs, tm):
    rows = hw.shape[0]
    return pl.pallas_call(
        _dispatch_kernel,
        grid_spec=pltpu.PrefetchScalarGridSpec(
            num_scalar_prefetch=2,
            grid=(rows // tm,),
            in_specs=[pl.BlockSpec((tm, HALF), lambda i, d1, d2: (i, 0)),
                      pl.BlockSpec(memory_space=pl.ANY)],
            out_specs=pl.BlockSpec(memory_space=pl.ANY),
            scratch_shapes=[pltpu.SemaphoreType.DMA(())],
        ),
        out_shape=jax.ShapeDtypeStruct(xs.shape, xs.dtype),
        input_output_aliases={3: 0},
        compiler_params=_params(("arbitrary",), row_dma=True),
        name="moe_dispatch",
    )(dest1, dest2, hw, xs)


def _experts_kernel(exp_ref, nv_ref, xs_ref, wg_ref, wu_ref, wd_ref, y_ref):
    del exp_ref
    used = pl.program_id(0) < nv_ref[0]

    @pl.when(used)
    def _():
        words = xs_ref[...]
        lo = lax.bitcast_convert_type(words << jnp.uint32(16), F32)
        hi = lax.bitcast_convert_type(words & jnp.uint32(HI_MASK), F32)
        hb = jnp.concatenate([lo, hi], axis=1).astype(BF16)
        _swiglu_into(hb, wg_ref, wu_ref, wd_ref, y_ref)

    @pl.when(jnp.logical_not(used))
    def _():
        y_ref[...] = jnp.zeros_like(y_ref)


def _experts_call(j, tile_exp, n_valid, xs, wg, wu, wd):
    n_tiles = tile_exp.shape[0]
    tm = EXPERT_TILE
    weight = lambda shape: pl.BlockSpec((None, 1) + shape, lambda i, ex, nv: (j, ex[i], 0, 0))
    return pl.pallas_call(
        _experts_kernel,
        grid_spec=pltpu.PrefetchScalarGridSpec(
            num_scalar_prefetch=2,
            grid=(n_tiles,),
            in_specs=[pl.BlockSpec((tm, HALF), lambda i, ex, nv: (i, 0)),
                      weight((D_MODEL, FFN_DIM)), weight((D_MODEL, FFN_DIM)), weight((FFN_DIM, D_MODEL))],
            out_specs=pl.BlockSpec((tm, D_MODEL), lambda i, ex, nv: (i, 0)),
        ),
        out_shape=jax.ShapeDtypeStruct((xs.shape[0], D_MODEL), F32),
        compiler_params=_params(("arbitrary",)),
        name="moe_experts",
    )(tile_exp, n_valid, xs, wg, wu, wd)


def _combine_kernel(d1_ref, d2_ref, y_ref, info_ref, x_ref, g2_ref, post_ref, o_ref, ybuf, sem):
    tm = x_ref.shape[1]
    base = (pl.program_id(0) * pl.num_programs(1) + pl.program_id(1)) * tm

    def issue(r, carry):
        _row_copy(y_ref, d1_ref[base + r], ybuf.at[0], r, sem).start(priority=0)
        _row_copy(y_ref, d2_ref[base + r], ybuf.at[1], r, sem).start(priority=1)
        return carry

    lax.fori_loop(0, tm, issue, 0, unroll=ISSUE_UNROLL)
    for k in range(2):
        pltpu.make_async_copy(y_ref.at[pl.ds(0, tm)], ybuf.at[k], sem).wait()

    info = info_ref[0]
    f = info[:, INFO_P1:INFO_P1 + 1] * ybuf[0] + info[:, INFO_P2:INFO_P2 + 1] * ybuf[1]
    o_ref[0] = x_ref[0] + g2_ref[0] * (_rms(f) * post_ref[...])


def _combine_call(dest1, dest2, y, info, x, mod, post_g, tm):
    nb, seq, _ = x.shape
    per_row = mod.shape[1] != 1
    return pl.pallas_call(
        _combine_kernel,
        grid_spec=pltpu.PrefetchScalarGridSpec(
            num_scalar_prefetch=2,
            grid=(nb, seq // tm),
            in_specs=[pl.BlockSpec(memory_space=pl.ANY),
                      pl.BlockSpec((1, tm, LANES), lambda b, i, d1, d2: (b, i, 0)),
                      pl.BlockSpec((1, tm, D_MODEL), lambda b, i, d1, d2: (b, i, 0)),
                      pl.BlockSpec((1, tm if per_row else 1, D_MODEL),
                                   lambda b, i, d1, d2: (b, i if per_row else 0, MOD_G2)),
                      pl.BlockSpec(post_g.shape, lambda b, i, d1, d2: (0, 0))],
            out_specs=pl.BlockSpec((1, tm, D_MODEL), lambda b, i, d1, d2: (b, i, 0)),
            scratch_shapes=[pltpu.VMEM((2, tm, D_MODEL), F32), pltpu.SemaphoreType.DMA(())],
        ),
        out_shape=jax.ShapeDtypeStruct(x.shape, F32),
        compiler_params=_params(("arbitrary", "arbitrary"), row_dma=True),
        name="moe_combine",
    )(dest1, dest2, y, info, x, mod, post_g)


def _moe_layer(j, xp, xs, mod_p, mod_s, pre_g, post_g, router_w, router_b, wg, wu, wd):
    rw = jnp.pad(router_w, ((0, 0), (0, LANES - N_EXPERTS)))
    rb = _pad_lanes(router_b)
    hw_p, info_p, cnt_p = _route_call(xp, mod_p, pre_g, rw, rb, jnp.zeros((1, LANES), F32), tm=512)
    hw_s, info_s, cnt = _route_call(xs, mod_s, pre_g, rw, rb, cnt_p, tm=xs.shape[1])

    tm = EXPERT_TILE
    n_assign = 2 * (xp.shape[0] * xp.shape[1] + xs.shape[1])
    n_tiles = n_assign // tm + N_EXPERTS
    counts = cnt[0, :N_EXPERTS].astype(jnp.int32)
    group_tiles = (counts + tm - 1) // tm
    tile_end = jnp.cumsum(group_tiles)
    start = (tile_end - group_tiles) * tm
    n_valid = tile_end[-1]
    tile_id = jnp.minimum(jnp.arange(n_tiles, dtype=jnp.int32), n_valid - 1)
    tile_exp = jnp.sum(tile_id[:, None] >= tile_end[None, :], axis=1).astype(jnp.int32)

    def dests(info):
        flat = info.reshape(-1, LANES)
        d = [start[flat[:, i].astype(jnp.int32)] + flat[:, r].astype(jnp.int32)
             for i, r in ((INFO_I1, INFO_R1), (INFO_I2, INFO_R2))]
        return d[0], d[1]

    dp = dests(info_p)
    ds = dests(info_s)
    slots = jnp.zeros((n_tiles * tm, HALF), jnp.uint32)
    slots = _dispatch_call(dp[0], dp[1], hw_p.reshape(-1, HALF), slots, tm=1024)
    slots = _dispatch_call(ds[0], ds[1], hw_s.reshape(-1, HALF), slots, tm=xs.shape[1])
    y = _experts_call(j, tile_exp, n_valid.reshape(1), slots, wg, wu, wd)
    xp = _combine_call(dp[0], dp[1], y, info_p, xp, mod_p, post_g, tm=512)
    xs = _combine_call(ds[0], ds[1], y, info_s, xs, mod_s, post_g, tm=xs.shape[1])
    return xp, xs


def _pad_lanes(v):
    return jnp.pad(v, (0, LANES - v.shape[0])).reshape(1, LANES)


def kernel(x_prompt, x_sample, state_ssm, state_conv, c_prompt, c_sample, w_mod, b_mod, mix_pre_g, mix_post_g, ffn_pre_g, ffn_post_g, w_in, conv_w, conv_b, dt_bias, a_log, d_skip, ssm_norm_g, w_ssd_out, cmlp_ln_g, cmlp_ln_b, w_spatial, b_spatial, w_cmlp_out, w_o, ffn_wg, ffn_wu, ffn_wd, router_w, router_b, exp_wg, exp_wu, exp_wd):
    n_prompt = x_prompt.shape[0]
    n_sample = x_sample.shape[0]

    c_all = jnp.concatenate([c_prompt, c_sample, jnp.zeros((8, D_MODEL), F32)], axis=0)
    mod_all = _mod_call(c_all, w_mod, b_mod)
    mod_p = mod_all[:, :n_prompt].reshape(DEPTH, n_prompt, 1, 6 * D_MODEL)
    mod_s = mod_all[:, n_prompt:n_prompt + n_sample].reshape(DEPTH, 1, n_sample, 6 * D_MODEL)

    xp = x_prompt
    xs = x_sample.reshape(1, n_sample, D_MODEL)
    conv_hist = state_conv.reshape(DEPTH, n_sample, (CONV_WIDTH - 1) * CONV_DIM)

    o_xbc = SSM_INNER
    o_dt = o_xbc + CONV_DIM
    o_u = o_dt + SSM_HEADS

    w_main = jnp.concatenate([w_in[:, :, :o_xbc], w_in[:, :, o_u:], w_in[:, :, o_xbc:o_dt]], axis=2).astype(BF16)
    w_dt = jnp.pad(w_in[:, :, o_dt:o_u], ((0, 0), (0, 0), (0, LANES - SSM_HEADS))).astype(BF16)
    wa, wb, wo = (a.astype(BF16) for a in (w_ssd_out, w_cmlp_out, w_o))
    dense_w = tuple(a.astype(BF16) for a in (ffn_wg, ffn_wu, ffn_wd))
    expert_w = tuple(a.astype(BF16) for a in (exp_wg, exp_wu, exp_wd))

    ssm_s = jnp.zeros(state_ssm.shape, F32)
    ssm_p, conv_p, conv_s, v_s = [], [], [], []
    for l in range(DEPTH):
        row = lambda a: a[l].reshape(1, -1)
        mixer_small = (conv_w[l], row(conv_b), _pad_lanes(dt_bias[l]), _pad_lanes(a_log[l]),
                       jnp.repeat(d_skip[l], SSM_HEAD_DIM).reshape(1, -1), row(ssm_norm_g),
                       row(cmlp_ln_g), row(cmlp_ln_b))

        proj_s, dt_s = _inproj_call(l, xs, mod_s[l], row(mix_pre_g), w_main, w_dt, tm=n_sample)
        xp, hs_p, cs_p = _mixer_call(xp, mod_p[l], row(mix_pre_g), row(mix_post_g), w_main[l], w_dt[l],
                                     wa[l], wb[l], wo[l], *mixer_small, w_spatial[l], b_spatial[l].T)
        ya_s, yb_s, vr_s, ssm_s, cs_s = _mixer_step_call(
            l, proj_s[0], dt_s[0], state_ssm, conv_hist[l], ssm_s, *mixer_small,
            jnp.repeat(w_spatial[l, :, 0, 0], CMLP_GROUP_DIM).reshape(1, -1),
            jnp.repeat(b_spatial[l, :, 0], CMLP_GROUP_DIM).reshape(1, -1))
        xs = _outproj_call(l, ya_s[None], yb_s[None], proj_s, xs, mod_s[l], row(mix_post_g), wa, wb, wo,
                           tm=n_sample)

        j = l // 2
        if l % 2 == 0:
            xp = _dense_ffn_call(j, xp, mod_p[l], row(ffn_pre_g), row(ffn_post_g), *dense_w, tm=512)
            xs = _dense_ffn_call(j, xs, mod_s[l], row(ffn_pre_g), row(ffn_post_g), *dense_w, tm=n_sample)
        else:
            xp, xs = _moe_layer(j, xp, xs, mod_p[l], mod_s[l], row(ffn_pre_g), row(ffn_post_g),
                                router_w[j], router_b[j], *expert_w)

        ssm_p.append(hs_p)
        conv_p.append(cs_p)
        conv_s.append(cs_s.reshape(n_sample, CONV_WIDTH - 1, CONV_DIM))
        v_s.append(vr_s.reshape(n_sample, 1, D_MODEL))

    return (xp, xs.reshape(n_sample, 1, D_MODEL), jnp.stack(ssm_p), jnp.stack(conv_p),
            ssm_s, jnp.stack(conv_s), jnp.stack(v_s))
```

```python
import functools

import jax
import jax.numpy as jnp
from jax import lax
from jax.experimental import pallas as pl
from jax.experimental.pallas import tpu as pltpu

F32 = jnp.float32
BF16 = jnp.bfloat16

D_MODEL = 1024
DEPTH = 4
SSM_HEADS = 16
SSM_HEAD_DIM = 64
SSM_GROUPS = 2
SSM_STATE = 128
SSM_INNER = 1024
GROUP_WIDTH = SSM_INNER // SSM_GROUPS
HEADS_PER_GROUP = SSM_HEADS // SSM_GROUPS
CONV_WIDTH = 4
CONV_DIM = 1536
CHUNK = 128
CMLP_GROUPS = 8
CMLP_GROUP_DIM = 128
FFN_DIM = 2816
N_EXPERTS = 8
EPS = 1e-6
LANES = 128
COL_Z, COL_U, COL_V, COL_GA, COL_GB, COL_XS = 0, 1, 2, 3, 4, 5
COL_BC_512 = 12
PROJ_MAIN = 6 * D_MODEL + 2 * SSM_GROUPS * SSM_STATE
PROJ_TILE = 1664
MOD_SH1, MOD_SC1, MOD_G1, MOD_SH2, MOD_SC2, MOD_G2 = range(6)
VMEM_LIMIT = 56 * 1024 * 1024


def _params(semantics, row_dma=False):
    return pltpu.CompilerParams(dimension_semantics=semantics, vmem_limit_bytes=VMEM_LIMIT,
                                disable_bounds_checks=row_dma)


def _rms(x):
    return x * lax.rsqrt(jnp.mean(x * x, axis=-1, keepdims=True) + EPS)


def _dot(a, b):
    return jnp.dot(a, b, preferred_element_type=F32)


def _dot_nt(a, b):
    return lax.dot_general(a, b, (((1,), (1,)), ((), ())), preferred_element_type=F32)


def _dot_tn(a, b):
    return lax.dot_general(a, b, (((0,), (0,)), ((), ())), preferred_element_type=F32)


def _mod_kernel(c_ref, w_ref, b_ref, o_ref):
    a = jax.nn.silu(c_ref[...]).astype(BF16)
    o_ref[0] = _dot(a, w_ref[0].astype(BF16)) + b_ref[0]


def _mod_call(c_all, w_mod, b_mod):
    rows = c_all.shape[0]
    return pl.pallas_call(
        _mod_kernel,
        grid=(DEPTH, 6),
        in_specs=[
            pl.BlockSpec((rows, D_MODEL), lambda l, j: (0, 0)),
            pl.BlockSpec((1, D_MODEL, D_MODEL), lambda l, j: (l, 0, j)),
            pl.BlockSpec((1, 1, D_MODEL), lambda l, j: (l, 0, j)),
        ],
        out_specs=pl.BlockSpec((1, rows, D_MODEL), lambda l, j: (l, 0, j)),
        out_shape=jax.ShapeDtypeStruct((DEPTH, rows, 6 * D_MODEL), F32),
        compiler_params=_params(("arbitrary", "arbitrary")),
        name="adaln_mod",
    )(c_all, w_mod, b_mod.reshape(DEPTH, 1, 6 * D_MODEL))


def _mod_spec(mod, tm, seg, grid_rank):
    per_row = mod.shape[1] != 1
    rows = tm if per_row else 1
    if grid_rank == 2:
        return pl.BlockSpec((1, rows, D_MODEL), lambda b, i: (b, i if per_row else 0, seg))
    return pl.BlockSpec((1, rows, D_MODEL), lambda b, i, j: (b, i if per_row else 0, seg))


def _inproj_kernel(x_ref, g_ref, sh_ref, sc_ref, w_ref, wdt_ref, p_ref, pdt_ref, h_scr):
    @pl.when(pl.program_id(2) == 0)
    def _():
        h = _rms(x_ref[0]) * g_ref[...] * (1.0 + sc_ref[0]) + sh_ref[0]
        hb = h.astype(BF16)
        h_scr[...] = hb
        pdt_ref[0] = _dot(hb, wdt_ref[...])

    p_ref[0] = _dot(h_scr[...], w_ref[...])


def _inproj_call(layer, x, mod, g, w_main, w_dt, tm):
    nb, seq, _ = x.shape
    grid = (nb, seq // tm, PROJ_MAIN // PROJ_TILE)
    return pl.pallas_call(
        _inproj_kernel,
        grid=grid,
        in_specs=[
            pl.BlockSpec((1, tm, D_MODEL), lambda b, i, j: (b, i, 0)),
            pl.BlockSpec((1, D_MODEL), lambda b, i, j: (0, 0)),
            _mod_spec(mod, tm, MOD_SH1, 3),
            _mod_spec(mod, tm, MOD_SC1, 3),
            pl.BlockSpec((None, D_MODEL, PROJ_TILE), lambda b, i, j: (layer, 0, j)),
            pl.BlockSpec((None, D_MODEL, LANES), lambda b, i, j: (layer, 0, 0)),
        ],
        out_specs=[
            pl.BlockSpec((1, tm, PROJ_TILE), lambda b, i, j: (b, i, j)),
            pl.BlockSpec((1, tm, LANES), lambda b, i, j: (b, i, 0)),
        ],
        out_shape=[
            jax.ShapeDtypeStruct((nb, seq, PROJ_MAIN), F32),
            jax.ShapeDtypeStruct((nb, seq, LANES), F32),
        ],
        scratch_shapes=[pltpu.VMEM((tm, D_MODEL), BF16)],
        compiler_params=_params(("arbitrary", "arbitrary", "arbitrary")),
        name="in_proj",
    )(x, g, mod, mod, w_main, w_dt)


def _gated_group_norm(y, z, norm_g):
    y = y * jax.nn.silu(z)
    parts = [_rms(y[:, g * GROUP_WIDTH:(g + 1) * GROUP_WIDTH]) for g in range(SSM_GROUPS)]
    return jnp.concatenate(parts, axis=-1) * norm_g


def _layernorm(x, g, b):
    mu = jnp.mean(x, axis=-1, keepdims=True)
    xc = x - mu
    return xc * lax.rsqrt(jnp.mean(xc * xc, axis=-1, keepdims=True) + EPS) * g + b


def _pair_columns(v, pair, lane_lo):
    h0 = 2 * pair
    return jnp.where(lane_lo, v[:, h0:h0 + 1], v[:, h0 + 1:h0 + 2])


MIX_CHUNKS = 4
MIX_GROUP = 2
P0_COLS = {"z": (COL_Z * D_MODEL, D_MODEL), "u": (COL_U * D_MODEL, D_MODEL), "v": (COL_V * D_MODEL, D_MODEL),
           "ga": (COL_GA * D_MODEL, D_MODEL), "gb": (COL_GB * D_MODEL, D_MODEL),
           "xs": (COL_XS * D_MODEL, D_MODEL), "bc": (COL_BC_512 * 512, 2 * SSM_GROUPS * SSM_STATE),
           "dt": (PROJ_MAIN, LANES)}


def _mixer_chunk(z, u, v, xs, bc, dt_raw,
                 cw_ref, cb_ref, dtb_ref, alog_ref, dskip_ref, ng_ref, lng_ref, lnb_ref, bsp_ref,
                 h_scr, cbuf, wsp_scr, causal, lane_lo, fill):
    cbuf[8:8 + CHUNK, 0:SSM_INNER] = xs
    cbuf[8:8 + CHUNK, SSM_INNER:CONV_DIM] = bc
    acc = cbuf[5:5 + CHUNK, :] * cw_ref[0:1, :]
    for k in range(1, CONV_WIDTH):
        acc = acc + cbuf[5 + k:5 + k + CHUNK, :] * cw_ref[k:k + 1, :]
    cbuf[5:8, :] = cbuf[5 + CHUNK:8 + CHUNK, :]
    fill()

    xbc = jax.nn.silu(acc + cb_ref[...])
    xc = xbc[:, 0:SSM_INNER]
    bm = xbc[:, SSM_INNER:SSM_INNER + SSM_GROUPS * SSM_STATE].astype(BF16)
    cm = xbc[:, SSM_INNER + SSM_GROUPS * SSM_STATE:CONV_DIM].astype(BF16)
    fill()

    dt = jax.nn.softplus(dt_raw + dtb_ref[...])
    da = dt * (-jnp.exp(alog_ref[...]))
    cum = jnp.dot(causal.astype(F32), da, preferred_element_type=F32,
                  precision=lax.Precision.HIGHEST)
    last = cum[CHUNK - 1:CHUNK, :]
    ecum = jnp.exp(cum)
    wend = jnp.exp(last - cum) * dt
    elast = jnp.exp(last)
    cum_t = cum.T
    dt_t = dt.T
    fill()

    y_pairs = []
    for g in range(SSM_GROUPS):
        bg = bm[:, g * SSM_STATE:(g + 1) * SSM_STATE]
        cg = cm[:, g * SSM_STATE:(g + 1) * SSM_STATE]
        cb = _dot_nt(cg, bg)
        hg = h_scr[g * GROUP_WIDTH:(g + 1) * GROUP_WIDTH, :]
        y_state = _dot_nt(cg, hg.astype(BF16))
        xw_parts = []
        for q in range(HEADS_PER_GROUP // 2):
            pair = g * (HEADS_PER_GROUP // 2) + q
            mixes = []
            for h in (2 * pair, 2 * pair + 1):
                seg = cum[:, h:h + 1] - cum_t[h:h + 1, :]
                decay = jnp.where(causal, jnp.exp(seg), 0.0)
                mixes.append((cb * decay * dt_t[h:h + 1, :]).astype(BF16))
            xp = xc[:, pair * LANES:(pair + 1) * LANES]
            rhs = jnp.concatenate([jnp.where(lane_lo, xp, 0.0), jnp.where(lane_lo, 0.0, xp)],
                                  axis=0).astype(BF16)
            y_in = _dot(jnp.concatenate(mixes, axis=1), rhs)
            y_st = y_state[:, q * LANES:(q + 1) * LANES] * _pair_columns(ecum, pair, lane_lo)
            y_pairs.append(y_in + y_st)
            xw_parts.append((xp * _pair_columns(wend, pair, lane_lo)).astype(BF16))
            fill()
        upd = _dot_tn(jnp.concatenate(xw_parts, axis=1), bg)
        for r in range(HEADS_PER_GROUP):
            h = g * HEADS_PER_GROUP + r
            head = slice(h * SSM_HEAD_DIM, (h + 1) * SSM_HEAD_DIM)
            scale = jnp.broadcast_to(elast[0:1, h:h + 1], (SSM_HEAD_DIM, SSM_STATE))
            h_scr[head, :] = h_scr[head, :] * scale + upd[r * SSM_HEAD_DIM:(r + 1) * SSM_HEAD_DIM, :]

    y = jnp.concatenate(y_pairs, axis=1) + dskip_ref[...] * xc
    ya = _gated_group_norm(y, z, ng_ref[...]).astype(BF16)
    fill()

    ug = jax.nn.gelu(u, approximate=True)
    vn = _layernorm(jax.nn.gelu(v, approximate=True), lng_ref[...], lnb_ref[...])
    fill()
    gates = []
    for g in range(CMLP_GROUPS):
        vg = vn[:, g * CMLP_GROUP_DIM:(g + 1) * CMLP_GROUP_DIM].astype(BF16)
        gates.append(_dot(wsp_scr[g], vg) + bsp_ref[:, g:g + 1])
    yb = (ug * jnp.concatenate(gates, axis=1)).astype(BF16)
    return ya, yb


def _merge_project(ya, yb, ga, gb, x, g1, post_g, wa_ref, wb_ref, wo_ref):
    merged = (jax.nn.sigmoid(ga) * _dot(ya, wa_ref[...]) + jax.nn.sigmoid(gb) * _dot(yb, wb_ref[...]))
    o = _dot(merged.astype(BF16), wo_ref[...])
    return x + g1 * (_rms(o) * post_g)


def _mixer_kernel(x_ref, pre_ref, sh_ref, sc_ref, g1_ref, xn_ref, shn_ref, scn_ref, win_ref, wdt_ref,
                  cw_ref, cb_ref, dtb_ref, alog_ref, dskip_ref, ng_ref, lng_ref, lnb_ref, wsp_ref, bsp_ref,
                  pg_ref, wa_ref, wb_ref, wo_ref,
                  o_ref, ssm_ref, conv_ref,
                  h_scr, cbuf, wsp_scr, p0_scr):
    c = pl.program_id(1)
    row = lax.broadcasted_iota(jnp.int32, (CHUNK, CHUNK), 0)
    col = lax.broadcasted_iota(jnp.int32, (CHUNK, CHUNK), 1)
    causal = row >= col
    lane_lo = col < SSM_HEAD_DIM

    @pl.when(c == 0)
    def _():
        h_scr[...] = jnp.zeros_like(h_scr)
        cbuf[0:8, :] = jnp.zeros((8, CONV_DIM), F32)
        for g in range(CMLP_GROUPS):
            wsp_scr[g] = jnp.where(causal, wsp_ref[g], 0.0).astype(BF16)

    def project(hb, block, width=D_MODEL):
        return _dot(hb, win_ref[:, block * D_MODEL:block * D_MODEL + width])

    def projection_pieces(load_x, load_sh, load_sc, store):
        t = {}

        def prep():
            x = load_x()
            store("x", x)
            t["hb"] = (_rms(x) * pre_ref[...] * (1.0 + load_sc()) + load_sh()).astype(BF16)

        pieces = [prep]
        for name, block in (("z", COL_Z), ("u", COL_U), ("v", COL_V), ("xs", COL_XS), ("bc", COL_XS + 1),
                            ("ga", COL_GA), ("gb", COL_GB)):
            width = P0_COLS[name][1]
            pieces.append(lambda name=name, block=block, width=width:
                          store(name, project(t["hb"], block, width)))
        pieces.append(lambda: store("dt", _dot(t["hb"], wdt_ref[...])))
        return pieces

    group_rows = MIX_GROUP * CHUNK

    def store_first(name, value):
        if name != "x":
            start, width = P0_COLS[name]
            p0_scr[:, start:start + width] = value

    def out_projection(rows, p, ya, yb):
        o_ref[0, rows, :] = _merge_project(ya, yb, p["ga"], p["gb"], x_ref[0, rows, :], g1_ref[0], pg_ref[...],
                                           wa_ref, wb_ref, wo_ref)

    @pl.when(jnp.logical_and(pl.program_id(0) == 0, c == 0))
    def _():
        for piece in projection_pieces(lambda: x_ref[0, 0:group_rows, :], lambda: sh_ref[0], lambda: sc_ref[0],
                                       store_first):
            piece()

    class _FromScratch:
        def __getitem__(self, name):
            start, width = P0_COLS[name]
            return p0_scr.at[:, start:start + width]

    cur = _FromScratch()
    pending = []
    n_groups = MIX_CHUNKS // MIX_GROUP
    for j in range(n_groups):
        rows = slice(j * group_rows, (j + 1) * group_rows)
        if j + 1 < n_groups:
            nxt = {}
            nrows = slice((j + 1) * group_rows, (j + 2) * group_rows)
            more = projection_pieces(lambda nrows=nrows: x_ref[0, nrows, :], lambda: sh_ref[0],
                                     lambda: sc_ref[0], nxt.__setitem__)
        else:
            more = projection_pieces(lambda: xn_ref[0], lambda: shn_ref[0], lambda: scn_ref[0], store_first)
        pending = pending + more

        def fill():
            if pending:
                pending.pop(0)()

        yas, ybs = [], []
        for i in range(MIX_GROUP):
            sub = slice(i * CHUNK, (i + 1) * CHUNK)
            ya, yb = _mixer_chunk(*(cur[name][sub, :] for name in ("z", "u", "v", "xs", "bc", "dt")),
                                  cw_ref, cb_ref, dtb_ref, alog_ref, dskip_ref, ng_ref, lng_ref, lnb_ref,
                                  bsp_ref, h_scr, cbuf, wsp_scr, causal, lane_lo, fill)
            yas.append(ya)
            ybs.append(yb)
        while pending:
            fill()
        gates = {name: cur[name][...] for name in ("ga", "gb")}
        pending = [functools.partial(out_projection, rows, gates, jnp.concatenate(yas, axis=0),
                                     jnp.concatenate(ybs, axis=0))]
        if j + 1 < n_groups:
            cur = nxt
    pending.pop(0)()

    @pl.when(c == pl.num_programs(1) - 1)
    def _():
        conv_ref[0] = cbuf[5:8, :]
        ssm_ref[0] = h_scr[...].reshape(SSM_HEADS, SSM_HEAD_DIM, SSM_STATE)


def _mixer_call(x, mod, pre_g, post_g, w_main, w_dt, wa, wb, wo,
                conv_w, conv_b, dt_bias, a_log, d_skip, norm_g, ln_g, ln_b, w_sp, b_sp_t):
    nb, seq, _ = x.shape
    tm = MIX_CHUNKS * CHUNK

    def whole(a):
        zeros = (0,) * a.ndim
        return pl.BlockSpec(a.shape, lambda b, c: zeros)

    nsteps = seq // tm

    def following(b, c):
        lin = jnp.minimum(b * nsteps + c + 1, nb * nsteps - 1)
        return lin // nsteps, lin % nsteps

    def next_x(b, c):
        bn, cn = following(b, c)
        return bn, cn * (MIX_CHUNKS // MIX_GROUP), 0

    def next_mod(seg):
        return pl.BlockSpec((1, 1, D_MODEL), lambda b, c: (following(b, c)[0], 0, seg))

    small = (conv_w, conv_b, dt_bias, a_log, d_skip, norm_g, ln_g, ln_b, w_sp, b_sp_t)
    return pl.pallas_call(
        _mixer_kernel,
        grid=(nb, nsteps),
        in_specs=[pl.BlockSpec((1, tm, D_MODEL), lambda b, c: (b, c, 0)), whole(pre_g),
                  _mod_spec(mod, tm, MOD_SH1, 2), _mod_spec(mod, tm, MOD_SC1, 2), _mod_spec(mod, tm, MOD_G1, 2),
                  pl.BlockSpec((1, MIX_GROUP * CHUNK, D_MODEL), next_x), next_mod(MOD_SH1), next_mod(MOD_SC1),
                  whole(w_main), whole(w_dt)]
                 + [whole(a) for a in small]
                 + [whole(post_g), whole(wa), whole(wb), whole(wo)],
        out_specs=[
            pl.BlockSpec((1, tm, D_MODEL), lambda b, c: (b, c, 0)),
            pl.BlockSpec((1, SSM_HEADS, SSM_HEAD_DIM, SSM_STATE), lambda b, c: (b, 0, 0, 0)),
            pl.BlockSpec((1, CONV_WIDTH - 1, CONV_DIM), lambda b, c: (b, 0, 0)),
        ],
        out_shape=[
            jax.ShapeDtypeStruct(x.shape, F32),
            jax.ShapeDtypeStruct((nb, SSM_HEADS, SSM_HEAD_DIM, SSM_STATE), F32),
            jax.ShapeDtypeStruct((nb, CONV_WIDTH - 1, CONV_DIM), F32),
        ],
        scratch_shapes=[
            pltpu.VMEM((SSM_INNER, SSM_STATE), F32),
            pltpu.VMEM((CHUNK + 8, CONV_DIM), F32),
            pltpu.VMEM((CMLP_GROUPS, CHUNK, CHUNK), BF16),
            pltpu.VMEM((MIX_GROUP * CHUNK, PROJ_MAIN + LANES), F32),
        ],
        compiler_params=_params(("arbitrary", "arbitrary")),
        name="mixer_prompt",
    )(x, pre_g, mod, mod, mod, x, mod, mod, w_main, w_dt, *small, post_g, wa, wb, wo)


SAMPLE_TILE = 16


def _mixer_step_kernel(z_ref, u_ref, v_ref, xs_ref, bc_ref, dt_ref, ssm_in_ref, conv_in_ref,
                       cw_ref, cb_ref, dtb_ref, alog_ref, dskip_ref, ng_ref,
                       lng_ref, lnb_ref, wsp0_ref, bsp0_ref, ssm_all_ref,
                       ya_ref, yb_ref, vout_ref, ssm_ref, conv_ref):
    del ssm_all_ref
    tb = SAMPLE_TILE
    xbc_new = jnp.concatenate([xs_ref[...], bc_ref[...]], axis=1)
    hist = conv_in_ref[...]
    acc = xbc_new * cw_ref[CONV_WIDTH - 1:CONV_WIDTH, :]
    for k in range(CONV_WIDTH - 1):
        acc = acc + hist[:, k * CONV_DIM:(k + 1) * CONV_DIM] * cw_ref[k:k + 1, :]
    conv_ref[:, 0:(CONV_WIDTH - 2) * CONV_DIM] = hist[:, CONV_DIM:]
    conv_ref[:, (CONV_WIDTH - 2) * CONV_DIM:] = xbc_new

    xbc = jax.nn.silu(acc + cb_ref[...])
    xc = xbc[:, 0:SSM_INNER]
    bm = xbc[:, SSM_INNER:SSM_INNER + SSM_GROUPS * SSM_STATE]
    cm = xbc[:, SSM_INNER + SSM_GROUPS * SSM_STATE:CONV_DIM]
    dt = jax.nn.softplus(dt_ref[...] + dtb_ref[...])
    dec = jnp.exp(dt * (-jnp.exp(alog_ref[...])))

    def transposed(a):
        pad = jnp.zeros((LANES - tb, a.shape[1]), F32)
        return jnp.concatenate([a, pad], axis=0).T

    xc_t = transposed(xc)
    dt_t = transposed(dt)
    dec_t = transposed(dec)
    row_id = lax.broadcasted_iota(jnp.int32, (tb, SSM_STATE), 0)

    y_groups = [jnp.zeros((tb, GROUP_WIDTH), F32) for _ in range(SSM_GROUPS)]
    for b in range(tb):
        for g in range(SSM_GROUPS):
            b_row = bm[b:b + 1, g * SSM_STATE:(g + 1) * SSM_STATE]
            c_only = jnp.where(row_id == b, cm[:, g * SSM_STATE:(g + 1) * SSM_STATE], 0.0).astype(BF16)
            new_heads = []
            for r in range(HEADS_PER_GROUP):
                h = g * HEADS_PER_GROUP + r
                x_col = xc_t[h * SSM_HEAD_DIM:(h + 1) * SSM_HEAD_DIM, b:b + 1]
                push = x_col * dt_t[h:h + 1, b:b + 1]
                keep = jnp.broadcast_to(dec_t[h:h + 1, b:b + 1], (SSM_HEAD_DIM, SSM_STATE))
                h_new = ssm_in_ref[b, h] * keep + push * b_row
                ssm_ref[b, h] = h_new
                new_heads.append(h_new.astype(BF16))
            hg = jnp.concatenate(new_heads, axis=0)
            y_groups[g] = y_groups[g] + _dot_nt(c_only, hg)

    y = jnp.concatenate(y_groups, axis=1) + dskip_ref[...] * xc
    ya_ref[...] = _gated_group_norm(y, z_ref[...], ng_ref[...]).astype(BF16)

    ug = jax.nn.gelu(u_ref[...], approximate=True)
    vn = _layernorm(jax.nn.gelu(v_ref[...], approximate=True), lng_ref[...], lnb_ref[...])
    vout_ref[...] = vn
    yb_ref[...] = (ug * (vn * wsp0_ref[...] + bsp0_ref[...])).astype(BF16)


def _mixer_step_call(layer, proj, proj_dt, state_ssm, conv_hist, ssm_all, conv_w, conv_b, dt_bias, a_log,
                     d_skip, norm_g, ln_g, ln_b, w_sp0, b_sp0):
    nseq = proj.shape[0]
    tb = SAMPLE_TILE

    def col(block, width=D_MODEL):
        return pl.BlockSpec((tb, width), lambda i: (i, block))

    def whole(a):
        zeros = (0,) * a.ndim
        return pl.BlockSpec(a.shape, lambda i: zeros)

    small = (conv_w, conv_b, dt_bias, a_log, d_skip, norm_g, ln_g, ln_b, w_sp0, b_sp0)
    hist_width = (CONV_WIDTH - 1) * CONV_DIM
    return pl.pallas_call(
        _mixer_step_kernel,
        grid=(nseq // tb,),
        in_specs=[col(COL_Z), col(COL_U), col(COL_V), col(COL_XS),
                  col(COL_BC_512, 2 * SSM_GROUPS * SSM_STATE),
                  pl.BlockSpec((tb, LANES), lambda i: (i, 0)),
                  pl.BlockSpec((None, tb, SSM_HEADS, SSM_HEAD_DIM, SSM_STATE),
                               lambda i: (layer, i, 0, 0, 0)),
                  pl.BlockSpec((tb, hist_width), lambda i: (i, 0))]
                 + [whole(a) for a in small] + [pl.BlockSpec(memory_space=pl.ANY)],
        out_specs=[
            pl.BlockSpec((tb, SSM_INNER), lambda i: (i, 0)),
            pl.BlockSpec((tb, D_MODEL), lambda i: (i, 0)),
            pl.BlockSpec((tb, D_MODEL), lambda i: (i, 0)),
            pl.BlockSpec((None, tb, SSM_HEADS, SSM_HEAD_DIM, SSM_STATE), lambda i: (layer, i, 0, 0, 0)),
            pl.BlockSpec((tb, hist_width), lambda i: (i, 0)),
        ],
        out_shape=[
            jax.ShapeDtypeStruct((nseq, SSM_INNER), BF16),
            jax.ShapeDtypeStruct((nseq, D_MODEL), BF16),
            jax.ShapeDtypeStruct((nseq, D_MODEL), F32),
            jax.ShapeDtypeStruct(ssm_all.shape, F32),
            jax.ShapeDtypeStruct((nseq, hist_width), F32),
        ],
        input_output_aliases={8 + len(small): 3},
        compiler_params=_params(("arbitrary",)),
        name="mixer_sample",
    )(proj, proj, proj, proj, proj, proj_dt, state_ssm, conv_hist, *small, ssm_all)


def _outproj_kernel(ya_ref, yb_ref, ga_ref, gb_ref, x_ref, g1_ref, pg_ref, wa_ref, wb_ref, wo_ref, o_ref):
    o_ref[0] = _merge_project(ya_ref[0], yb_ref[0], ga_ref[0], gb_ref[0], x_ref[0], g1_ref[0], pg_ref[...],
                              wa_ref, wb_ref, wo_ref)


def _outproj_call(layer, ya, yb, proj, x, mod, post_g, wa, wb, wo, tm):
    nb, seq, _ = x.shape

    def rows(block=0):
        return pl.BlockSpec((1, tm, D_MODEL), lambda b, i: (b, i, block))

    def whole(a):
        return pl.BlockSpec(a.shape, lambda b, i: (0, 0))

    def of_layer(a):
        return pl.BlockSpec((None,) + a.shape[1:], lambda b, i: (layer, 0, 0))

    return pl.pallas_call(
        _outproj_kernel,
        grid=(nb, seq // tm),
        in_specs=[rows(), rows(), rows(COL_GA), rows(COL_GB), rows(),
                  _mod_spec(mod, tm, MOD_G1, 2), whole(post_g), of_layer(wa), of_layer(wb), of_layer(wo)],
        out_specs=rows(),
        out_shape=jax.ShapeDtypeStruct(x.shape, F32),
        compiler_params=_params(("arbitrary", "arbitrary")),
        name="out_proj",
    )(ya, yb, proj, proj, x, mod, post_g, wa, wb, wo)


FFN_CHUNK = 256


def _swiglu_into(hb, wg_ref, wu_ref, wd_ref, acc_ref):
    for f in range(FFN_DIM // FFN_CHUNK):
        cols = slice(f * FFN_CHUNK, (f + 1) * FFN_CHUNK)
        act = jax.nn.silu(_dot(hb, wg_ref[0, :, cols])) * _dot(hb, wu_ref[0, :, cols])
        part = _dot(act.astype(BF16), wd_ref[0, cols, :])
        if f == 0:
            acc_ref[...] = part
        else:
            acc_ref[...] += part


def _premod(x_ref, pre_ref, sh_ref, sc_ref):
    return _rms(x_ref[0]) * pre_ref[...] * (1.0 + sc_ref[0]) + sh_ref[0]


def _dense_ffn_kernel(x_ref, pre_ref, post_ref, sh_ref, sc_ref, g2_ref, wg_ref, wu_ref, wd_ref,
                      o_ref, acc_scr):
    hb = _premod(x_ref, pre_ref, sh_ref, sc_ref).astype(BF16)
    _swiglu_into(hb, wg_ref, wu_ref, wd_ref, acc_scr)
    o_ref[0] = x_ref[0] + g2_ref[0] * (_rms(acc_scr[...]) * post_ref[...])


def _dense_ffn_call(j, x, mod, pre_g, post_g, wg, wu, wd, tm):
    nb, seq, _ = x.shape

    def whole(a):
        zeros = (0,) * a.ndim
        return pl.BlockSpec(a.shape, lambda b, i: zeros)

    def of_layer(a):
        return pl.BlockSpec((1,) + a.shape[1:], lambda b, i: (j, 0, 0))

    rows = pl.BlockSpec((1, tm, D_MODEL), lambda b, i: (b, i, 0))
    return pl.pallas_call(
        _dense_ffn_kernel,
        grid=(nb, seq // tm),
        in_specs=[rows, whole(pre_g), whole(post_g),
                  _mod_spec(mod, tm, MOD_SH2, 2), _mod_spec(mod, tm, MOD_SC2, 2),
                  _mod_spec(mod, tm, MOD_G2, 2), of_layer(wg), of_layer(wu), of_layer(wd)],
        out_specs=rows,
        out_shape=jax.ShapeDtypeStruct(x.shape, F32),
        scratch_shapes=[pltpu.VMEM((tm, D_MODEL), F32)],
        compiler_params=_params(("arbitrary", "arbitrary")),
        name="dense_ffn",
    )(x, pre_g, post_g, mod, mod, mod, wg, wu, wd)


EXPERT_TILE = 512
HALF = D_MODEL // 2
INFO_I1, INFO_I2, INFO_P1, INFO_P2, INFO_R1, INFO_R2 = range(6)
HI_MASK = 0xFFFF0000
ISSUE_UNROLL = 8


def _route_kernel(x_ref, pre_ref, sh_ref, sc_ref, rw_ref, rb_ref, cnt0_ref,
                  hw_ref, info_ref, cnt_ref, carry):
    @pl.when(jnp.logical_and(pl.program_id(0) == 0, pl.program_id(1) == 0))
    def _():
        carry[...] = cnt0_ref[...]

    hb = _premod(x_ref, pre_ref, sh_ref, sc_ref).astype(BF16)
    tm = hb.shape[0]
    bits = lax.bitcast_convert_type(hb.astype(F32), jnp.uint32)
    hw_ref[0] = (bits[:, HALF:] & jnp.uint32(HI_MASK)) | (bits[:, :HALF] >> jnp.uint32(16))

    logits = _dot(hb, rw_ref[...].astype(BF16)) + rb_ref[...]
    lane = lax.broadcasted_iota(jnp.int32, logits.shape, 1).astype(F32)
    neg = jnp.float32(-jnp.inf)
    logits = jnp.where(lane < N_EXPERTS, logits, neg)
    m1 = jnp.max(logits, axis=-1, keepdims=True)
    i1 = jnp.min(jnp.where(logits == m1, lane, float(LANES)), axis=-1, keepdims=True)
    rest = jnp.where(lane == i1, neg, logits)
    m2 = jnp.max(rest, axis=-1, keepdims=True)
    i2 = jnp.min(jnp.where(rest == m2, lane, float(LANES)), axis=-1, keepdims=True)
    e2 = jnp.exp(m2 - m1)
    p1 = 1.0 / (1.0 + e2)
    p2 = e2 * p1

    member = jnp.logical_or(lane == i1, lane == i2)
    row = lax.broadcasted_iota(jnp.int32, (tm, tm), 0)
    col = lax.broadcasted_iota(jnp.int32, (tm, tm), 1)
    before = jnp.where(row > col, 1.0, 0.0).astype(BF16)
    ones = jnp.where(member, 1.0, 0.0)
    prior = _dot(before, ones.astype(BF16)) + carry[...]
    r1 = jnp.sum(jnp.where(lane == i1, prior, 0.0), axis=-1, keepdims=True)
    r2 = jnp.sum(jnp.where(lane == i2, prior, 0.0), axis=-1, keepdims=True)
    carry[...] += jnp.sum(ones, axis=0, keepdims=True)
    cnt_ref[...] = carry[...]

    info = jnp.zeros_like(logits)
    for k, v in ((INFO_I1, i1), (INFO_I2, i2), (INFO_P1, p1), (INFO_P2, p2), (INFO_R1, r1), (INFO_R2, r2)):
        info = jnp.where(lane == float(k), v, info)
    info_ref[0] = info


def _route_call(x, mod, pre_g, rw, rb, cnt0, tm):
    nb, seq, _ = x.shape

    def whole(a):
        return pl.BlockSpec(a.shape, lambda b, i: (0, 0))

    return pl.pallas_call(
        _route_kernel,
        grid=(nb, seq // tm),
        in_specs=[pl.BlockSpec((1, tm, D_MODEL), lambda b, i: (b, i, 0)), whole(pre_g),
                  _mod_spec(mod, tm, MOD_SH2, 2), _mod_spec(mod, tm, MOD_SC2, 2),
                  whole(rw), whole(rb), whole(cnt0)],
        out_specs=[pl.BlockSpec((1, tm, HALF), lambda b, i: (b, i, 0)),
                   pl.BlockSpec((1, tm, LANES), lambda b, i: (b, i, 0)),
                   pl.BlockSpec((1, LANES), lambda b, i: (0, 0))],
        out_shape=[jax.ShapeDtypeStruct((nb, seq, HALF), jnp.uint32),
                   jax.ShapeDtypeStruct((nb, seq, LANES), F32),
                   jax.ShapeDtypeStruct((1, LANES), F32)],
        scratch_shapes=[pltpu.VMEM((1, LANES), F32)],
        compiler_params=_params(("arbitrary", "arbitrary")),
        name="moe_route",
    )(x, pre_g, mod, mod, rw, rb, cnt0)


def _row_copy(src, src_row, dst, dst_row, sem):
    return pltpu.make_async_copy(src.at[pl.ds(src_row, 1)], dst.at[pl.ds(dst_row, 1)], sem)


def _dispatch_kernel(d1_ref, d2_ref, hw_ref, xs_in_ref, xs_ref, sem):
    del xs_in_ref
    tm = hw_ref.shape[0]
    base = pl.program_id(0) * tm

    def issue(r, carry):
        _row_copy(hw_ref, r, xs_ref, d1_ref[base + r], sem).start(priority=0)
        _row_copy(hw_ref, r, xs_ref, d2_ref[base + r], sem).start(priority=1)
        return carry

    lax.fori_loop(0, tm, issue, 0, unroll=ISSUE_UNROLL)
    for _ in range(2):
        pltpu.make_async_copy(hw_ref, xs_ref.at[pl.ds(0, tm)], sem).wait()


def _dispatch_call(dest1, dest2, hw, xs, tm):
    rows = hw.shape[0]
    return pl.pallas_call(
        _dispatch_kernel,
        grid_spec=pltpu.PrefetchScalarGridSpec(
            num_scalar_prefetch=2,
            grid=(rows // tm,),
            in_specs=[pl.BlockSpec((tm, HALF), lambda i, d1, d2: (i, 0)),
                      pl.BlockSpec(memory_space=pl.ANY)],
            out_specs=pl.BlockSpec(memory_space=pl.ANY),
            scratch_shapes=[pltpu.SemaphoreType.DMA(())],
        ),
        out_shape=jax.ShapeDtypeStruct(xs.shape, xs.dtype),
        input_output_aliases={3: 0},
        compiler_params=_params(("arbitrary",), row_dma=True),
        name="moe_dispatch",
    )(dest1, dest2, hw, xs)


def _experts_kernel(exp_ref, nv_ref, xs_ref, wg_ref, wu_ref, wd_ref, y_ref):
    del exp_ref
    used = pl.program_id(0) < nv_ref[0]

    @pl.when(used)
    def _():
        words = xs_ref[...]
        lo = lax.bitcast_convert_type(words << jnp.uint32(16), F32)
        hi = lax.bitcast_convert_type(words & jnp.uint32(HI_MASK), F32)
        hb = jnp.concatenate([lo, hi], axis=1).astype(BF16)
        _swiglu_into(hb, wg_ref, wu_ref, wd_ref, y_ref)

    @pl.when(jnp.logical_not(used))
    def _():
        y_ref[...] = jnp.zeros_like(y_ref)


def _experts_call(j, tile_exp, n_valid, xs, wg, wu, wd):
    n_tiles = tile_exp.shape[0]
    tm = EXPERT_TILE
    weight = lambda shape: pl.BlockSpec((None, 1) + shape, lambda i, ex, nv: (j, ex[i], 0, 0))
    return pl.pallas_call(
        _experts_kernel,
        grid_spec=pltpu.PrefetchScalarGridSpec(
            num_scalar_prefetch=2,
            grid=(n_tiles,),
            in_specs=[pl.BlockSpec((tm, HALF), lambda i, ex, nv: (i, 0)),
                      weight((D_MODEL, FFN_DIM)), weight((D_MODEL, FFN_DIM)), weight((FFN_DIM, D_MODEL))],
            out_specs=pl.BlockSpec((tm, D_MODEL), lambda i, ex, nv: (i, 0)),
        ),
        out_shape=jax.ShapeDtypeStruct((xs.shape[0], D_MODEL), F32),
        compiler_params=_params(("arbitrary",)),
        name="moe_experts",
    )(tile_exp, n_valid, xs, wg, wu, wd)


def _combine_kernel(d1_ref, d2_ref, y_ref, info_ref, x_ref, g2_ref, post_ref, o_ref, ybuf, sem):
    tm = x_ref.shape[1]
    base = (pl.program_id(0) * pl.num_programs(1) + pl.program_id(1)) * tm

    def issue(r, carry):
        _row_copy(y_ref, d1_ref[base + r], ybuf.at[0], r, sem).start(priority=0)
        _row_copy(y_ref, d2_ref[base + r], ybuf.at[1], r, sem).start(priority=1)
        return carry

    lax.fori_loop(0, tm, issue, 0, unroll=ISSUE_UNROLL)
    for k in range(2):
        pltpu.make_async_copy(y_ref.at[pl.ds(0, tm)], ybuf.at[k], sem).wait()

    info = info_ref[0]
    f = info[:, INFO_P1:INFO_P1 + 1] * ybuf[0] + info[:, INFO_P2:INFO_P2 + 1] * ybuf[1]
    o_ref[0] = x_ref[0] + g2_ref[0] * (_rms(f) * post_ref[...])


def _combine_call(dest1, dest2, y, info, x, mod, post_g, tm):
    nb, seq, _ = x.shape
    per_row = mod.shape[1] != 1
    return pl.pallas_call(
        _combine_kernel,
        grid_spec=pltpu.PrefetchScalarGridSpec(
            num_scalar_prefetch=2,
            grid=(nb, seq // tm),
            in_specs=[pl.BlockSpec(memory_space=pl.ANY),
                      pl.BlockSpec((1, tm, LANES), lambda b, i, d1, d2: (b, i, 0)),
                      pl.BlockSpec((1, tm, D_MODEL), lambda b, i, d1, d2: (b, i, 0)),
                      pl.BlockSpec((1, tm if per_row else 1, D_MODEL),
                                   lambda b, i, d1, d2: (b, i if per_row else 0, MOD_G2)),
                      pl.BlockSpec(post_g.shape, lambda b, i, d1, d2: (0, 0))],
            out_specs=pl.BlockSpec((1, tm, D_MODEL), lambda b, i, d1, d2: (b, i, 0)),
            scratch_shapes=[pltpu.VMEM((2, tm, D_MODEL), F32), pltpu.SemaphoreType.DMA(())],
        ),
        out_shape=jax.ShapeDtypeStruct(x.shape, F32),
        compiler_params=_params(("arbitrary", "arbitrary"), row_dma=True),
        name="moe_combine",
    )(dest1, dest2, y, info, x, mod, post_g)


def _moe_layer(j, xp, xs, mod_p, mod_s, pre_g, post_g, router_w, router_b, wg, wu, wd):
    rw = jnp.pad(router_w, ((0, 0), (0, LANES - N_EXPERTS)))
    rb = _pad_lanes(router_b)
    hw_p, info_p, cnt_p = _route_call(xp, mod_p, pre_g, rw, rb, jnp.zeros((1, LANES), F32), tm=512)
    hw_s, info_s, cnt = _route_call(xs, mod_s, pre_g, rw, rb, cnt_p, tm=xs.shape[1])

    tm = EXPERT_TILE
    n_assign = 2 * (xp.shape[0] * xp.shape[1] + xs.shape[1])
    n_tiles = n_assign // tm + N_EXPERTS
    counts = cnt[0, :N_EXPERTS].astype(jnp.int32)
    group_tiles = (counts + tm - 1) // tm
    tile_end = jnp.cumsum(group_tiles)
    start = (tile_end - group_tiles) * tm
    n_valid = tile_end[-1]
    tile_id = jnp.minimum(jnp.arange(n_tiles, dtype=jnp.int32), n_valid - 1)
    tile_exp = jnp.sum(tile_id[:, None] >= tile_end[None, :], axis=1).astype(jnp.int32)

    def dests(info):
        flat = info.reshape(-1, LANES)
        d = [start[flat[:, i].astype(jnp.int32)] + flat[:, r].astype(jnp.int32)
             for i, r in ((INFO_I1, INFO_R1), (INFO_I2, INFO_R2))]
        return d[0], d[1]

    dp = dests(info_p)
    ds = dests(info_s)
    slots = jnp.zeros((n_tiles * tm, HALF), jnp.uint32)
    slots = _dispatch_call(dp[0], dp[1], hw_p.reshape(-1, HALF), slots, tm=1024)
    slots = _dispatch_call(ds[0], ds[1], hw_s.reshape(-1, HALF), slots, tm=xs.shape[1])
    y = _experts_call(j, tile_exp, n_valid.reshape(1), slots, wg, wu, wd)
    xp = _combine_call(dp[0], dp[1], y, info_p, xp, mod_p, post_g, tm=512)
    xs = _combine_call(ds[0], ds[1], y, info_s, xs, mod_s, post_g, tm=xs.shape[1])
    return xp, xs


def _pad_lanes(v):
    return jnp.pad(v, (0, LANES - v.shape[0])).reshape(1, LANES)


def kernel(x_prompt, x_sample, state_ssm, state_conv, c_prompt, c_sample, w_mod, b_mod, mix_pre_g, mix_post_g, ffn_pre_g, ffn_post_g, w_in, conv_w, conv_b, dt_bias, a_log, d_skip, ssm_norm_g, w_ssd_out, cmlp_ln_g, cmlp_ln_b, w_spatial, b_spatial, w_cmlp_out, w_o, ffn_wg, ffn_wu, ffn_wd, router_w, router_b, exp_wg, exp_wu, exp_wd):
    n_prompt = x_prompt.shape[0]
    n_sample = x_sample.shape[0]

    c_all = jnp.concatenate([c_prompt, c_sample, jnp.zeros((8, D_MODEL), F32)], axis=0)
    mod_all = _mod_call(c_all, w_mod, b_mod)
    mod_p = mod_all[:, :n_prompt].reshape(DEPTH, n_prompt, 1, 6 * D_MODEL)
    mod_s = mod_all[:, n_prompt:n_prompt + n_sample].reshape(DEPTH, 1, n_sample, 6 * D_MODEL)

    xp = x_prompt
    xs = x_sample.reshape(1, n_sample, D_MODEL)
    conv_hist = state_conv.reshape(DEPTH, n_sample, (CONV_WIDTH - 1) * CONV_DIM)

    o_xbc = SSM_INNER
    o_dt = o_xbc + CONV_DIM
    o_u = o_dt + SSM_HEADS

    dense_w = tuple(a.astype(BF16) for a in (ffn_wg, ffn_wu, ffn_wd))
    expert_w = tuple(a.astype(BF16) for a in (exp_wg, exp_wu, exp_wd))

    ssm_s = jnp.zeros(state_ssm.shape, F32)
    ssm_p, conv_p, conv_s, v_s = [], [], [], []
    for l in range(DEPTH):
        row = lambda a: a[l].reshape(1, -1)
        mixer_small = (conv_w[l], row(conv_b), _pad_lanes(dt_bias[l]), _pad_lanes(a_log[l]),
                       jnp.repeat(d_skip[l], SSM_HEAD_DIM).reshape(1, -1), row(ssm_norm_g),
                       row(cmlp_ln_g), row(cmlp_ln_b))

        w = w_in[l]
        w_main = jnp.concatenate([w[:, :o_xbc], w[:, o_u:], w[:, o_xbc:o_dt]], axis=1).astype(BF16)
        w_dt = jnp.pad(w[:, o_dt:o_u], ((0, 0), (0, LANES - SSM_HEADS))).astype(BF16)
        wa, wb, wo = (a[l].astype(BF16) for a in (w_ssd_out, w_cmlp_out, w_o))
        proj_s, dt_s = _inproj_call(0, xs, mod_s[l], row(mix_pre_g), w_main[None], w_dt[None], tm=n_sample)
        xp, hs_p, cs_p = _mixer_call(xp, mod_p[l], row(mix_pre_g), row(mix_post_g), w_main, w_dt,
                                     wa, wb, wo, *mixer_small, w_spatial[l], b_spatial[l].T)
        ya_s, yb_s, vr_s, ssm_s, cs_s = _mixer_step_call(
            l, proj_s[0], dt_s[0], state_ssm, conv_hist[l], ssm_s, *mixer_small,
            jnp.repeat(w_spatial[l, :, 0, 0], CMLP_GROUP_DIM).reshape(1, -1),
            jnp.repeat(b_spatial[l, :, 0], CMLP_GROUP_DIM).reshape(1, -1))
        xs = _outproj_call(0, ya_s[None], yb_s[None], proj_s, xs, mod_s[l], row(mix_post_g),
                           wa[None], wb[None], wo[None], tm=n_sample)

        j = l // 2
        if l % 2 == 0:
            xp = _dense_ffn_call(j, xp, mod_p[l], row(ffn_pre_g), row(ffn_post_g), *dense_w, tm=512)
            xs = _dense_ffn_call(j, xs, mod_s[l], row(ffn_pre_g), row(ffn_post_g), *dense_w, tm=n_sample)
        else:
            xp, xs = _moe_layer(j, xp, xs, mod_p[l], mod_s[l], row(ffn_pre_g), row(ffn_post_g),
                                router_w[j], router_b[j], *expert_w)

        ssm_p.append(hs_p)
        conv_p.append(cs_p)
        conv_s.append(cs_s.reshape(n_sample, CONV_WIDTH - 1, CONV_DIM))
        v_s.append(vr_s.reshape(n_sample, 1, D_MODEL))

    return (xp, xs.reshape(n_sample, 1, D_MODEL), jnp.stack(ssm_p), jnp.stack(conv_p),
            ssm_s, jnp.stack(conv_s), jnp.stack(v_s))
```

```python
import functools

import jax
import jax.numpy as jnp
from jax import lax
from jax.experimental import pallas as pl
from jax.experimental.pallas import tpu as pltpu

F32 = jnp.float32
BF16 = jnp.bfloat16

D_MODEL = 1024
DEPTH = 4
SSM_HEADS = 16
SSM_HEAD_DIM = 64
SSM_GROUPS = 2
SSM_STATE = 128
SSM_INNER = 1024
GROUP_WIDTH = SSM_INNER // SSM_GROUPS
HEADS_PER_GROUP = SSM_HEADS // SSM_GROUPS
CONV_WIDTH = 4
CONV_DIM = 1536
CHUNK = 128
CMLP_GROUPS = 8
CMLP_GROUP_DIM = 128
FFN_DIM = 2816
N_EXPERTS = 8
EPS = 1e-6
LANES = 128
COL_Z, COL_U, COL_V, COL_GA, COL_GB, COL_XS = 0, 1, 2, 3, 4, 5
COL_BC_512 = 12
PROJ_MAIN = 6 * D_MODEL + 2 * SSM_GROUPS * SSM_STATE
PROJ_TILE = 1664
MOD_SH1, MOD_SC1, MOD_G1, MOD_SH2, MOD_SC2, MOD_G2 = range(6)
VMEM_LIMIT = 56 * 1024 * 1024


def _params(semantics, row_dma=False):
    return pltpu.CompilerParams(dimension_semantics=semantics, vmem_limit_bytes=VMEM_LIMIT,
                                disable_bounds_checks=row_dma)


def _rms(x):
    return x * lax.rsqrt(jnp.mean(x * x, axis=-1, keepdims=True) + EPS)


def _dot(a, b):
    return jnp.dot(a, b, preferred_element_type=F32)


def _dot_nt(a, b):
    return lax.dot_general(a, b, (((1,), (1,)), ((), ())), preferred_element_type=F32)


def _dot_tn(a, b):
    return lax.dot_general(a, b, (((0,), (0,)), ((), ())), preferred_element_type=F32)


def _mod_kernel(c_ref, w_ref, b_ref, o_ref):
    a = jax.nn.silu(c_ref[...]).astype(BF16)
    o_ref[0] = _dot(a, w_ref[0].astype(BF16)) + b_ref[0]


def _mod_call(c_all, w_mod, b_mod):
    rows = c_all.shape[0]
    width = 2 * D_MODEL
    return pl.pallas_call(
        _mod_kernel,
        grid=(DEPTH, 6 * D_MODEL // width),
        in_specs=[
            pl.BlockSpec((rows, D_MODEL), lambda l, j: (0, 0)),
            pl.BlockSpec((1, D_MODEL, width), lambda l, j: (l, 0, j)),
            pl.BlockSpec((1, 1, width), lambda l, j: (l, 0, j)),
        ],
        out_specs=pl.BlockSpec((1, rows, width), lambda l, j: (l, 0, j)),
        out_shape=jax.ShapeDtypeStruct((DEPTH, rows, 6 * D_MODEL), F32),
        compiler_params=_params(("arbitrary", "arbitrary")),
        name="adaln_mod",
    )(c_all, w_mod, b_mod.reshape(DEPTH, 1, 6 * D_MODEL))


def _mod_spec(mod, tm, seg, grid_rank):
    per_row = mod.shape[1] != 1
    rows = tm if per_row else 1
    if grid_rank == 2:
        return pl.BlockSpec((1, rows, D_MODEL), lambda b, i: (b, i if per_row else 0, seg))
    return pl.BlockSpec((1, rows, D_MODEL), lambda b, i, j: (b, i if per_row else 0, seg))


def _inproj_kernel(x_ref, g_ref, sh_ref, sc_ref, w_ref, wdt_ref, p_ref, pdt_ref, h_scr):
    @pl.when(pl.program_id(2) == 0)
    def _():
        h = _rms(x_ref[0]) * g_ref[...] * (1.0 + sc_ref[0]) + sh_ref[0]
        hb = h.astype(BF16)
        h_scr[...] = hb
        pdt_ref[0] = _dot(hb, wdt_ref[...])

    p_ref[0] = _dot(h_scr[...], w_ref[...])


def _inproj_call(layer, x, mod, g, w_main, w_dt, tm):
    nb, seq, _ = x.shape
    grid = (nb, seq // tm, PROJ_MAIN // PROJ_TILE)
    return pl.pallas_call(
        _inproj_kernel,
        grid=grid,
        in_specs=[
            pl.BlockSpec((1, tm, D_MODEL), lambda b, i, j: (b, i, 0)),
            pl.BlockSpec((1, D_MODEL), lambda b, i, j: (0, 0)),
            _mod_spec(mod, tm, MOD_SH1, 3),
            _mod_spec(mod, tm, MOD_SC1, 3),
            pl.BlockSpec((None, D_MODEL, PROJ_TILE), lambda b, i, j: (layer, 0, j)),
            pl.BlockSpec((None, D_MODEL, LANES), lambda b, i, j: (layer, 0, 0)),
        ],
        out_specs=[
            pl.BlockSpec((1, tm, PROJ_TILE), lambda b, i, j: (b, i, j)),
            pl.BlockSpec((1, tm, LANES), lambda b, i, j: (b, i, 0)),
        ],
        out_shape=[
            jax.ShapeDtypeStruct((nb, seq, PROJ_MAIN), F32),
            jax.ShapeDtypeStruct((nb, seq, LANES), F32),
        ],
        scratch_shapes=[pltpu.VMEM((tm, D_MODEL), BF16)],
        compiler_params=_params(("arbitrary", "arbitrary", "arbitrary")),
        name="in_proj",
    )(x, g, mod, mod, w_main, w_dt)


def _gated_group_norm(y, z, norm_g):
    y = y * jax.nn.silu(z)
    parts = [_rms(y[:, g * GROUP_WIDTH:(g + 1) * GROUP_WIDTH]) for g in range(SSM_GROUPS)]
    return jnp.concatenate(parts, axis=-1) * norm_g


def _layernorm(x, g, b):
    mu = jnp.mean(x, axis=-1, keepdims=True)
    xc = x - mu
    return xc * lax.rsqrt(jnp.mean(xc * xc, axis=-1, keepdims=True) + EPS) * g + b


def _pair_columns(v, pair, lane_lo):
    h0 = 2 * pair
    return jnp.where(lane_lo, v[:, h0:h0 + 1], v[:, h0 + 1:h0 + 2])


MIX_CHUNKS = 4
MIX_GROUP = 1
P0_COLS = {"z": (COL_Z * D_MODEL, D_MODEL), "u": (COL_U * D_MODEL, D_MODEL), "v": (COL_V * D_MODEL, D_MODEL),
           "ga": (COL_GA * D_MODEL, D_MODEL), "gb": (COL_GB * D_MODEL, D_MODEL),
           "xs": (COL_XS * D_MODEL, D_MODEL), "bc": (COL_BC_512 * 512, 2 * SSM_GROUPS * SSM_STATE),
           "dt": (PROJ_MAIN, LANES)}


def _mixer_chunk(z, u, v, xs, bc, dt_raw,
                 cw_ref, cb_ref, dtb_ref, alog_ref, dskip_ref, ng_ref, lng_ref, lnb_ref, bsp_ref,
                 h_scr, cbuf, wsp_scr, causal, lane_lo, fill):
    cbuf[8:8 + CHUNK, 0:SSM_INNER] = xs
    cbuf[8:8 + CHUNK, SSM_INNER:CONV_DIM] = bc
    window = cbuf[...]
    acc = window * cw_ref[0:1, :]
    for k in range(1, CONV_WIDTH):
        acc = pltpu.roll(acc, 1, 0) + window * cw_ref[k:k + 1, :]
    acc = acc[8:8 + CHUNK, :]
    cbuf[0:8, :] = cbuf[CHUNK:8 + CHUNK, :]
    fill()

    xbc = jax.nn.silu(acc + cb_ref[...])
    xc = xbc[:, 0:SSM_INNER]
    bm = xbc[:, SSM_INNER:SSM_INNER + SSM_GROUPS * SSM_STATE].astype(BF16)
    cm = xbc[:, SSM_INNER + SSM_GROUPS * SSM_STATE:CONV_DIM].astype(BF16)
    fill()

    dt = jax.nn.softplus(dt_raw + dtb_ref[...])
    da = dt * (-jnp.exp(alog_ref[...]))
    cum = jnp.dot(causal.astype(F32), da, preferred_element_type=F32,
                  precision=lax.Precision.HIGHEST)
    last = cum[CHUNK - 1:CHUNK, :]
    ecum = jnp.exp(cum)
    wend = jnp.exp(last - cum) * dt
    elast = jnp.exp(last)
    cum_t = cum.T
    dt_t = dt.T
    fill()

    y_pairs = []
    for g in range(SSM_GROUPS):
        bg = bm[:, g * SSM_STATE:(g + 1) * SSM_STATE]
        cg = cm[:, g * SSM_STATE:(g + 1) * SSM_STATE]
        cb = _dot_nt(cg, bg)
        hg = h_scr[g * GROUP_WIDTH:(g + 1) * GROUP_WIDTH, :]
        y_state = _dot_nt(cg, hg.astype(BF16))
        xw_parts = []
        for q in range(HEADS_PER_GROUP // 2):
            pair = g * (HEADS_PER_GROUP // 2) + q
            mixes = []
            for h in (2 * pair, 2 * pair + 1):
                seg = cum[:, h:h + 1] - cum_t[h:h + 1, :]
                decay = jnp.where(causal, jnp.exp(seg), 0.0)
                mixes.append((cb * decay * dt_t[h:h + 1, :]).astype(BF16))
            xp = xc[:, pair * LANES:(pair + 1) * LANES]
            rhs = jnp.concatenate([jnp.where(lane_lo, xp, 0.0), jnp.where(lane_lo, 0.0, xp)],
                                  axis=0).astype(BF16)
            y_in = _dot(jnp.concatenate(mixes, axis=1), rhs)
            y_st = y_state[:, q * LANES:(q + 1) * LANES] * _pair_columns(ecum, pair, lane_lo)
            y_pairs.append(y_in + y_st)
            xw_parts.append((xp * _pair_columns(wend, pair, lane_lo)).astype(BF16))
            fill()
        upd = _dot_tn(jnp.concatenate(xw_parts, axis=1), bg)
        for r in range(HEADS_PER_GROUP):
            h = g * HEADS_PER_GROUP + r
            head = slice(h * SSM_HEAD_DIM, (h + 1) * SSM_HEAD_DIM)
            scale = jnp.broadcast_to(elast[0:1, h:h + 1], (SSM_HEAD_DIM, SSM_STATE))
            h_scr[head, :] = h_scr[head, :] * scale + upd[r * SSM_HEAD_DIM:(r + 1) * SSM_HEAD_DIM, :]

    y = jnp.concatenate(y_pairs, axis=1) + dskip_ref[...] * xc
    ya = _gated_group_norm(y, z, ng_ref[...]).astype(BF16)
    fill()

    ug = jax.nn.gelu(u, approximate=True)
    vn = _layernorm(jax.nn.gelu(v, approximate=True), lng_ref[...], lnb_ref[...])
    fill()
    gates = []
    for g in range(CMLP_GROUPS):
        vg = vn[:, g * CMLP_GROUP_DIM:(g + 1) * CMLP_GROUP_DIM].astype(BF16)
        gates.append(_dot(wsp_scr[g], vg) + bsp_ref[:, g:g + 1])
    yb = (ug * jnp.concatenate(gates, axis=1)).astype(BF16)
    return ya, yb


def _merge_project(ya, yb, ga, gb, x, g1, post_g, wa_ref, wb_ref, wo_ref):
    merged = (jax.nn.sigmoid(ga) * _dot(ya, wa_ref[...]) + jax.nn.sigmoid(gb) * _dot(yb, wb_ref[...]))
    o = _dot(merged.astype(BF16), wo_ref[...])
    return x + g1 * (_rms(o) * post_g)


def _mixer_kernel(x_ref, pre_ref, sh_ref, sc_ref, g1_ref, xn_ref, shn_ref, scn_ref, win_ref, wdt_ref,
                  cw_ref, cb_ref, dtb_ref, alog_ref, dskip_ref, ng_ref, lng_ref, lnb_ref, wsp_ref, bsp_ref,
                  pg_ref, wa_ref, wb_ref, wo_ref,
                  o_ref, ssm_ref, conv_ref,
                  h_scr, cbuf, wsp_scr, p0_scr):
    c = pl.program_id(1)
    row = lax.broadcasted_iota(jnp.int32, (CHUNK, CHUNK), 0)
    col = lax.broadcasted_iota(jnp.int32, (CHUNK, CHUNK), 1)
    causal = row >= col
    lane_lo = col < SSM_HEAD_DIM

    @pl.when(c == 0)
    def _():
        h_scr[...] = jnp.zeros_like(h_scr)
        cbuf[0:8, :] = jnp.zeros((8, CONV_DIM), F32)
        for g in range(CMLP_GROUPS):
            wsp_scr[g] = jnp.where(causal, wsp_ref[g], 0.0).astype(BF16)

    def project(hb, block, width=D_MODEL):
        return _dot(hb, win_ref[:, block * D_MODEL:block * D_MODEL + width])

    def projection_pieces(load_x, load_sh, load_sc, store):
        t = {}

        def prep():
            x = load_x()
            store("x", x)
            t["hb"] = (_rms(x) * pre_ref[...] * (1.0 + load_sc()) + load_sh()).astype(BF16)

        pieces = [prep]
        for name, block in (("z", COL_Z), ("u", COL_U), ("v", COL_V), ("xs", COL_XS), ("bc", COL_XS + 1),
                            ("ga", COL_GA), ("gb", COL_GB)):
            width = P0_COLS[name][1]
            pieces.append(lambda name=name, block=block, width=width:
                          store(name, project(t["hb"], block, width)))
        pieces.append(lambda: store("dt", _dot(t["hb"], wdt_ref[...])))
        return pieces

    group_rows = MIX_GROUP * CHUNK

    def store_first(name, value):
        if name != "x":
            start, width = P0_COLS[name]
            p0_scr[:, start:start + width] = value

    def out_projection(rows, p, ya, yb):
        o_ref[0, rows, :] = _merge_project(ya, yb, p["ga"], p["gb"], x_ref[0, rows, :], g1_ref[0], pg_ref[...],
                                           wa_ref, wb_ref, wo_ref)

    @pl.when(jnp.logical_and(pl.program_id(0) == 0, c == 0))
    def _():
        for piece in projection_pieces(lambda: x_ref[0, 0:group_rows, :], lambda: sh_ref[0], lambda: sc_ref[0],
                                       store_first):
            piece()

    class _FromScratch:
        def __getitem__(self, name):
            start, width = P0_COLS[name]
            return p0_scr.at[:, start:start + width]

    cur = _FromScratch()
    pending = []
    n_groups = MIX_CHUNKS // MIX_GROUP
    for j in range(n_groups):
        rows = slice(j * group_rows, (j + 1) * group_rows)
        if j + 1 < n_groups:
            nxt = {}
            nrows = slice((j + 1) * group_rows, (j + 2) * group_rows)
            more = projection_pieces(lambda nrows=nrows: x_ref[0, nrows, :], lambda: sh_ref[0],
                                     lambda: sc_ref[0], nxt.__setitem__)
        else:
            more = projection_pieces(lambda: xn_ref[0], lambda: shn_ref[0], lambda: scn_ref[0], store_first)
        pending = pending + more

        def fill():
            if pending:
                pending.pop(0)()

        yas, ybs = [], []
        for i in range(MIX_GROUP):
            sub = slice(i * CHUNK, (i + 1) * CHUNK)
            ya, yb = _mixer_chunk(*(cur[name][sub, :] for name in ("z", "u", "v", "xs", "bc", "dt")),
                                  cw_ref, cb_ref, dtb_ref, alog_ref, dskip_ref, ng_ref, lng_ref, lnb_ref,
                                  bsp_ref, h_scr, cbuf, wsp_scr, causal, lane_lo, fill)
            yas.append(ya)
            ybs.append(yb)
        while pending:
            fill()
        gates = {name: cur[name][...] for name in ("ga", "gb")}
        pending = [functools.partial(out_projection, rows, gates, jnp.concatenate(yas, axis=0),
                                     jnp.concatenate(ybs, axis=0))]
        if j + 1 < n_groups:
            cur = nxt
    pending.pop(0)()

    @pl.when(c == pl.num_programs(1) - 1)
    def _():
        conv_ref[0] = cbuf[5:8, :]
        ssm_ref[0] = h_scr[...].reshape(SSM_HEADS, SSM_HEAD_DIM, SSM_STATE)


def _mixer_call(x, mod, pre_g, post_g, w_main, w_dt, wa, wb, wo,
                conv_w, conv_b, dt_bias, a_log, d_skip, norm_g, ln_g, ln_b, w_sp, b_sp_t):
    nb, seq, _ = x.shape
    tm = MIX_CHUNKS * CHUNK

    def whole(a):
        zeros = (0,) * a.ndim
        return pl.BlockSpec(a.shape, lambda b, c: zeros)

    nsteps = seq // tm

    def following(b, c):
        lin = jnp.minimum(b * nsteps + c + 1, nb * nsteps - 1)
        return lin // nsteps, lin % nsteps

    def next_x(b, c):
        bn, cn = following(b, c)
        return bn, cn * (MIX_CHUNKS // MIX_GROUP), 0

    def next_mod(seg):
        return pl.BlockSpec((1, 1, D_MODEL), lambda b, c: (following(b, c)[0], 0, seg))

    small = (conv_w, conv_b, dt_bias, a_log, d_skip, norm_g, ln_g, ln_b, w_sp, b_sp_t)
    return pl.pallas_call(
        _mixer_kernel,
        grid=(nb, nsteps),
        in_specs=[pl.BlockSpec((1, tm, D_MODEL), lambda b, c: (b, c, 0)), whole(pre_g),
                  _mod_spec(mod, tm, MOD_SH1, 2), _mod_spec(mod, tm, MOD_SC1, 2), _mod_spec(mod, tm, MOD_G1, 2),
                  pl.BlockSpec((1, MIX_GROUP * CHUNK, D_MODEL), next_x), next_mod(MOD_SH1), next_mod(MOD_SC1),
                  whole(w_main), whole(w_dt)]
                 + [whole(a) for a in small]
                 + [whole(post_g), whole(wa), whole(wb), whole(wo)],
        out_specs=[
            pl.BlockSpec((1, tm, D_MODEL), lambda b, c: (b, c, 0)),
            pl.BlockSpec((1, SSM_HEADS, SSM_HEAD_DIM, SSM_STATE), lambda b, c: (b, 0, 0, 0)),
            pl.BlockSpec((1, CONV_WIDTH - 1, CONV_DIM), lambda b, c: (b, 0, 0)),
        ],
        out_shape=[
            jax.ShapeDtypeStruct(x.shape, F32),
            jax.ShapeDtypeStruct((nb, SSM_HEADS, SSM_HEAD_DIM, SSM_STATE), F32),
            jax.ShapeDtypeStruct((nb, CONV_WIDTH - 1, CONV_DIM), F32),
        ],
        scratch_shapes=[
            pltpu.VMEM((SSM_INNER, SSM_STATE), F32),
            pltpu.VMEM((CHUNK + 8, CONV_DIM), F32),
            pltpu.VMEM((CMLP_GROUPS, CHUNK, CHUNK), BF16),
            pltpu.VMEM((MIX_GROUP * CHUNK, PROJ_MAIN + LANES), F32),
        ],
        compiler_params=_params(("arbitrary", "arbitrary")),
        name="mixer_prompt",
    )(x, pre_g, mod, mod, mod, x, mod, mod, w_main, w_dt, *small, post_g, wa, wb, wo)


SAMPLE_TILE = 16


def _mixer_step_kernel(z_ref, u_ref, v_ref, xs_ref, bc_ref, dt_ref, ssm_in_ref, conv_in_ref,
                       cw_ref, cb_ref, dtb_ref, alog_ref, dskip_ref, ng_ref,
                       lng_ref, lnb_ref, wsp0_ref, bsp0_ref, ssm_all_ref,
                       ya_ref, yb_ref, vout_ref, ssm_ref, conv_ref):
    del ssm_all_ref
    tb = SAMPLE_TILE
    xbc_new = jnp.concatenate([xs_ref[...], bc_ref[...]], axis=1)
    hist = conv_in_ref[...]
    acc = xbc_new * cw_ref[CONV_WIDTH - 1:CONV_WIDTH, :]
    for k in range(CONV_WIDTH - 1):
        acc = acc + hist[:, k * CONV_DIM:(k + 1) * CONV_DIM] * cw_ref[k:k + 1, :]
    conv_ref[:, 0:(CONV_WIDTH - 2) * CONV_DIM] = hist[:, CONV_DIM:]
    conv_ref[:, (CONV_WIDTH - 2) * CONV_DIM:] = xbc_new

    xbc = jax.nn.silu(acc + cb_ref[...])
    xc = xbc[:, 0:SSM_INNER]
    bm = xbc[:, SSM_INNER:SSM_INNER + SSM_GROUPS * SSM_STATE]
    cm = xbc[:, SSM_INNER + SSM_GROUPS * SSM_STATE:CONV_DIM]
    dt = jax.nn.softplus(dt_ref[...] + dtb_ref[...])
    dec = jnp.exp(dt * (-jnp.exp(alog_ref[...])))

    def transposed(a):
        pad = jnp.zeros((LANES - tb, a.shape[1]), F32)
        return jnp.concatenate([a, pad], axis=0).T

    xc_t = transposed(xc)
    dt_t = transposed(dt)
    dec_t = transposed(dec)
    row_id = lax.broadcasted_iota(jnp.int32, (tb, SSM_STATE), 0)

    y_groups = [jnp.zeros((tb, GROUP_WIDTH), F32) for _ in range(SSM_GROUPS)]
    for b in range(tb):
        for g in range(SSM_GROUPS):
            b_row = bm[b:b + 1, g * SSM_STATE:(g + 1) * SSM_STATE]
            c_only = jnp.where(row_id == b, cm[:, g * SSM_STATE:(g + 1) * SSM_STATE], 0.0).astype(BF16)
            new_heads = []
            for r in range(HEADS_PER_GROUP):
                h = g * HEADS_PER_GROUP + r
                x_col = xc_t[h * SSM_HEAD_DIM:(h + 1) * SSM_HEAD_DIM, b:b + 1]
                push = x_col * dt_t[h:h + 1, b:b + 1]
                keep = jnp.broadcast_to(dec_t[h:h + 1, b:b + 1], (SSM_HEAD_DIM, SSM_STATE))
                h_new = ssm_in_ref[b, h] * keep + push * b_row
                ssm_ref[b, h] = h_new
                new_heads.append(h_new.astype(BF16))
            hg = jnp.concatenate(new_heads, axis=0)
            y_groups[g] = y_groups[g] + _dot_nt(c_only, hg)

    y = jnp.concatenate(y_groups, axis=1) + dskip_ref[...] * xc
    ya_ref[...] = _gated_group_norm(y, z_ref[...], ng_ref[...]).astype(BF16)

    ug = jax.nn.gelu(u_ref[...], approximate=True)
    vn = _layernorm(jax.nn.gelu(v_ref[...], approximate=True), lng_ref[...], lnb_ref[...])
    vout_ref[...] = vn
    yb_ref[...] = (ug * (vn * wsp0_ref[...] + bsp0_ref[...])).astype(BF16)


def _mixer_step_call(layer, proj, proj_dt, state_ssm, conv_hist, ssm_all, conv_w, conv_b, dt_bias, a_log,
                     d_skip, norm_g, ln_g, ln_b, w_sp0, b_sp0):
    nseq = proj.shape[0]
    tb = SAMPLE_TILE

    def col(block, width=D_MODEL):
        return pl.BlockSpec((tb, width), lambda i: (i, block))

    def whole(a):
        zeros = (0,) * a.ndim
        return pl.BlockSpec(a.shape, lambda i: zeros)

    small = (conv_w, conv_b, dt_bias, a_log, d_skip, norm_g, ln_g, ln_b, w_sp0, b_sp0)
    hist_width = (CONV_WIDTH - 1) * CONV_DIM
    return pl.pallas_call(
        _mixer_step_kernel,
        grid=(nseq // tb,),
        in_specs=[col(COL_Z), col(COL_U), col(COL_V), col(COL_XS),
                  col(COL_BC_512, 2 * SSM_GROUPS * SSM_STATE),
                  pl.BlockSpec((tb, LANES), lambda i: (i, 0)),
                  pl.BlockSpec((None, tb, SSM_HEADS, SSM_HEAD_DIM, SSM_STATE),
                               lambda i: (layer, i, 0, 0, 0)),
                  pl.BlockSpec((tb, hist_width), lambda i: (i, 0))]
                 + [whole(a) for a in small] + [pl.BlockSpec(memory_space=pl.ANY)],
        out_specs=[
            pl.BlockSpec((tb, SSM_INNER), lambda i: (i, 0)),
            pl.BlockSpec((tb, D_MODEL), lambda i: (i, 0)),
            pl.BlockSpec((tb, D_MODEL), lambda i: (i, 0)),
            pl.BlockSpec((None, tb, SSM_HEADS, SSM_HEAD_DIM, SSM_STATE), lambda i: (layer, i, 0, 0, 0)),
            pl.BlockSpec((tb, hist_width), lambda i: (i, 0)),
        ],
        out_shape=[
            jax.ShapeDtypeStruct((nseq, SSM_INNER), BF16),
            jax.ShapeDtypeStruct((nseq, D_MODEL), BF16),
            jax.ShapeDtypeStruct((nseq, D_MODEL), F32),
            jax.ShapeDtypeStruct(ssm_all.shape, F32),
            jax.ShapeDtypeStruct((nseq, hist_width), F32),
        ],
        input_output_aliases={8 + len(small): 3},
        compiler_params=_params(("arbitrary",)),
        name="mixer_sample",
    )(proj, proj, proj, proj, proj, proj_dt, state_ssm, conv_hist, *small, ssm_all)


def _outproj_kernel(ya_ref, yb_ref, ga_ref, gb_ref, x_ref, g1_ref, pg_ref, wa_ref, wb_ref, wo_ref, o_ref):
    o_ref[0] = _merge_project(ya_ref[0], yb_ref[0], ga_ref[0], gb_ref[0], x_ref[0], g1_ref[0], pg_ref[...],
                              wa_ref, wb_ref, wo_ref)


def _outproj_call(layer, ya, yb, proj, x, mod, post_g, wa, wb, wo, tm):
    nb, seq, _ = x.shape

    def rows(block=0):
        return pl.BlockSpec((1, tm, D_MODEL), lambda b, i: (b, i, block))

    def whole(a):
        return pl.BlockSpec(a.shape, lambda b, i: (0, 0))

    def of_layer(a):
        return pl.BlockSpec((None,) + a.shape[1:], lambda b, i: (layer, 0, 0))

    return pl.pallas_call(
        _outproj_kernel,
        grid=(nb, seq // tm),
        in_specs=[rows(), rows(), rows(COL_GA), rows(COL_GB), rows(),
                  _mod_spec(mod, tm, MOD_G1, 2), whole(post_g), of_layer(wa), of_layer(wb), of_layer(wo)],
        out_specs=rows(),
        out_shape=jax.ShapeDtypeStruct(x.shape, F32),
        compiler_params=_params(("arbitrary", "arbitrary")),
        name="out_proj",
    )(ya, yb, proj, proj, x, mod, post_g, wa, wb, wo)


FFN_CHUNK = 256


def _swiglu_into(hb, wg_ref, wu_ref, wd_ref, acc_ref):
    for f in range(FFN_DIM // FFN_CHUNK):
        cols = slice(f * FFN_CHUNK, (f + 1) * FFN_CHUNK)
        act = jax.nn.silu(_dot(hb, wg_ref[0, :, cols])) * _dot(hb, wu_ref[0, :, cols])
        part = _dot(act.astype(BF16), wd_ref[0, cols, :])
        if f == 0:
            acc_ref[...] = part
        else:
            acc_ref[...] += part


def _premod(x_ref, pre_ref, sh_ref, sc_ref):
    return _rms(x_ref[0]) * pre_ref[...] * (1.0 + sc_ref[0]) + sh_ref[0]


def _dense_ffn_kernel(x_ref, pre_ref, post_ref, sh_ref, sc_ref, g2_ref, wg_ref, wu_ref, wd_ref,
                      o_ref, acc_scr):
    hb = _premod(x_ref, pre_ref, sh_ref, sc_ref).astype(BF16)
    _swiglu_into(hb, wg_ref, wu_ref, wd_ref, acc_scr)
    o_ref[0] = x_ref[0] + g2_ref[0] * (_rms(acc_scr[...]) * post_ref[...])


def _dense_ffn_call(j, x, mod, pre_g, post_g, wg, wu, wd, tm):
    nb, seq, _ = x.shape

    def whole(a):
        zeros = (0,) * a.ndim
        return pl.BlockSpec(a.shape, lambda b, i: zeros)

    def of_layer(a):
        return pl.BlockSpec((1,) + a.shape[1:], lambda b, i: (j, 0, 0))

    rows = pl.BlockSpec((1, tm, D_MODEL), lambda b, i: (b, i, 0))
    return pl.pallas_call(
        _dense_ffn_kernel,
        grid=(nb, seq // tm),
        in_specs=[rows, whole(pre_g), whole(post_g),
                  _mod_spec(mod, tm, MOD_SH2, 2), _mod_spec(mod, tm, MOD_SC2, 2),
                  _mod_spec(mod, tm, MOD_G2, 2), of_layer(wg), of_layer(wu), of_layer(wd)],
        out_specs=rows,
        out_shape=jax.ShapeDtypeStruct(x.shape, F32),
        scratch_shapes=[pltpu.VMEM((tm, D_MODEL), F32)],
        compiler_params=_params(("arbitrary", "arbitrary")),
        name="dense_ffn",
    )(x, pre_g, post_g, mod, mod, mod, wg, wu, wd)


EXPERT_TILE = 512
HALF = D_MODEL // 2
INFO_I1, INFO_I2, INFO_P1, INFO_P2, INFO_R1, INFO_R2 = range(6)
HI_MASK = 0xFFFF0000
ISSUE_UNROLL = 8


def _route_kernel(x_ref, pre_ref, sh_ref, sc_ref, rw_ref, rb_ref, cnt0_ref,
                  hw_ref, info_ref, cnt_ref, carry):
    @pl.when(jnp.logical_and(pl.program_id(0) == 0, pl.program_id(1) == 0))
    def _():
        carry[...] = cnt0_ref[...]

    hb = _premod(x_ref, pre_ref, sh_ref, sc_ref).astype(BF16)
    tm = hb.shape[0]
    bits = lax.bitcast_convert_type(hb.astype(F32), jnp.uint32)
    hw_ref[0] = (bits[:, HALF:] & jnp.uint32(HI_MASK)) | (bits[:, :HALF] >> jnp.uint32(16))

    logits = _dot(hb, rw_ref[...].astype(BF16)) + rb_ref[...]
    lane = lax.broadcasted_iota(jnp.int32, logits.shape, 1).astype(F32)
    neg = jnp.float32(-jnp.inf)
    logits = jnp.where(lane < N_EXPERTS, logits, neg)
    m1 = jnp.max(logits, axis=-1, keepdims=True)
    i1 = jnp.min(jnp.where(logits == m1, lane, float(LANES)), axis=-1, keepdims=True)
    rest = jnp.where(lane == i1, neg, logits)
    m2 = jnp.max(rest, axis=-1, keepdims=True)
    i2 = jnp.min(jnp.where(rest == m2, lane, float(LANES)), axis=-1, keepdims=True)
    e2 = jnp.exp(m2 - m1)
    p1 = 1.0 / (1.0 + e2)
    p2 = e2 * p1

    member = jnp.logical_or(lane == i1, lane == i2)
    row = lax.broadcasted_iota(jnp.int32, (tm, tm), 0)
    col = lax.broadcasted_iota(jnp.int32, (tm, tm), 1)
    before = jnp.where(row > col, 1.0, 0.0).astype(BF16)
    ones = jnp.where(member, 1.0, 0.0)
    prior = _dot(before, ones.astype(BF16)) + carry[...]
    r1 = jnp.sum(jnp.where(lane == i1, prior, 0.0), axis=-1, keepdims=True)
    r2 = jnp.sum(jnp.where(lane == i2, prior, 0.0), axis=-1, keepdims=True)
    carry[...] += jnp.sum(ones, axis=0, keepdims=True)
    cnt_ref[...] = carry[...]

    info = jnp.zeros_like(logits)
    for k, v in ((INFO_I1, i1), (INFO_I2, i2), (INFO_P1, p1), (INFO_P2, p2), (INFO_R1, r1), (INFO_R2, r2)):
        info = jnp.where(lane == float(k), v, info)
    info_ref[0] = info


def _route_call(x, mod, pre_g, rw, rb, cnt0, tm):
    nb, seq, _ = x.shape

    def whole(a):
        return pl.BlockSpec(a.shape, lambda b, i: (0, 0))

    return pl.pallas_call(
        _route_kernel,
        grid=(nb, seq // tm),
        in_specs=[pl.BlockSpec((1, tm, D_MODEL), lambda b, i: (b, i, 0)), whole(pre_g),
                  _mod_spec(mod, tm, MOD_SH2, 2), _mod_spec(mod, tm, MOD_SC2, 2),
                  whole(rw), whole(rb), whole(cnt0)],
        out_specs=[pl.BlockSpec((1, tm, HALF), lambda b, i: (b, i, 0)),
                   pl.BlockSpec((1, tm, LANES), lambda b, i: (b, i, 0)),
                   pl.BlockSpec((1, LANES), lambda b, i: (0, 0))],
        out_shape=[jax.ShapeDtypeStruct((nb, seq, HALF), jnp.uint32),
                   jax.ShapeDtypeStruct((nb, seq, LANES), F32),
                   jax.ShapeDtypeStruct((1, LANES), F32)],
        scratch_shapes=[pltpu.VMEM((1, LANES), F32)],
        compiler_params=_params(("arbitrary", "arbitrary")),
        name="moe_route",
    )(x, pre_g, mod, mod, rw, rb, cnt0)


def _row_copy(src, src_row, dst, dst_row, sem):
    return pltpu.make_async_copy(src.at[pl.ds(src_row, 1)], dst.at[pl.ds(dst_row, 1)], sem)


def _dispatch_kernel(d1_ref, d2_ref, hw_ref, xs_in_ref, xs_ref, sem):
    del xs_in_ref
    tm = hw_ref.shape[0]
    base = pl.program_id(0) * tm

    def issue(r, carry):
        _row_copy(hw_ref, r, xs_ref, d1_ref[base + r], sem).start(priority=0)
        _row_copy(hw_ref, r, xs_ref, d2_ref[base + r], sem).start(priority=1)
        return carry

    lax.fori_loop(0, tm, issue, 0, unroll=ISSUE_UNROLL)
    for _ in range(2):
        pltpu.make_async_copy(hw_ref, xs_ref.at[pl.ds(0, tm)], sem).wait()


def _dispatch_call(dest1, dest2, hw, xs, tm):
    rows = hw.shape[0]
    return pl.pallas_call(
        _dispatch_kernel,
        grid_spec=pltpu.PrefetchScalarGridSpec(
            num_scalar_prefetch=2,
            grid=(rows // tm,),
            in_specs=[pl.BlockSpec((tm, HALF), lambda i, d1, d2: (i, 0)),
                      pl.BlockSpec(memory_space=pl.ANY)],
            out_specs=pl.BlockSpec(memory_space=pl.ANY),
            scratch_shapes=[pltpu.SemaphoreType.DMA(())],
        ),
        out_shape=jax.ShapeDtypeStruct(xs.shape, xs.dtype),
        input_output_aliases={3: 0},
        compiler_params=_params(("arbitrary",), row_dma=True),
        name="moe_dispatch",
    )(dest1, dest2, hw, xs)


def _experts_kernel(exp_ref, nv_ref, xs_ref, wg_ref, wu_ref, wd_ref, y_ref):
    del exp_ref
    used = pl.program_id(0) < nv_ref[0]

    @pl.when(used)
    def _():
        words = xs_ref[...]
        lo = lax.bitcast_convert_type(words << jnp.uint32(16), F32)
        hi = lax.bitcast_convert_type(words & jnp.uint32(HI_MASK), F32)
        hb = jnp.concatenate([lo, hi], axis=1).astype(BF16)
        _swiglu_into(hb, wg_ref, wu_ref, wd_ref, y_ref)

    @pl.when(jnp.logical_not(used))
    def _():
        y_ref[...] = jnp.zeros_like(y_ref)


def _experts_call(j, tile_exp, n_valid, xs, wg, wu, wd):
    n_tiles = tile_exp.shape[0]
    tm = EXPERT_TILE
    weight = lambda shape: pl.BlockSpec((None, 1) + shape, lambda i, ex, nv: (j, ex[i], 0, 0))
    return pl.pallas_call(
        _experts_kernel,
        grid_spec=pltpu.PrefetchScalarGridSpec(
            num_scalar_prefetch=2,
            grid=(n_tiles,),
            in_specs=[pl.BlockSpec((tm, HALF), lambda i, ex, nv: (i, 0)),
                      weight((D_MODEL, FFN_DIM)), weight((D_MODEL, FFN_DIM)), weight((FFN_DIM, D_MODEL))],
            out_specs=pl.BlockSpec((tm, D_MODEL), lambda i, ex, nv: (i, 0)),
        ),
        out_shape=jax.ShapeDtypeStruct((xs.shape[0], D_MODEL), F32),
        compiler_params=_params(("arbitrary",)),
        name="moe_experts",
    )(tile_exp, n_valid, xs, wg, wu, wd)


def _combine_kernel(d1_ref, d2_ref, y_ref, info_ref, x_ref, g2_ref, post_ref, o_ref, ybuf, sems):
    tm = x_ref.shape[1]
    step = pl.program_id(0) * pl.num_programs(1) + pl.program_id(1)
    n_steps = pl.num_programs(0) * pl.num_programs(1)

    def gather(tile, slot):
        base = tile * tm

        def issue(r, carry):
            _row_copy(y_ref, d1_ref[base + r], ybuf.at[slot, 0], r, sems.at[slot]).start(priority=0)
            _row_copy(y_ref, d2_ref[base + r], ybuf.at[slot, 1], r, sems.at[slot]).start(priority=1)
            return carry

        lax.fori_loop(0, tm, issue, 0, unroll=ISSUE_UNROLL)

    slot = step % 2

    @pl.when(step == 0)
    def _():
        gather(step, slot)

    @pl.when(step + 1 < n_steps)
    def _():
        gather(step + 1, 1 - slot)

    for k in range(2):
        pltpu.make_async_copy(y_ref.at[pl.ds(0, tm)], ybuf.at[slot, k], sems.at[slot]).wait()

    info = info_ref[0]
    f = info[:, INFO_P1:INFO_P1 + 1] * ybuf[slot, 0] + info[:, INFO_P2:INFO_P2 + 1] * ybuf[slot, 1]
    o_ref[0] = x_ref[0] + g2_ref[0] * (_rms(f) * post_ref[...])


def _combine_call(dest1, dest2, y, info, x, mod, post_g, tm):
    nb, seq, _ = x.shape
    per_row = mod.shape[1] != 1
    return pl.pallas_call(
        _combine_kernel,
        grid_spec=pltpu.PrefetchScalarGridSpec(
            num_scalar_prefetch=2,
            grid=(nb, seq // tm),
            in_specs=[pl.BlockSpec(memory_space=pl.ANY),
                      pl.BlockSpec((1, tm, LANES), lambda b, i, d1, d2: (b, i, 0)),
                      pl.BlockSpec((1, tm, D_MODEL), lambda b, i, d1, d2: (b, i, 0)),
                      pl.BlockSpec((1, tm if per_row else 1, D_MODEL),
                                   lambda b, i, d1, d2: (b, i if per_row else 0, MOD_G2)),
                      pl.BlockSpec(post_g.shape, lambda b, i, d1, d2: (0, 0))],
            out_specs=pl.BlockSpec((1, tm, D_MODEL), lambda b, i, d1, d2: (b, i, 0)),
            scratch_shapes=[pltpu.VMEM((2, 2, tm, D_MODEL), F32), pltpu.SemaphoreType.DMA((2,))],
        ),
        out_shape=jax.ShapeDtypeStruct(x.shape, F32),
        compiler_params=_params(("arbitrary", "arbitrary"), row_dma=True),
        name="moe_combine",
    )(dest1, dest2, y, info, x, mod, post_g)


def _moe_layer(j, xp, xs, mod_p, mod_s, pre_g, post_g, router_w, router_b, wg, wu, wd):
    rw = jnp.pad(router_w, ((0, 0), (0, LANES - N_EXPERTS)))
    rb = _pad_lanes(router_b)
    hw_p, info_p, cnt_p = _route_call(xp, mod_p, pre_g, rw, rb, jnp.zeros((1, LANES), F32), tm=512)
    hw_s, info_s, cnt = _route_call(xs, mod_s, pre_g, rw, rb, cnt_p, tm=xs.shape[1])

    tm = EXPERT_TILE
    n_assign = 2 * (xp.shape[0] * xp.shape[1] + xs.shape[1])
    n_tiles = n_assign // tm + N_EXPERTS
    counts = cnt[0, :N_EXPERTS].astype(jnp.int32)
    group_tiles = (counts + tm - 1) // tm
    tile_end = jnp.cumsum(group_tiles)
    start = (tile_end - group_tiles) * tm
    n_valid = tile_end[-1]
    tile_id = jnp.minimum(jnp.arange(n_tiles, dtype=jnp.int32), n_valid - 1)
    tile_exp = jnp.sum(tile_id[:, None] >= tile_end[None, :], axis=1).astype(jnp.int32)

    def dests(info):
        flat = info.reshape(-1, LANES)
        d = [start[flat[:, i].astype(jnp.int32)] + flat[:, r].astype(jnp.int32)
             for i, r in ((INFO_I1, INFO_R1), (INFO_I2, INFO_R2))]
        return d[0], d[1]

    dp = dests(info_p)
    ds = dests(info_s)
    slots = jnp.zeros((n_tiles * tm, HALF), jnp.uint32)
    slots = _dispatch_call(dp[0], dp[1], hw_p.reshape(-1, HALF), slots, tm=1024)
    slots = _dispatch_call(ds[0], ds[1], hw_s.reshape(-1, HALF), slots, tm=xs.shape[1])
    y = _experts_call(j, tile_exp, n_valid.reshape(1), slots, wg, wu, wd)
    xp = _combine_call(dp[0], dp[1], y, info_p, xp, mod_p, post_g, tm=512)
    xs = _combine_call(ds[0], ds[1], y, info_s, xs, mod_s, post_g, tm=xs.shape[1])
    return xp, xs


def _pad_lanes(v):
    return jnp.pad(v, (0, LANES - v.shape[0])).reshape(1, LANES)


def kernel(x_prompt, x_sample, state_ssm, state_conv, c_prompt, c_sample, w_mod, b_mod, mix_pre_g, mix_post_g, ffn_pre_g, ffn_post_g, w_in, conv_w, conv_b, dt_bias, a_log, d_skip, ssm_norm_g, w_ssd_out, cmlp_ln_g, cmlp_ln_b, w_spatial, b_spatial, w_cmlp_out, w_o, ffn_wg, ffn_wu, ffn_wd, router_w, router_b, exp_wg, exp_wu, exp_wd):
    n_prompt = x_prompt.shape[0]
    n_sample = x_sample.shape[0]

    c_all = jnp.concatenate([c_prompt, c_sample, jnp.zeros((8, D_MODEL), F32)], axis=0)
    mod_all = _mod_call(c_all, w_mod, b_mod)
    mod_p = mod_all[:, :n_prompt].reshape(DEPTH, n_prompt, 1, 6 * D_MODEL)
    mod_s = mod_all[:, n_prompt:n_prompt + n_sample].reshape(DEPTH, 1, n_sample, 6 * D_MODEL)

    xp = x_prompt
    xs = x_sample.reshape(1, n_sample, D_MODEL)
    conv_hist = state_conv.reshape(DEPTH, n_sample, (CONV_WIDTH - 1) * CONV_DIM)

    o_xbc = SSM_INNER
    o_dt = o_xbc + CONV_DIM
    o_u = o_dt + SSM_HEADS

    w_main = jnp.concatenate([w_in[:, :, :o_xbc], w_in[:, :, o_u:], w_in[:, :, o_xbc:o_dt]], axis=2).astype(BF16)
    w_dt = jnp.pad(w_in[:, :, o_dt:o_u], ((0, 0), (0, 0), (0, LANES - SSM_HEADS))).astype(BF16)
    wa, wb, wo = (a.astype(BF16) for a in (w_ssd_out, w_cmlp_out, w_o))
    dense_w = tuple(a.astype(BF16) for a in (ffn_wg, ffn_wu, ffn_wd))
    expert_w = tuple(a.astype(BF16) for a in (exp_wg, exp_wu, exp_wd))

    ssm_s = jnp.zeros(state_ssm.shape, F32)
    ssm_p, conv_p, conv_s, v_s = [], [], [], []
    for l in range(DEPTH):
        row = lambda a: a[l].reshape(1, -1)
        mixer_small = (conv_w[l], row(conv_b), _pad_lanes(dt_bias[l]), _pad_lanes(a_log[l]),
                       jnp.repeat(d_skip[l], SSM_HEAD_DIM).reshape(1, -1), row(ssm_norm_g),
                       row(cmlp_ln_g), row(cmlp_ln_b))

        proj_s, dt_s = _inproj_call(l, xs, mod_s[l], row(mix_pre_g), w_main, w_dt, tm=n_sample)
        xp, hs_p, cs_p = _mixer_call(xp, mod_p[l], row(mix_pre_g), row(mix_post_g), w_main[l], w_dt[l],
                                     wa[l], wb[l], wo[l], *mixer_small, w_spatial[l], b_spatial[l].T)
        ya_s, yb_s, vr_s, ssm_s, cs_s = _mixer_step_call(
            l, proj_s[0], dt_s[0], state_ssm, conv_hist[l], ssm_s, *mixer_small,
            jnp.repeat(w_spatial[l, :, 0, 0], CMLP_GROUP_DIM).reshape(1, -1),
            jnp.repeat(b_spatial[l, :, 0], CMLP_GROUP_DIM).reshape(1, -1))
        xs = _outproj_call(l, ya_s[None], yb_s[None], proj_s, xs, mod_s[l], row(mix_post_g), wa, wb, wo,
                           tm=n_sample)

        j = l // 2
        if l % 2 == 0:
            xp = _dense_ffn_call(j, xp, mod_p[l], row(ffn_pre_g), row(ffn_post_g), *dense_w, tm=512)
            xs = _dense_ffn_call(j, xs, mod_s[l], row(ffn_pre_g), row(ffn_post_g), *dense_w, tm=n_sample)
        else:
            xp, xs = _moe_layer(j, xp, xs, mod_p[l], mod_s[l], row(ffn_pre_g), row(ffn_post_g),
                                router_w[j], router_b[j], *expert_w)

        ssm_p.append(hs_p)
        conv_p.append(cs_p)
        conv_s.append(cs_s.reshape(n_sample, CONV_WIDTH - 1, CONV_DIM))
        v_s.append(vr_s.reshape(n_sample, 1, D_MODEL))

    return (xp, xs.reshape(n_sample, 1, D_MODEL), jnp.stack(ssm_p), jnp.stack(conv_p),
            ssm_s, jnp.stack(conv_s), jnp.stack(v_s))
```

```python
import functools

import jax
import jax.numpy as jnp
from jax import lax
from jax.experimental import pallas as pl
from jax.experimental.pallas import tpu as pltpu

F32 = jnp.float32
BF16 = jnp.bfloat16

D_MODEL = 1024
DEPTH = 4
SSM_HEADS = 16
SSM_HEAD_DIM = 64
SSM_GROUPS = 2
SSM_STATE = 128
SSM_INNER = 1024
GROUP_WIDTH = SSM_INNER // SSM_GROUPS
HEADS_PER_GROUP = SSM_HEADS // SSM_GROUPS
CONV_WIDTH = 4
CONV_DIM = 1536
CHUNK = 128
CMLP_GROUPS = 8
CMLP_GROUP_DIM = 128
FFN_DIM = 2816
N_EXPERTS = 8
EPS = 1e-6
LANES = 128
COL_Z, COL_U, COL_V, COL_GA, COL_GB, COL_XS = 0, 1, 2, 3, 4, 5
COL_BC_512 = 12
PROJ_MAIN = 6 * D_MODEL + 2 * SSM_GROUPS * SSM_STATE
PROJ_TILE = 1664
MOD_SH1, MOD_SC1, MOD_G1, MOD_SH2, MOD_SC2, MOD_G2 = range(6)
VMEM_LIMIT = 56 * 1024 * 1024


def _params(semantics, row_dma=False):
    return pltpu.CompilerParams(dimension_semantics=semantics, vmem_limit_bytes=VMEM_LIMIT,
                                disable_bounds_checks=row_dma)


def _rms(x):
    return x * lax.rsqrt(jnp.mean(x * x, axis=-1, keepdims=True) + EPS)


def _dot(a, b):
    return jnp.dot(a, b, preferred_element_type=F32)


def _dot_nt(a, b):
    return lax.dot_general(a, b, (((1,), (1,)), ((), ())), preferred_element_type=F32)


def _dot_tn(a, b):
    return lax.dot_general(a, b, (((0,), (0,)), ((), ())), preferred_element_type=F32)


def _mod_kernel(c_ref, w_ref, b_ref, o_ref):
    a = jax.nn.silu(c_ref[...]).astype(BF16)
    o_ref[0] = _dot(a, w_ref[0].astype(BF16)) + b_ref[0]


def _mod_call(c_all, w_mod, b_mod):
    rows = c_all.shape[0]
    width = 2 * D_MODEL
    return pl.pallas_call(
        _mod_kernel,
        grid=(DEPTH, 6 * D_MODEL // width),
        in_specs=[
            pl.BlockSpec((rows, D_MODEL), lambda l, j: (0, 0)),
            pl.BlockSpec((1, D_MODEL, width), lambda l, j: (l, 0, j)),
            pl.BlockSpec((1, 1, width), lambda l, j: (l, 0, j)),
        ],
        out_specs=pl.BlockSpec((1, rows, width), lambda l, j: (l, 0, j)),
        out_shape=jax.ShapeDtypeStruct((DEPTH, rows, 6 * D_MODEL), F32),
        compiler_params=_params(("arbitrary", "arbitrary")),
        name="adaln_mod",
    )(c_all, w_mod, b_mod.reshape(DEPTH, 1, 6 * D_MODEL))


def _mod_spec(mod, tm, seg, grid_rank):
    per_row = mod.shape[1] != 1
    rows = tm if per_row else 1
    if grid_rank == 2:
        return pl.BlockSpec((1, rows, D_MODEL), lambda b, i: (b, i if per_row else 0, seg))
    return pl.BlockSpec((1, rows, D_MODEL), lambda b, i, j: (b, i if per_row else 0, seg))


def _inproj_kernel(x_ref, g_ref, sh_ref, sc_ref, w_ref, wdt_ref, p_ref, pdt_ref, h_scr):
    @pl.when(pl.program_id(2) == 0)
    def _():
        h = _rms(x_ref[0]) * g_ref[...] * (1.0 + sc_ref[0]) + sh_ref[0]
        hb = h.astype(BF16)
        h_scr[...] = hb
        pdt_ref[0] = _dot(hb, wdt_ref[...])

    p_ref[0] = _dot(h_scr[...], w_ref[...])


def _inproj_call(layer, x, mod, g, w_main, w_dt, tm):
    nb, seq, _ = x.shape
    grid = (nb, seq // tm, PROJ_MAIN // PROJ_TILE)
    return pl.pallas_call(
        _inproj_kernel,
        grid=grid,
        in_specs=[
            pl.BlockSpec((1, tm, D_MODEL), lambda b, i, j: (b, i, 0)),
            pl.BlockSpec((1, D_MODEL), lambda b, i, j: (0, 0)),
            _mod_spec(mod, tm, MOD_SH1, 3),
            _mod_spec(mod, tm, MOD_SC1, 3),
            pl.BlockSpec((None, D_MODEL, PROJ_TILE), lambda b, i, j: (layer, 0, j)),
            pl.BlockSpec((None, D_MODEL, LANES), lambda b, i, j: (layer, 0, 0)),
        ],
        out_specs=[
            pl.BlockSpec((1, tm, PROJ_TILE), lambda b, i, j: (b, i, j)),
            pl.BlockSpec((1, tm, LANES), lambda b, i, j: (b, i, 0)),
        ],
        out_shape=[
            jax.ShapeDtypeStruct((nb, seq, PROJ_MAIN), F32),
            jax.ShapeDtypeStruct((nb, seq, LANES), F32),
        ],
        scratch_shapes=[pltpu.VMEM((tm, D_MODEL), BF16)],
        compiler_params=_params(("arbitrary", "arbitrary", "arbitrary")),
        name="in_proj",
    )(x, g, mod, mod, w_main, w_dt)


def _gated_group_norm(y, z, norm_g):
    y = y * jax.nn.silu(z)
    parts = [_rms(y[:, g * GROUP_WIDTH:(g + 1) * GROUP_WIDTH]) for g in range(SSM_GROUPS)]
    return jnp.concatenate(parts, axis=-1) * norm_g


def _layernorm(x, g, b):
    mu = jnp.mean(x, axis=-1, keepdims=True)
    xc = x - mu
    return xc * lax.rsqrt(jnp.mean(xc * xc, axis=-1, keepdims=True) + EPS) * g + b


def _pair_columns(v, pair, lane_lo):
    h0 = 2 * pair
    return jnp.where(lane_lo, v[:, h0:h0 + 1], v[:, h0 + 1:h0 + 2])


MIX_CHUNKS = 4
MIX_GROUP = 1
P0_COLS = {"z": (COL_Z * D_MODEL, D_MODEL), "u": (COL_U * D_MODEL, D_MODEL), "v": (COL_V * D_MODEL, D_MODEL),
           "ga": (COL_GA * D_MODEL, D_MODEL), "gb": (COL_GB * D_MODEL, D_MODEL),
           "xs": (COL_XS * D_MODEL, D_MODEL), "bc": (COL_BC_512 * 512, 2 * SSM_GROUPS * SSM_STATE),
           "dt": (PROJ_MAIN, LANES)}


def _mixer_chunk(z, u, v, xs, bc, dt_raw,
                 cw_ref, cb_ref, dtb_ref, alog_ref, dskip_ref, ng_ref, lng_ref, lnb_ref, bsp_ref,
                 h_scr, cbuf, wsp_scr, causal, lane_lo, fill):
    cbuf[8:8 + CHUNK, 0:SSM_INNER] = xs
    cbuf[8:8 + CHUNK, SSM_INNER:CONV_DIM] = bc
    window = cbuf[...]
    acc = window * cw_ref[0:1, :]
    for k in range(1, CONV_WIDTH):
        acc = pltpu.roll(acc, 1, 0) + window * cw_ref[k:k + 1, :]
    acc = acc[8:8 + CHUNK, :]
    cbuf[0:8, :] = cbuf[CHUNK:8 + CHUNK, :]
    fill()

    xbc = jax.nn.silu(acc + cb_ref[...])
    xc = xbc[:, 0:SSM_INNER]
    bm = xbc[:, SSM_INNER:SSM_INNER + SSM_GROUPS * SSM_STATE].astype(BF16)
    cm = xbc[:, SSM_INNER + SSM_GROUPS * SSM_STATE:CONV_DIM].astype(BF16)
    fill()

    dt = jax.nn.softplus(dt_raw + dtb_ref[...])
    da = dt * (-jnp.exp(alog_ref[...]))
    cum = jnp.dot(causal.astype(F32), da, preferred_element_type=F32,
                  precision=lax.Precision.HIGHEST)
    last = cum[CHUNK - 1:CHUNK, :]
    ecum = jnp.exp(cum)
    wend = jnp.exp(last - cum) * dt
    elast = jnp.exp(last)
    cum_t = cum.T
    dt_t = dt.T
    fill()

    y_pairs = []
    for g in range(SSM_GROUPS):
        bg = bm[:, g * SSM_STATE:(g + 1) * SSM_STATE]
        cg = cm[:, g * SSM_STATE:(g + 1) * SSM_STATE]
        cb = _dot_nt(cg, bg)
        hg = h_scr[g * GROUP_WIDTH:(g + 1) * GROUP_WIDTH, :]
        y_state = _dot_nt(cg, hg.astype(BF16))
        xw_parts = []
        for q in range(HEADS_PER_GROUP // 2):
            pair = g * (HEADS_PER_GROUP // 2) + q
            mixes = []
            for h in (2 * pair, 2 * pair + 1):
                seg = cum[:, h:h + 1] - cum_t[h:h + 1, :]
                decay = jnp.where(causal, jnp.exp(seg), 0.0)
                mixes.append((cb * decay * dt_t[h:h + 1, :]).astype(BF16))
            xp = xc[:, pair * LANES:(pair + 1) * LANES]
            rhs = jnp.concatenate([jnp.where(lane_lo, xp, 0.0), jnp.where(lane_lo, 0.0, xp)],
                                  axis=0).astype(BF16)
            y_in = _dot(jnp.concatenate(mixes, axis=1), rhs)
            y_st = y_state[:, q * LANES:(q + 1) * LANES] * _pair_columns(ecum, pair, lane_lo)
            y_pairs.append(y_in + y_st)
            xw_parts.append((xp * _pair_columns(wend, pair, lane_lo)).astype(BF16))
            fill()
        upd = _dot_tn(jnp.concatenate(xw_parts, axis=1), bg)
        for r in range(HEADS_PER_GROUP):
            h = g * HEADS_PER_GROUP + r
            head = slice(h * SSM_HEAD_DIM, (h + 1) * SSM_HEAD_DIM)
            scale = jnp.broadcast_to(elast[0:1, h:h + 1], (SSM_HEAD_DIM, SSM_STATE))
            h_scr[head, :] = h_scr[head, :] * scale + upd[r * SSM_HEAD_DIM:(r + 1) * SSM_HEAD_DIM, :]

    y = jnp.concatenate(y_pairs, axis=1) + dskip_ref[...] * xc
    ya = _gated_group_norm(y, z, ng_ref[...]).astype(BF16)
    fill()

    ug = jax.nn.gelu(u, approximate=True)
    vn = _layernorm(jax.nn.gelu(v, approximate=True), lng_ref[...], lnb_ref[...])
    fill()
    gates = []
    for g in range(CMLP_GROUPS):
        vg = vn[:, g * CMLP_GROUP_DIM:(g + 1) * CMLP_GROUP_DIM].astype(BF16)
        gates.append(_dot(wsp_scr[g], vg) + bsp_ref[:, g:g + 1])
    yb = (ug * jnp.concatenate(gates, axis=1)).astype(BF16)
    return ya, yb


def _merge_project(ya, yb, ga, gb, x, g1, post_g, wa_ref, wb_ref, wo_ref):
    merged = (jax.nn.sigmoid(ga) * _dot(ya, wa_ref[...]) + jax.nn.sigmoid(gb) * _dot(yb, wb_ref[...]))
    o = _dot(merged.astype(BF16), wo_ref[...])
    return x + g1 * (_rms(o) * post_g)


def _mixer_kernel(x_ref, pre_ref, sh_ref, sc_ref, g1_ref, xn_ref, shn_ref, scn_ref, win_ref, wdt_ref,
                  cw_ref, cb_ref, dtb_ref, alog_ref, dskip_ref, ng_ref, lng_ref, lnb_ref, wsp_ref, bsp_ref,
                  pg_ref, wa_ref, wb_ref, wo_ref,
                  o_ref, ssm_ref, conv_ref,
                  h_scr, cbuf, wsp_scr, p0_scr):
    c = pl.program_id(1)
    row = lax.broadcasted_iota(jnp.int32, (CHUNK, CHUNK), 0)
    col = lax.broadcasted_iota(jnp.int32, (CHUNK, CHUNK), 1)
    causal = row >= col
    lane_lo = col < SSM_HEAD_DIM

    @pl.when(c == 0)
    def _():
        h_scr[...] = jnp.zeros_like(h_scr)
        cbuf[0:8, :] = jnp.zeros((8, CONV_DIM), F32)
        for g in range(CMLP_GROUPS):
            wsp_scr[g] = jnp.where(causal, wsp_ref[g], 0.0).astype(BF16)

    def project(hb, block, width=D_MODEL):
        return _dot(hb, win_ref[:, block * D_MODEL:block * D_MODEL + width])

    def projection_pieces(load_x, load_sh, load_sc, store):
        t = {}

        def prep():
            x = load_x()
            store("x", x)
            t["hb"] = (_rms(x) * pre_ref[...] * (1.0 + load_sc()) + load_sh()).astype(BF16)

        pieces = [prep]
        for name, block in (("z", COL_Z), ("u", COL_U), ("v", COL_V), ("xs", COL_XS), ("bc", COL_XS + 1),
                            ("ga", COL_GA), ("gb", COL_GB)):
            width = P0_COLS[name][1]
            pieces.append(lambda name=name, block=block, width=width:
                          store(name, project(t["hb"], block, width)))
        pieces.append(lambda: store("dt", _dot(t["hb"], wdt_ref[...])))
        return pieces

    group_rows = MIX_GROUP * CHUNK

    def store_first(name, value):
        if name != "x":
            start, width = P0_COLS[name]
            p0_scr[:, start:start + width] = value

    def out_projection(rows, p, ya, yb):
        o_ref[0, rows, :] = _merge_project(ya, yb, p["ga"], p["gb"], x_ref[0, rows, :], g1_ref[0], pg_ref[...],
                                           wa_ref, wb_ref, wo_ref)

    @pl.when(jnp.logical_and(pl.program_id(0) == 0, c == 0))
    def _():
        for piece in projection_pieces(lambda: x_ref[0, 0:group_rows, :], lambda: sh_ref[0], lambda: sc_ref[0],
                                       store_first):
            piece()

    class _FromScratch:
        def __getitem__(self, name):
            start, width = P0_COLS[name]
            return p0_scr.at[:, start:start + width]

    cur = _FromScratch()
    pending = []
    n_groups = MIX_CHUNKS // MIX_GROUP
    for j in range(n_groups):
        rows = slice(j * group_rows, (j + 1) * group_rows)
        if j + 1 < n_groups:
            nxt = {}
            nrows = slice((j + 1) * group_rows, (j + 2) * group_rows)
            more = projection_pieces(lambda nrows=nrows: x_ref[0, nrows, :], lambda: sh_ref[0],
                                     lambda: sc_ref[0], nxt.__setitem__)
        else:
            more = projection_pieces(lambda: xn_ref[0], lambda: shn_ref[0], lambda: scn_ref[0], store_first)
        pending = pending + more

        def fill():
            if pending:
                pending.pop(0)()

        yas, ybs = [], []
        for i in range(MIX_GROUP):
            sub = slice(i * CHUNK, (i + 1) * CHUNK)
            ya, yb = _mixer_chunk(*(cur[name][sub, :] for name in ("z", "u", "v", "xs", "bc", "dt")),
                                  cw_ref, cb_ref, dtb_ref, alog_ref, dskip_ref, ng_ref, lng_ref, lnb_ref,
                                  bsp_ref, h_scr, cbuf, wsp_scr, causal, lane_lo, fill)
            yas.append(ya)
            ybs.append(yb)
        while pending:
            fill()
        gates = {name: cur[name][...] for name in ("ga", "gb")}
        pending = [functools.partial(out_projection, rows, gates, jnp.concatenate(yas, axis=0),
                                     jnp.concatenate(ybs, axis=0))]
        if j + 1 < n_groups:
            cur = nxt
    pending.pop(0)()

    @pl.when(c == pl.num_programs(1) - 1)
    def _():
        conv_ref[0] = cbuf[5:8, :]
        ssm_ref[0] = h_scr[...].reshape(SSM_HEADS, SSM_HEAD_DIM, SSM_STATE)


def _mixer_call(x, mod, pre_g, post_g, w_main, w_dt, wa, wb, wo,
                conv_w, conv_b, dt_bias, a_log, d_skip, norm_g, ln_g, ln_b, w_sp, b_sp_t):
    nb, seq, _ = x.shape
    tm = MIX_CHUNKS * CHUNK

    def whole(a):
        zeros = (0,) * a.ndim
        return pl.BlockSpec(a.shape, lambda b, c: zeros)

    nsteps = seq // tm

    def following(b, c):
        lin = jnp.minimum(b * nsteps + c + 1, nb * nsteps - 1)
        return lin // nsteps, lin % nsteps

    def next_x(b, c):
        bn, cn = following(b, c)
        return bn, cn * (MIX_CHUNKS // MIX_GROUP), 0

    def next_mod(seg):
        return pl.BlockSpec((1, 1, D_MODEL), lambda b, c: (following(b, c)[0], 0, seg))

    small = (conv_w, conv_b, dt_bias, a_log, d_skip, norm_g, ln_g, ln_b, w_sp, b_sp_t)
    return pl.pallas_call(
        _mixer_kernel,
        grid=(nb, nsteps),
        in_specs=[pl.BlockSpec((1, tm, D_MODEL), lambda b, c: (b, c, 0)), whole(pre_g),
                  _mod_spec(mod, tm, MOD_SH1, 2), _mod_spec(mod, tm, MOD_SC1, 2), _mod_spec(mod, tm, MOD_G1, 2),
                  pl.BlockSpec((1, MIX_GROUP * CHUNK, D_MODEL), next_x), next_mod(MOD_SH1), next_mod(MOD_SC1),
                  whole(w_main), whole(w_dt)]
                 + [whole(a) for a in small]
                 + [whole(post_g), whole(wa), whole(wb), whole(wo)],
        out_specs=[
            pl.BlockSpec((1, tm, D_MODEL), lambda b, c: (b, c, 0)),
            pl.BlockSpec((1, SSM_HEADS, SSM_HEAD_DIM, SSM_STATE), lambda b, c: (b, 0, 0, 0)),
            pl.BlockSpec((1, CONV_WIDTH - 1, CONV_DIM), lambda b, c: (b, 0, 0)),
        ],
        out_shape=[
            jax.ShapeDtypeStruct(x.shape, F32),
            jax.ShapeDtypeStruct((nb, SSM_HEADS, SSM_HEAD_DIM, SSM_STATE), F32),
            jax.ShapeDtypeStruct((nb, CONV_WIDTH - 1, CONV_DIM), F32),
        ],
        scratch_shapes=[
            pltpu.VMEM((SSM_INNER, SSM_STATE), F32),
            pltpu.VMEM((CHUNK + 8, CONV_DIM), F32),
            pltpu.VMEM((CMLP_GROUPS, CHUNK, CHUNK), BF16),
            pltpu.VMEM((MIX_GROUP * CHUNK, PROJ_MAIN + LANES), F32),
        ],
        compiler_params=_params(("arbitrary", "arbitrary")),
        name="mixer_prompt",
    )(x, pre_g, mod, mod, mod, x, mod, mod, w_main, w_dt, *small, post_g, wa, wb, wo)


SAMPLE_TILE = 16


def _mixer_step_kernel(z_ref, u_ref, v_ref, xs_ref, bc_ref, dt_ref, ssm_in_ref, conv_in_ref,
                       cw_ref, cb_ref, dtb_ref, alog_ref, dskip_ref, ng_ref,
                       lng_ref, lnb_ref, wsp0_ref, bsp0_ref, ssm_all_ref,
                       ya_ref, yb_ref, vout_ref, ssm_ref, conv_ref):
    del ssm_all_ref
    tb = SAMPLE_TILE
    xbc_new = jnp.concatenate([xs_ref[...], bc_ref[...]], axis=1)
    hist = conv_in_ref[...]
    acc = xbc_new * cw_ref[CONV_WIDTH - 1:CONV_WIDTH, :]
    for k in range(CONV_WIDTH - 1):
        acc = acc + hist[:, k * CONV_DIM:(k + 1) * CONV_DIM] * cw_ref[k:k + 1, :]
    conv_ref[:, 0:(CONV_WIDTH - 2) * CONV_DIM] = hist[:, CONV_DIM:]
    conv_ref[:, (CONV_WIDTH - 2) * CONV_DIM:] = xbc_new

    xbc = jax.nn.silu(acc + cb_ref[...])
    xc = xbc[:, 0:SSM_INNER]
    bm = xbc[:, SSM_INNER:SSM_INNER + SSM_GROUPS * SSM_STATE]
    cm = xbc[:, SSM_INNER + SSM_GROUPS * SSM_STATE:CONV_DIM]
    dt = jax.nn.softplus(dt_ref[...] + dtb_ref[...])
    dec = jnp.exp(dt * (-jnp.exp(alog_ref[...])))

    def transposed(a):
        pad = jnp.zeros((LANES - tb, a.shape[1]), F32)
        return jnp.concatenate([a, pad], axis=0).T

    xc_t = transposed(xc)
    dt_t = transposed(dt)
    dec_t = transposed(dec)
    row_id = lax.broadcasted_iota(jnp.int32, (tb, SSM_STATE), 0)

    y_groups = [jnp.zeros((tb, GROUP_WIDTH), F32) for _ in range(SSM_GROUPS)]
    for b in range(tb):
        for g in range(SSM_GROUPS):
            b_row = bm[b:b + 1, g * SSM_STATE:(g + 1) * SSM_STATE]
            c_only = jnp.where(row_id == b, cm[:, g * SSM_STATE:(g + 1) * SSM_STATE], 0.0).astype(BF16)
            new_heads = []
            for r in range(HEADS_PER_GROUP):
                h = g * HEADS_PER_GROUP + r
                x_col = xc_t[h * SSM_HEAD_DIM:(h + 1) * SSM_HEAD_DIM, b:b + 1]
                push = x_col * dt_t[h:h + 1, b:b + 1]
                keep = jnp.broadcast_to(dec_t[h:h + 1, b:b + 1], (SSM_HEAD_DIM, SSM_STATE))
                h_new = ssm_in_ref[b, h] * keep + push * b_row
                ssm_ref[b, h] = h_new
                new_heads.append(h_new.astype(BF16))
            hg = jnp.concatenate(new_heads, axis=0)
            y_groups[g] = y_groups[g] + _dot_nt(c_only, hg)

    y = jnp.concatenate(y_groups, axis=1) + dskip_ref[...] * xc
    ya_ref[...] = _gated_group_norm(y, z_ref[...], ng_ref[...]).astype(BF16)

    ug = jax.nn.gelu(u_ref[...], approximate=True)
    vn = _layernorm(jax.nn.gelu(v_ref[...], approximate=True), lng_ref[...], lnb_ref[...])
    vout_ref[...] = vn
    yb_ref[...] = (ug * (vn * wsp0_ref[...] + bsp0_ref[...])).astype(BF16)


def _mixer_step_call(layer, proj, proj_dt, state_ssm, conv_hist, ssm_all, conv_w, conv_b, dt_bias, a_log,
                     d_skip, norm_g, ln_g, ln_b, w_sp0, b_sp0):
    nseq = proj.shape[0]
    tb = SAMPLE_TILE

    def col(block, width=D_MODEL):
        return pl.BlockSpec((tb, width), lambda i: (i, block))

    def whole(a):
        zeros = (0,) * a.ndim
        return pl.BlockSpec(a.shape, lambda i: zeros)

    small = (conv_w, conv_b, dt_bias, a_log, d_skip, norm_g, ln_g, ln_b, w_sp0, b_sp0)
    hist_width = (CONV_WIDTH - 1) * CONV_DIM
    return pl.pallas_call(
        _mixer_step_kernel,
        grid=(nseq // tb,),
        in_specs=[col(COL_Z), col(COL_U), col(COL_V), col(COL_XS),
                  col(COL_BC_512, 2 * SSM_GROUPS * SSM_STATE),
                  pl.BlockSpec((tb, LANES), lambda i: (i, 0)),
                  pl.BlockSpec((None, tb, SSM_HEADS, SSM_HEAD_DIM, SSM_STATE),
                               lambda i: (layer, i, 0, 0, 0)),
                  pl.BlockSpec((tb, hist_width), lambda i: (i, 0))]
                 + [whole(a) for a in small] + [pl.BlockSpec(memory_space=pl.ANY)],
        out_specs=[
            pl.BlockSpec((tb, SSM_INNER), lambda i: (i, 0)),
            pl.BlockSpec((tb, D_MODEL), lambda i: (i, 0)),
            pl.BlockSpec((tb, D_MODEL), lambda i: (i, 0)),
            pl.BlockSpec((None, tb, SSM_HEADS, SSM_HEAD_DIM, SSM_STATE), lambda i: (layer, i, 0, 0, 0)),
            pl.BlockSpec((tb, hist_width), lambda i: (i, 0)),
        ],
        out_shape=[
            jax.ShapeDtypeStruct((nseq, SSM_INNER), BF16),
            jax.ShapeDtypeStruct((nseq, D_MODEL), BF16),
            jax.ShapeDtypeStruct((nseq, D_MODEL), F32),
            jax.ShapeDtypeStruct(ssm_all.shape, F32),
            jax.ShapeDtypeStruct((nseq, hist_width), F32),
        ],
        input_output_aliases={8 + len(small): 3},
        compiler_params=_params(("arbitrary",)),
        name="mixer_sample",
    )(proj, proj, proj, proj, proj, proj_dt, state_ssm, conv_hist, *small, ssm_all)


def _outproj_kernel(ya_ref, yb_ref, ga_ref, gb_ref, x_ref, g1_ref, pg_ref, wa_ref, wb_ref, wo_ref, o_ref):
    o_ref[0] = _merge_project(ya_ref[0], yb_ref[0], ga_ref[0], gb_ref[0], x_ref[0], g1_ref[0], pg_ref[...],
                              wa_ref, wb_ref, wo_ref)


def _outproj_call(layer, ya, yb, proj, x, mod, post_g, wa, wb, wo, tm):
    nb, seq, _ = x.shape

    def rows(block=0):
        return pl.BlockSpec((1, tm, D_MODEL), lambda b, i: (b, i, block))

    def whole(a):
        return pl.BlockSpec(a.shape, lambda b, i: (0, 0))

    def of_layer(a):
        return pl.BlockSpec((None,) + a.shape[1:], lambda b, i: (layer, 0, 0))

    return pl.pallas_call(
        _outproj_kernel,
        grid=(nb, seq // tm),
        in_specs=[rows(), rows(), rows(COL_GA), rows(COL_GB), rows(),
                  _mod_spec(mod, tm, MOD_G1, 2), whole(post_g), of_layer(wa), of_layer(wb), of_layer(wo)],
        out_specs=rows(),
        out_shape=jax.ShapeDtypeStruct(x.shape, F32),
        compiler_params=_params(("arbitrary", "arbitrary")),
        name="out_proj",
    )(ya, yb, proj, proj, x, mod, post_g, wa, wb, wo)


FFN_CHUNK = 256


def _swiglu_into(hb, wg_ref, wu_ref, wd_ref, acc_ref):
    n_chunks = FFN_DIM // FFN_CHUNK

    def gate_up(f):
        cols = slice(f * FFN_CHUNK, (f + 1) * FFN_CHUNK)
        return _dot(hb, wg_ref[0, :, cols]), _dot(hb, wu_ref[0, :, cols])

    g, u = gate_up(0)
    for f in range(n_chunks):
        ahead = gate_up(f + 1) if f + 1 < n_chunks else None
        act = (jax.nn.silu(g) * u).astype(BF16)
        part = _dot(act, wd_ref[0, f * FFN_CHUNK:(f + 1) * FFN_CHUNK, :])
        if f == 0:
            acc_ref[...] = part
        else:
            acc_ref[...] += part
        if ahead is not None:
            g, u = ahead


def _premod(x_ref, pre_ref, sh_ref, sc_ref):
    return _rms(x_ref[0]) * pre_ref[...] * (1.0 + sc_ref[0]) + sh_ref[0]


def _dense_ffn_kernel(x_ref, pre_ref, post_ref, sh_ref, sc_ref, g2_ref, wg_ref, wu_ref, wd_ref,
                      o_ref, acc_scr):
    hb = _premod(x_ref, pre_ref, sh_ref, sc_ref).astype(BF16)
    _swiglu_into(hb, wg_ref, wu_ref, wd_ref, acc_scr)
    o_ref[0] = x_ref[0] + g2_ref[0] * (_rms(acc_scr[...]) * post_ref[...])


def _dense_ffn_call(j, x, mod, pre_g, post_g, wg, wu, wd, tm):
    nb, seq, _ = x.shape

    def whole(a):
        zeros = (0,) * a.ndim
        return pl.BlockSpec(a.shape, lambda b, i: zeros)

    def of_layer(a):
        return pl.BlockSpec((1,) + a.shape[1:], lambda b, i: (j, 0, 0))

    rows = pl.BlockSpec((1, tm, D_MODEL), lambda b, i: (b, i, 0))
    return pl.pallas_call(
        _dense_ffn_kernel,
        grid=(nb, seq // tm),
        in_specs=[rows, whole(pre_g), whole(post_g),
                  _mod_spec(mod, tm, MOD_SH2, 2), _mod_spec(mod, tm, MOD_SC2, 2),
                  _mod_spec(mod, tm, MOD_G2, 2), of_layer(wg), of_layer(wu), of_layer(wd)],
        out_specs=rows,
        out_shape=jax.ShapeDtypeStruct(x.shape, F32),
        scratch_shapes=[pltpu.VMEM((tm, D_MODEL), F32)],
        compiler_params=_params(("arbitrary", "arbitrary")),
        name="dense_ffn",
    )(x, pre_g, post_g, mod, mod, mod, wg, wu, wd)


EXPERT_TILE = 512
HALF = D_MODEL // 2
INFO_I1, INFO_I2, INFO_P1, INFO_P2, INFO_R1, INFO_R2 = range(6)
HI_MASK = 0xFFFF0000
ISSUE_UNROLL = 8


def _route_kernel(x_ref, pre_ref, sh_ref, sc_ref, rw_ref, rb_ref, cnt0_ref,
                  hw_ref, info_ref, cnt_ref, carry):
    @pl.when(jnp.logical_and(pl.program_id(0) == 0, pl.program_id(1) == 0))
    def _():
        carry[...] = cnt0_ref[...]

    hb = _premod(x_ref, pre_ref, sh_ref, sc_ref).astype(BF16)
    tm = hb.shape[0]
    bits = lax.bitcast_convert_type(hb.astype(F32), jnp.uint32)
    hw_ref[0] = (bits[:, HALF:] & jnp.uint32(HI_MASK)) | (bits[:, :HALF] >> jnp.uint32(16))

    logits = _dot(hb, rw_ref[...].astype(BF16)) + rb_ref[...]
    lane = lax.broadcasted_iota(jnp.int32, logits.shape, 1).astype(F32)
    neg = jnp.float32(-jnp.inf)
    logits = jnp.where(lane < N_EXPERTS, logits, neg)
    m1 = jnp.max(logits, axis=-1, keepdims=True)
    i1 = jnp.min(jnp.where(logits == m1, lane, float(LANES)), axis=-1, keepdims=True)
    rest = jnp.where(lane == i1, neg, logits)
    m2 = jnp.max(rest, axis=-1, keepdims=True)
    i2 = jnp.min(jnp.where(rest == m2, lane, float(LANES)), axis=-1, keepdims=True)
    e2 = jnp.exp(m2 - m1)
    p1 = 1.0 / (1.0 + e2)
    p2 = e2 * p1

    member = jnp.logical_or(lane == i1, lane == i2)
    row = lax.broadcasted_iota(jnp.int32, (tm, tm), 0)
    col = lax.broadcasted_iota(jnp.int32, (tm, tm), 1)
    before = jnp.where(row > col, 1.0, 0.0).astype(BF16)
    ones = jnp.where(member, 1.0, 0.0)
    prior = _dot(before, ones.astype(BF16)) + carry[...]
    r1 = jnp.sum(jnp.where(lane == i1, prior, 0.0), axis=-1, keepdims=True)
    r2 = jnp.sum(jnp.where(lane == i2, prior, 0.0), axis=-1, keepdims=True)
    carry[...] += jnp.sum(ones, axis=0, keepdims=True)
    cnt_ref[...] = carry[...]

    info = jnp.zeros_like(logits)
    for k, v in ((INFO_I1, i1), (INFO_I2, i2), (INFO_P1, p1), (INFO_P2, p2), (INFO_R1, r1), (INFO_R2, r2)):
        info = jnp.where(lane == float(k), v, info)
    info_ref[0] = info


def _route_call(x, mod, pre_g, rw, rb, cnt0, tm):
    nb, seq, _ = x.shape

    def whole(a):
        return pl.BlockSpec(a.shape, lambda b, i: (0, 0))

    return pl.pallas_call(
        _route_kernel,
        grid=(nb, seq // tm),
        in_specs=[pl.BlockSpec((1, tm, D_MODEL), lambda b, i: (b, i, 0)), whole(pre_g),
                  _mod_spec(mod, tm, MOD_SH2, 2), _mod_spec(mod, tm, MOD_SC2, 2),
                  whole(rw), whole(rb), whole(cnt0)],
        out_specs=[pl.BlockSpec((1, tm, HALF), lambda b, i: (b, i, 0)),
                   pl.BlockSpec((1, tm, LANES), lambda b, i: (b, i, 0)),
                   pl.BlockSpec((1, LANES), lambda b, i: (0, 0))],
        out_shape=[jax.ShapeDtypeStruct((nb, seq, HALF), jnp.uint32),
                   jax.ShapeDtypeStruct((nb, seq, LANES), F32),
                   jax.ShapeDtypeStruct((1, LANES), F32)],
        scratch_shapes=[pltpu.VMEM((1, LANES), F32)],
        compiler_params=_params(("arbitrary", "arbitrary")),
        name="moe_route",
    )(x, pre_g, mod, mod, rw, rb, cnt0)


def _row_copy(src, src_row, dst, dst_row, sem):
    return pltpu.make_async_copy(src.at[pl.ds(src_row, 1)], dst.at[pl.ds(dst_row, 1)], sem)


def _dispatch_kernel(d1_ref, d2_ref, hw_ref, xs_in_ref, xs_ref, sem):
    del xs_in_ref
    tm = hw_ref.shape[0]
    base = pl.program_id(0) * tm

    def issue(r, carry):
        _row_copy(hw_ref, r, xs_ref, d1_ref[base + r], sem).start(priority=0)
        _row_copy(hw_ref, r, xs_ref, d2_ref[base + r], sem).start(priority=1)
        return carry

    lax.fori_loop(0, tm, issue, 0, unroll=ISSUE_UNROLL)
    for _ in range(2):
        pltpu.make_async_copy(hw_ref, xs_ref.at[pl.ds(0, tm)], sem).wait()


def _dispatch_call(dest1, dest2, hw, xs, tm):
    rows = hw.shape[0]
    return pl.pallas_call(
        _dispatch_kernel,
        grid_spec=pltpu.PrefetchScalarGridSpec(
            num_scalar_prefetch=2,
            grid=(rows // tm,),
            in_specs=[pl.BlockSpec((tm, HALF), lambda i, d1, d2: (i, 0)),
                      pl.BlockSpec(memory_space=pl.ANY)],
            out_specs=pl.BlockSpec(memory_space=pl.ANY),
            scratch_shapes=[pltpu.SemaphoreType.DMA(())],
        ),
        out_shape=jax.ShapeDtypeStruct(xs.shape, xs.dtype),
        input_output_aliases={3: 0},
        compiler_params=_params(("arbitrary",), row_dma=True),
        name="moe_dispatch",
    )(dest1, dest2, hw, xs)


def _experts_kernel(exp_ref, nv_ref, xs_ref, wg_ref, wu_ref, wd_ref, y_ref):
    del exp_ref
    used = pl.program_id(0) < nv_ref[0]

    @pl.when(used)
    def _():
        words = xs_ref[...]
        lo = lax.bitcast_convert_type(words << jnp.uint32(16), F32)
        hi = lax.bitcast_convert_type(words & jnp.uint32(HI_MASK), F32)
        hb = jnp.concatenate([lo, hi], axis=1).astype(BF16)
        _swiglu_into(hb, wg_ref, wu_ref, wd_ref, y_ref)

    @pl.when(jnp.logical_not(used))
    def _():
        y_ref[...] = jnp.zeros_like(y_ref)


def _experts_call(j, tile_exp, n_valid, xs, wg, wu, wd):
    n_tiles = tile_exp.shape[0]
    tm = EXPERT_TILE
    weight = lambda shape: pl.BlockSpec((None, 1) + shape, lambda i, ex, nv: (j, ex[i], 0, 0))
    return pl.pallas_call(
        _experts_kernel,
        grid_spec=pltpu.PrefetchScalarGridSpec(
            num_scalar_prefetch=2,
            grid=(n_tiles,),
            in_specs=[pl.BlockSpec((tm, HALF), lambda i, ex, nv: (i, 0)),
                      weight((D_MODEL, FFN_DIM)), weight((D_MODEL, FFN_DIM)), weight((FFN_DIM, D_MODEL))],
            out_specs=pl.BlockSpec((tm, D_MODEL), lambda i, ex, nv: (i, 0)),
        ),
        out_shape=jax.ShapeDtypeStruct((xs.shape[0], D_MODEL), F32),
        compiler_params=_params(("arbitrary",)),
        name="moe_experts",
    )(tile_exp, n_valid, xs, wg, wu, wd)


def _combine_kernel(d1_ref, d2_ref, y_ref, info_ref, x_ref, g2_ref, post_ref, o_ref, ybuf, sems):
    tm = x_ref.shape[1]
    step = pl.program_id(0) * pl.num_programs(1) + pl.program_id(1)
    n_steps = pl.num_programs(0) * pl.num_programs(1)

    def gather(tile, slot):
        base = tile * tm

        def issue(r, carry):
            _row_copy(y_ref, d1_ref[base + r], ybuf.at[slot, 0], r, sems.at[slot]).start(priority=0)
            _row_copy(y_ref, d2_ref[base + r], ybuf.at[slot, 1], r, sems.at[slot]).start(priority=1)
            return carry

        lax.fori_loop(0, tm, issue, 0, unroll=ISSUE_UNROLL)

    slot = step % 2

    @pl.when(step == 0)
    def _():
        gather(step, slot)

    @pl.when(step + 1 < n_steps)
    def _():
        gather(step + 1, 1 - slot)

    for k in range(2):
        pltpu.make_async_copy(y_ref.at[pl.ds(0, tm)], ybuf.at[slot, k], sems.at[slot]).wait()

    info = info_ref[0]
    f = info[:, INFO_P1:INFO_P1 + 1] * ybuf[slot, 0] + info[:, INFO_P2:INFO_P2 + 1] * ybuf[slot, 1]
    o_ref[0] = x_ref[0] + g2_ref[0] * (_rms(f) * post_ref[...])


def _combine_call(dest1, dest2, y, info, x, mod, post_g, tm):
    nb, seq, _ = x.shape
    per_row = mod.shape[1] != 1
    return pl.pallas_call(
        _combine_kernel,
        grid_spec=pltpu.PrefetchScalarGridSpec(
            num_scalar_prefetch=2,
            grid=(nb, seq // tm),
            in_specs=[pl.BlockSpec(memory_space=pl.ANY),
                      pl.BlockSpec((1, tm, LANES), lambda b, i, d1, d2: (b, i, 0)),
                      pl.BlockSpec((1, tm, D_MODEL), lambda b, i, d1, d2: (b, i, 0)),
                      pl.BlockSpec((1, tm if per_row else 1, D_MODEL),
                                   lambda b, i, d1, d2: (b, i if per_row else 0, MOD_G2)),
                      pl.BlockSpec(post_g.shape, lambda b, i, d1, d2: (0, 0))],
            out_specs=pl.BlockSpec((1, tm, D_MODEL), lambda b, i, d1, d2: (b, i, 0)),
            scratch_shapes=[pltpu.VMEM((2, 2, tm, D_MODEL), F32), pltpu.SemaphoreType.DMA((2,))],
        ),
        out_shape=jax.ShapeDtypeStruct(x.shape, F32),
        compiler_params=_params(("arbitrary", "arbitrary"), row_dma=True),
        name="moe_combine",
    )(dest1, dest2, y, info, x, mod, post_g)


def _moe_layer(j, xp, xs, mod_p, mod_s, pre_g, post_g, router_w, router_b, wg, wu, wd):
    rw = jnp.pad(router_w, ((0, 0), (0, LANES - N_EXPERTS)))
    rb = _pad_lanes(router_b)
    hw_p, info_p, cnt_p = _route_call(xp, mod_p, pre_g, rw, rb, jnp.zeros((1, LANES), F32), tm=512)
    hw_s, info_s, cnt = _route_call(xs, mod_s, pre_g, rw, rb, cnt_p, tm=xs.shape[1])

    tm = EXPERT_TILE
    n_assign = 2 * (xp.shape[0] * xp.shape[1] + xs.shape[1])
    n_tiles = n_assign // tm + N_EXPERTS
    counts = cnt[0, :N_EXPERTS].astype(jnp.int32)
    group_tiles = (counts + tm - 1) // tm
    tile_end = jnp.cumsum(group_tiles)
    start = (tile_end - group_tiles) * tm
    n_valid = tile_end[-1]
    tile_id = jnp.minimum(jnp.arange(n_tiles, dtype=jnp.int32), n_valid - 1)
    tile_exp = jnp.sum(tile_id[:, None] >= tile_end[None, :], axis=1).astype(jnp.int32)

    def dests(info):
        flat = info.reshape(-1, LANES)
        d = [start[flat[:, i].astype(jnp.int32)] + flat[:, r].astype(jnp.int32)
             for i, r in ((INFO_I1, INFO_R1), (INFO_I2, INFO_R2))]
        return d[0], d[1]

    dp = dests(info_p)
    ds = dests(info_s)
    slots = jnp.zeros((n_tiles * tm, HALF), jnp.uint32)
    slots = _dispatch_call(dp[0], dp[1], hw_p.reshape(-1, HALF), slots, tm=1024)
    slots = _dispatch_call(ds[0], ds[1], hw_s.reshape(-1, HALF), slots, tm=xs.shape[1])
    y = _experts_call(j, tile_exp, n_valid.reshape(1), slots, wg, wu, wd)
    xp = _combine_call(dp[0], dp[1], y, info_p, xp, mod_p, post_g, tm=512)
    xs = _combine_call(ds[0], ds[1], y, info_s, xs, mod_s, post_g, tm=xs.shape[1])
    return xp, xs


def _pad_lanes(v):
    return jnp.pad(v, (0, LANES - v.shape[0])).reshape(1, LANES)


def kernel(x_prompt, x_sample, state_ssm, state_conv, c_prompt, c_sample, w_mod, b_mod, mix_pre_g, mix_post_g, ffn_pre_g, ffn_post_g, w_in, conv_w, conv_b, dt_bias, a_log, d_skip, ssm_norm_g, w_ssd_out, cmlp_ln_g, cmlp_ln_b, w_spatial, b_spatial, w_cmlp_out, w_o, ffn_wg, ffn_wu, ffn_wd, router_w, router_b, exp_wg, exp_wu, exp_wd):
    n_prompt = x_prompt.shape[0]
    n_sample = x_sample.shape[0]

    c_all = jnp.concatenate([c_prompt, c_sample, jnp.zeros((8, D_MODEL), F32)], axis=0)
    mod_all = _mod_call(c_all, w_mod, b_mod)
    mod_p = mod_all[:, :n_prompt].reshape(DEPTH, n_prompt, 1, 6 * D_MODEL)
    mod_s = mod_all[:, n_prompt:n_prompt + n_sample].reshape(DEPTH, 1, n_sample, 6 * D_MODEL)

    xp = x_prompt
    xs = x_sample.reshape(1, n_sample, D_MODEL)
    conv_hist = state_conv.reshape(DEPTH, n_sample, (CONV_WIDTH - 1) * CONV_DIM)

    o_xbc = SSM_INNER
    o_dt = o_xbc + CONV_DIM
    o_u = o_dt + SSM_HEADS

    w_main = jnp.concatenate([w_in[:, :, :o_xbc], w_in[:, :, o_u:], w_in[:, :, o_xbc:o_dt]], axis=2).astype(BF16)
    w_dt = jnp.pad(w_in[:, :, o_dt:o_u], ((0, 0), (0, 0), (0, LANES - SSM_HEADS))).astype(BF16)
    wa, wb, wo = (a.astype(BF16) for a in (w_ssd_out, w_cmlp_out, w_o))
    dense_w = tuple(a.astype(BF16) for a in (ffn_wg, ffn_wu, ffn_wd))
    expert_w = tuple(a.astype(BF16) for a in (exp_wg, exp_wu, exp_wd))

    ssm_s = jnp.zeros(state_ssm.shape, F32)
    ssm_p, conv_p, conv_s, v_s = [], [], [], []
    for l in range(DEPTH):
        row = lambda a: a[l].reshape(1, -1)
        mixer_small = (conv_w[l], row(conv_b), _pad_lanes(dt_bias[l]), _pad_lanes(a_log[l]),
                       jnp.repeat(d_skip[l], SSM_HEAD_DIM).reshape(1, -1), row(ssm_norm_g),
                       row(cmlp_ln_g), row(cmlp_ln_b))

        proj_s, dt_s = _inproj_call(l, xs, mod_s[l], row(mix_pre_g), w_main, w_dt, tm=n_sample)
        xp, hs_p, cs_p = _mixer_call(xp, mod_p[l], row(mix_pre_g), row(mix_post_g), w_main[l], w_dt[l],
                                     wa[l], wb[l], wo[l], *mixer_small, w_spatial[l], b_spatial[l].T)
        ya_s, yb_s, vr_s, ssm_s, cs_s = _mixer_step_call(
            l, proj_s[0], dt_s[0], state_ssm, conv_hist[l], ssm_s, *mixer_small,
            jnp.repeat(w_spatial[l, :, 0, 0], CMLP_GROUP_DIM).reshape(1, -1),
            jnp.repeat(b_spatial[l, :, 0], CMLP_GROUP_DIM).reshape(1, -1))
        xs = _outproj_call(l, ya_s[None], yb_s[None], proj_s, xs, mod_s[l], row(mix_post_g), wa, wb, wo,
                           tm=n_sample)

        j = l // 2
        if l % 2 == 0:
            xp = _dense_ffn_call(j, xp, mod_p[l], row(ffn_pre_g), row(ffn_post_g), *dense_w, tm=512)
            xs = _dense_ffn_call(j, xs, mod_s[l], row(ffn_pre_g), row(ffn_post_g), *dense_w, tm=n_sample)
        else:
            xp, xs = _moe_layer(j, xp, xs, mod_p[l], mod_s[l], row(ffn_pre_g), row(ffn_post_g),
                                router_w[j], router_b[j], *expert_w)

        ssm_p.append(hs_p)
        conv_p.append(cs_p)
        conv_s.append(cs_s.reshape(n_sample, CONV_WIDTH - 1, CONV_DIM))
        v_s.append(vr_s.reshape(n_sample, 1, D_MODEL))

    return (xp, xs.reshape(n_sample, 1, D_MODEL), jnp.stack(ssm_p), jnp.stack(conv_p),
            ssm_s, jnp.stack(conv_s), jnp.stack(v_s))
```

```python
import functools

import jax
import jax.numpy as jnp
from jax import lax
from jax.experimental import pallas as pl
from jax.experimental.pallas import tpu as pltpu

F32 = jnp.float32
BF16 = jnp.bfloat16

D_MODEL = 1024
DEPTH = 4
SSM_HEADS = 16
SSM_HEAD_DIM = 64
SSM_GROUPS = 2
SSM_STATE = 128
SSM_INNER = 1024
GROUP_WIDTH = SSM_INNER // SSM_GROUPS
HEADS_PER_GROUP = SSM_HEADS // SSM_GROUPS
CONV_WIDTH = 4
CONV_DIM = 1536
CHUNK = 128
CMLP_GROUPS = 8
CMLP_GROUP_DIM = 128
FFN_DIM = 2816
N_EXPERTS = 8
EPS = 1e-6
LANES = 128
COL_Z, COL_U, COL_V, COL_GA, COL_GB, COL_XS = 0, 1, 2, 3, 4, 5
COL_BC_512 = 12
PROJ_MAIN = 6 * D_MODEL + 2 * SSM_GROUPS * SSM_STATE
PROJ_TILE = 1664
MOD_SH1, MOD_SC1, MOD_G1, MOD_SH2, MOD_SC2, MOD_G2 = range(6)
VMEM_LIMIT = 56 * 1024 * 1024


def _params(semantics, row_dma=False):
    return pltpu.CompilerParams(dimension_semantics=semantics, vmem_limit_bytes=VMEM_LIMIT,
                                disable_bounds_checks=row_dma)


def _rms(x):
    return x * lax.rsqrt(jnp.mean(x * x, axis=-1, keepdims=True) + EPS)


def _dot(a, b):
    return jnp.dot(a, b, preferred_element_type=F32)


def _dot_nt(a, b):
    return lax.dot_general(a, b, (((1,), (1,)), ((), ())), preferred_element_type=F32)


def _dot_tn(a, b):
    return lax.dot_general(a, b, (((0,), (0,)), ((), ())), preferred_element_type=F32)


def _mod_kernel(c_ref, w_ref, b_ref, o_ref):
    a = jax.nn.silu(c_ref[...]).astype(BF16)
    o_ref[0] = _dot(a, w_ref[0].astype(BF16)) + b_ref[0]


def _mod_call(c_all, w_mod, b_mod):
    rows = c_all.shape[0]
    width = 2 * D_MODEL
    return pl.pallas_call(
        _mod_kernel,
        grid=(DEPTH, 6 * D_MODEL // width),
        in_specs=[
            pl.BlockSpec((rows, D_MODEL), lambda l, j: (0, 0)),
            pl.BlockSpec((1, D_MODEL, width), lambda l, j: (l, 0, j)),
            pl.BlockSpec((1, 1, width), lambda l, j: (l, 0, j)),
        ],
        out_specs=pl.BlockSpec((1, rows, width), lambda l, j: (l, 0, j)),
        out_shape=jax.ShapeDtypeStruct((DEPTH, rows, 6 * D_MODEL), F32),
        compiler_params=_params(("arbitrary", "arbitrary")),
        name="adaln_mod",
    )(c_all, w_mod, b_mod.reshape(DEPTH, 1, 6 * D_MODEL))


def _mod_spec(mod, tm, seg, grid_rank):
    per_row = mod.shape[1] != 1
    rows = tm if per_row else 1
    if grid_rank == 2:
        return pl.BlockSpec((1, rows, D_MODEL), lambda b, i: (b, i if per_row else 0, seg))
    return pl.BlockSpec((1, rows, D_MODEL), lambda b, i, j: (b, i if per_row else 0, seg))


def _inproj_kernel(x_ref, g_ref, sh_ref, sc_ref, w_ref, wdt_ref, p_ref, pdt_ref, h_scr):
    @pl.when(pl.program_id(2) == 0)
    def _():
        h = _rms(x_ref[0]) * g_ref[...] * (1.0 + sc_ref[0]) + sh_ref[0]
        hb = h.astype(BF16)
        h_scr[...] = hb
        pdt_ref[0] = _dot(hb, wdt_ref[...])

    p_ref[0] = _dot(h_scr[...], w_ref[...])


def _inproj_call(layer, x, mod, g, w_main, w_dt, tm):
    nb, seq, _ = x.shape
    grid = (nb, seq // tm, PROJ_MAIN // PROJ_TILE)
    return pl.pallas_call(
        _inproj_kernel,
        grid=grid,
        in_specs=[
            pl.BlockSpec((1, tm, D_MODEL), lambda b, i, j: (b, i, 0)),
            pl.BlockSpec((1, D_MODEL), lambda b, i, j: (0, 0)),
            _mod_spec(mod, tm, MOD_SH1, 3),
            _mod_spec(mod, tm, MOD_SC1, 3),
            pl.BlockSpec((None, D_MODEL, PROJ_TILE), lambda b, i, j: (layer, 0, j)),
            pl.BlockSpec((None, D_MODEL, LANES), lambda b, i, j: (layer, 0, 0)),
        ],
        out_specs=[
            pl.BlockSpec((1, tm, PROJ_TILE), lambda b, i, j: (b, i, j)),
            pl.BlockSpec((1, tm, LANES), lambda b, i, j: (b, i, 0)),
        ],
        out_shape=[
            jax.ShapeDtypeStruct((nb, seq, PROJ_MAIN), F32),
            jax.ShapeDtypeStruct((nb, seq, LANES), F32),
        ],
        scratch_shapes=[pltpu.VMEM((tm, D_MODEL), BF16)],
        compiler_params=_params(("arbitrary", "arbitrary", "arbitrary")),
        name="in_proj",
    )(x, g, mod, mod, w_main, w_dt)


def _gated_group_norm(y, z, norm_g):
    y = y * jax.nn.silu(z)
    parts = [_rms(y[:, g * GROUP_WIDTH:(g + 1) * GROUP_WIDTH]) for g in range(SSM_GROUPS)]
    return jnp.concatenate(parts, axis=-1) * norm_g


def _layernorm(x, g, b):
    mu = jnp.mean(x, axis=-1, keepdims=True)
    xc = x - mu
    return xc * lax.rsqrt(jnp.mean(xc * xc, axis=-1, keepdims=True) + EPS) * g + b


def _pair_columns(v, pair, lane_lo):
    h0 = 2 * pair
    return jnp.where(lane_lo, v[:, h0:h0 + 1], v[:, h0 + 1:h0 + 2])


MIX_CHUNKS = 8
MIX_GROUP = 1
P0_COLS = {"z": (COL_Z * D_MODEL, D_MODEL), "u": (COL_U * D_MODEL, D_MODEL), "v": (COL_V * D_MODEL, D_MODEL),
           "ga": (COL_GA * D_MODEL, D_MODEL), "gb": (COL_GB * D_MODEL, D_MODEL),
           "xs": (COL_XS * D_MODEL, D_MODEL), "bc": (COL_BC_512 * 512, 2 * SSM_GROUPS * SSM_STATE),
           "dt": (PROJ_MAIN, LANES)}


def _mixer_chunk(z, u, v, xs, bc, dt_raw,
                 cw_ref, cb_ref, dtb_ref, alog_ref, dskip_ref, ng_ref, lng_ref, lnb_ref, bsp_ref,
                 h_scr, cbuf, wsp_scr, causal, lane_lo, fill):
    cbuf[8:8 + CHUNK, 0:SSM_INNER] = xs
    cbuf[8:8 + CHUNK, SSM_INNER:CONV_DIM] = bc
    window = cbuf[...]
    acc = window * cw_ref[0:1, :]
    for k in range(1, CONV_WIDTH):
        acc = pltpu.roll(acc, 1, 0) + window * cw_ref[k:k + 1, :]
    acc = acc[8:8 + CHUNK, :]
    cbuf[0:8, :] = cbuf[CHUNK:8 + CHUNK, :]
    fill()

    xbc = jax.nn.silu(acc + cb_ref[...])
    xc = xbc[:, 0:SSM_INNER]
    bm = xbc[:, SSM_INNER:SSM_INNER + SSM_GROUPS * SSM_STATE].astype(BF16)
    cm = xbc[:, SSM_INNER + SSM_GROUPS * SSM_STATE:CONV_DIM].astype(BF16)
    fill()

    dt = jax.nn.softplus(dt_raw + dtb_ref[...])
    da = dt * (-jnp.exp(alog_ref[...]))
    cum = jnp.dot(causal.astype(F32), da, preferred_element_type=F32,
                  precision=lax.Precision.HIGHEST)
    last = cum[CHUNK - 1:CHUNK, :]
    ecum = jnp.exp(cum)
    wend = jnp.exp(last - cum) * dt
    elast = jnp.exp(last)
    cum_t = cum.T
    dt_t = dt.T
    fill()

    y_pairs = []
    for g in range(SSM_GROUPS):
        bg = bm[:, g * SSM_STATE:(g + 1) * SSM_STATE]
        cg = cm[:, g * SSM_STATE:(g + 1) * SSM_STATE]
        cb = _dot_nt(cg, bg)
        hg = h_scr[g * GROUP_WIDTH:(g + 1) * GROUP_WIDTH, :]
        y_state = _dot_nt(cg, hg.astype(BF16))
        xw_parts = []
        for q in range(HEADS_PER_GROUP // 2):
            pair = g * (HEADS_PER_GROUP // 2) + q
            mixes = []
            for h in (2 * pair, 2 * pair + 1):
                seg = cum[:, h:h + 1] - cum_t[h:h + 1, :]
                decay = jnp.where(causal, jnp.exp(seg), 0.0)
                mixes.append((cb * decay * dt_t[h:h + 1, :]).astype(BF16))
            xp = xc[:, pair * LANES:(pair + 1) * LANES]
            rhs = jnp.concatenate([jnp.where(lane_lo, xp, 0.0), jnp.where(lane_lo, 0.0, xp)],
                                  axis=0).astype(BF16)
            y_in = _dot(jnp.concatenate(mixes, axis=1), rhs)
            y_st = y_state[:, q * LANES:(q + 1) * LANES] * _pair_columns(ecum, pair, lane_lo)
            y_pairs.append(y_in + y_st)
            xw_parts.append((xp * _pair_columns(wend, pair, lane_lo)).astype(BF16))
            fill()
        upd = _dot_tn(jnp.concatenate(xw_parts, axis=1), bg)
        for r in range(HEADS_PER_GROUP):
            h = g * HEADS_PER_GROUP + r
            head = slice(h * SSM_HEAD_DIM, (h + 1) * SSM_HEAD_DIM)
            scale = jnp.broadcast_to(elast[0:1, h:h + 1], (SSM_HEAD_DIM, SSM_STATE))
            h_scr[head, :] = h_scr[head, :] * scale + upd[r * SSM_HEAD_DIM:(r + 1) * SSM_HEAD_DIM, :]

    y = jnp.concatenate(y_pairs, axis=1) + dskip_ref[...] * xc
    ya = _gated_group_norm(y, z, ng_ref[...]).astype(BF16)
    fill()

    ug = jax.nn.gelu(u, approximate=True)
    vn = _layernorm(jax.nn.gelu(v, approximate=True), lng_ref[...], lnb_ref[...])
    fill()
    gates = []
    for g in range(CMLP_GROUPS):
        vg = vn[:, g * CMLP_GROUP_DIM:(g + 1) * CMLP_GROUP_DIM].astype(BF16)
        gates.append(_dot(wsp_scr[g], vg) + bsp_ref[:, g:g + 1])
    yb = (ug * jnp.concatenate(gates, axis=1)).astype(BF16)
    return ya, yb


def _merge_project(ya, yb, ga, gb, x, g1, post_g, wa_ref, wb_ref, wo_ref):
    merged = (jax.nn.sigmoid(ga) * _dot(ya, wa_ref[...]) + jax.nn.sigmoid(gb) * _dot(yb, wb_ref[...]))
    o = _dot(merged.astype(BF16), wo_ref[...])
    return x + g1 * (_rms(o) * post_g)


def _mixer_kernel(x_ref, pre_ref, sh_ref, sc_ref, g1_ref, xn_ref, shn_ref, scn_ref, win_ref, wdt_ref,
                  cw_ref, cb_ref, dtb_ref, alog_ref, dskip_ref, ng_ref, lng_ref, lnb_ref, wsp_ref, bsp_ref,
                  pg_ref, wa_ref, wb_ref, wo_ref,
                  o_ref, ssm_ref, conv_ref,
                  h_scr, cbuf, wsp_scr, p0_scr):
    c = pl.program_id(1)
    row = lax.broadcasted_iota(jnp.int32, (CHUNK, CHUNK), 0)
    col = lax.broadcasted_iota(jnp.int32, (CHUNK, CHUNK), 1)
    causal = row >= col
    lane_lo = col < SSM_HEAD_DIM

    @pl.when(c == 0)
    def _():
        h_scr[...] = jnp.zeros_like(h_scr)
        cbuf[0:8, :] = jnp.zeros((8, CONV_DIM), F32)
        for g in range(CMLP_GROUPS):
            wsp_scr[g] = jnp.where(causal, wsp_ref[g], 0.0).astype(BF16)

    def project(hb, block, width=D_MODEL):
        return _dot(hb, win_ref[:, block * D_MODEL:block * D_MODEL + width])

    def projection_pieces(load_x, load_sh, load_sc, store):
        t = {}

        def prep():
            x = load_x()
            store("x", x)
            t["hb"] = (_rms(x) * pre_ref[...] * (1.0 + load_sc()) + load_sh()).astype(BF16)

        pieces = [prep]
        for name, block in (("z", COL_Z), ("u", COL_U), ("v", COL_V), ("xs", COL_XS), ("bc", COL_XS + 1),
                            ("ga", COL_GA), ("gb", COL_GB)):
            width = P0_COLS[name][1]
            pieces.append(lambda name=name, block=block, width=width:
                          store(name, project(t["hb"], block, width)))
        pieces.append(lambda: store("dt", _dot(t["hb"], wdt_ref[...])))
        return pieces

    group_rows = MIX_GROUP * CHUNK

    def store_first(name, value):
        if name != "x":
            start, width = P0_COLS[name]
            p0_scr[:, start:start + width] = value

    def out_projection(rows, p, ya, yb):
        o_ref[0, rows, :] = _merge_project(ya, yb, p["ga"], p["gb"], x_ref[0, rows, :], g1_ref[0], pg_ref[...],
                                           wa_ref, wb_ref, wo_ref)

    @pl.when(jnp.logical_and(pl.program_id(0) == 0, c == 0))
    def _():
        for piece in projection_pieces(lambda: x_ref[0, 0:group_rows, :], lambda: sh_ref[0], lambda: sc_ref[0],
                                       store_first):
            piece()

    class _FromScratch:
        def __getitem__(self, name):
            start, width = P0_COLS[name]
            return p0_scr.at[:, start:start + width]

    cur = _FromScratch()
    pending = []
    n_groups = MIX_CHUNKS // MIX_GROUP
    for j in range(n_groups):
        rows = slice(j * group_rows, (j + 1) * group_rows)
        if j + 1 < n_groups:
            nxt = {}
            nrows = slice((j + 1) * group_rows, (j + 2) * group_rows)
            more = projection_pieces(lambda nrows=nrows: x_ref[0, nrows, :], lambda: sh_ref[0],
                                     lambda: sc_ref[0], nxt.__setitem__)
        else:
            more = projection_pieces(lambda: xn_ref[0], lambda: shn_ref[0], lambda: scn_ref[0], store_first)
        pending = pending + more

        def fill():
            if pending:
                pending.pop(0)()

        yas, ybs = [], []
        for i in range(MIX_GROUP):
            sub = slice(i * CHUNK, (i + 1) * CHUNK)
            ya, yb = _mixer_chunk(*(cur[name][sub, :] for name in ("z", "u", "v", "xs", "bc", "dt")),
                                  cw_ref, cb_ref, dtb_ref, alog_ref, dskip_ref, ng_ref, lng_ref, lnb_ref,
                                  bsp_ref, h_scr, cbuf, wsp_scr, causal, lane_lo, fill)
            yas.append(ya)
            ybs.append(yb)
        while pending:
            fill()
        gates = {name: cur[name][...] for name in ("ga", "gb")}
        pending = [functools.partial(out_projection, rows, gates, jnp.concatenate(yas, axis=0),
                                     jnp.concatenate(ybs, axis=0))]
        if j + 1 < n_groups:
            cur = nxt
    pending.pop(0)()

    @pl.when(c == pl.num_programs(1) - 1)
    def _():
        conv_ref[0] = cbuf[5:8, :]
        ssm_ref[0] = h_scr[...].reshape(SSM_HEADS, SSM_HEAD_DIM, SSM_STATE)


def _mixer_call(x, mod, pre_g, post_g, w_main, w_dt, wa, wb, wo,
                conv_w, conv_b, dt_bias, a_log, d_skip, norm_g, ln_g, ln_b, w_sp, b_sp_t):
    nb, seq, _ = x.shape
    tm = MIX_CHUNKS * CHUNK

    def whole(a):
        zeros = (0,) * a.ndim
        return pl.BlockSpec(a.shape, lambda b, c: zeros)

    nsteps = seq // tm

    def following(b, c):
        lin = jnp.minimum(b * nsteps + c + 1, nb * nsteps - 1)
        return lin // nsteps, lin % nsteps

    def next_x(b, c):
        bn, cn = following(b, c)
        return bn, cn * (MIX_CHUNKS // MIX_GROUP), 0

    def next_mod(seg):
        return pl.BlockSpec((1, 1, D_MODEL), lambda b, c: (following(b, c)[0], 0, seg))

    small = (conv_w, conv_b, dt_bias, a_log, d_skip, norm_g, ln_g, ln_b, w_sp, b_sp_t)
    return pl.pallas_call(
        _mixer_kernel,
        grid=(nb, nsteps),
        in_specs=[pl.BlockSpec((1, tm, D_MODEL), lambda b, c: (b, c, 0)), whole(pre_g),
                  _mod_spec(mod, tm, MOD_SH1, 2), _mod_spec(mod, tm, MOD_SC1, 2), _mod_spec(mod, tm, MOD_G1, 2),
                  pl.BlockSpec((1, MIX_GROUP * CHUNK, D_MODEL), next_x), next_mod(MOD_SH1), next_mod(MOD_SC1),
                  whole(w_main), whole(w_dt)]
                 + [whole(a) for a in small]
                 + [whole(post_g), whole(wa), whole(wb), whole(wo)],
        out_specs=[
            pl.BlockSpec((1, tm, D_MODEL), lambda b, c: (b, c, 0)),
            pl.BlockSpec((1, SSM_HEADS, SSM_HEAD_DIM, SSM_STATE), lambda b, c: (b, 0, 0, 0)),
            pl.BlockSpec((1, CONV_WIDTH - 1, CONV_DIM), lambda b, c: (b, 0, 0)),
        ],
        out_shape=[
            jax.ShapeDtypeStruct(x.shape, F32),
            jax.ShapeDtypeStruct((nb, SSM_HEADS, SSM_HEAD_DIM, SSM_STATE), F32),
            jax.ShapeDtypeStruct((nb, CONV_WIDTH - 1, CONV_DIM), F32),
        ],
        scratch_shapes=[
            pltpu.VMEM((SSM_INNER, SSM_STATE), F32),
            pltpu.VMEM((CHUNK + 8, CONV_DIM), F32),
            pltpu.VMEM((CMLP_GROUPS, CHUNK, CHUNK), BF16),
            pltpu.VMEM((MIX_GROUP * CHUNK, PROJ_MAIN + LANES), F32),
        ],
        compiler_params=_params(("arbitrary", "arbitrary")),
        name="mixer_prompt",
    )(x, pre_g, mod, mod, mod, x, mod, mod, w_main, w_dt, *small, post_g, wa, wb, wo)


SAMPLE_TILE = 16


def _mixer_step_kernel(z_ref, u_ref, v_ref, xs_ref, bc_ref, dt_ref, ssm_in_ref, conv_in_ref,
                       cw_ref, cb_ref, dtb_ref, alog_ref, dskip_ref, ng_ref,
                       lng_ref, lnb_ref, wsp0_ref, bsp0_ref, ssm_all_ref,
                       ya_ref, yb_ref, vout_ref, ssm_ref, conv_ref):
    del ssm_all_ref
    tb = SAMPLE_TILE
    xbc_new = jnp.concatenate([xs_ref[...], bc_ref[...]], axis=1)
    hist = conv_in_ref[...]
    acc = xbc_new * cw_ref[CONV_WIDTH - 1:CONV_WIDTH, :]
    for k in range(CONV_WIDTH - 1):
        acc = acc + hist[:, k * CONV_DIM:(k + 1) * CONV_DIM] * cw_ref[k:k + 1, :]
    conv_ref[:, 0:(CONV_WIDTH - 2) * CONV_DIM] = hist[:, CONV_DIM:]
    conv_ref[:, (CONV_WIDTH - 2) * CONV_DIM:] = xbc_new

    xbc = jax.nn.silu(acc + cb_ref[...])
    xc = xbc[:, 0:SSM_INNER]
    bm = xbc[:, SSM_INNER:SSM_INNER + SSM_GROUPS * SSM_STATE]
    cm = xbc[:, SSM_INNER + SSM_GROUPS * SSM_STATE:CONV_DIM]
    dt = jax.nn.softplus(dt_ref[...] + dtb_ref[...])
    dec = jnp.exp(dt * (-jnp.exp(alog_ref[...])))

    def transposed(a):
        pad = jnp.zeros((LANES - tb, a.shape[1]), F32)
        return jnp.concatenate([a, pad], axis=0).T

    xc_t = transposed(xc)
    dt_t = transposed(dt)
    dec_t = transposed(dec)
    row_id = lax.broadcasted_iota(jnp.int32, (tb, SSM_STATE), 0)

    y_groups = [jnp.zeros((tb, GROUP_WIDTH), F32) for _ in range(SSM_GROUPS)]
    for b in range(tb):
        for g in range(SSM_GROUPS):
            b_row = bm[b:b + 1, g * SSM_STATE:(g + 1) * SSM_STATE]
            c_only = jnp.where(row_id == b, cm[:, g * SSM_STATE:(g + 1) * SSM_STATE], 0.0).astype(BF16)
            new_heads = []
            for r in range(HEADS_PER_GROUP):
                h = g * HEADS_PER_GROUP + r
                x_col = xc_t[h * SSM_HEAD_DIM:(h + 1) * SSM_HEAD_DIM, b:b + 1]
                push = x_col * dt_t[h:h + 1, b:b + 1]
                keep = jnp.broadcast_to(dec_t[h:h + 1, b:b + 1], (SSM_HEAD_DIM, SSM_STATE))
                h_new = ssm_in_ref[b, h] * keep + push * b_row
                ssm_ref[b, h] = h_new
                new_heads.append(h_new.astype(BF16))
            hg = jnp.concatenate(new_heads, axis=0)
            y_groups[g] = y_groups[g] + _dot_nt(c_only, hg)

    y = jnp.concatenate(y_groups, axis=1) + dskip_ref[...] * xc
    ya_ref[...] = _gated_group_norm(y, z_ref[...], ng_ref[...]).astype(BF16)

    ug = jax.nn.gelu(u_ref[...], approximate=True)
    vn = _layernorm(jax.nn.gelu(v_ref[...], approximate=True), lng_ref[...], lnb_ref[...])
    vout_ref[...] = vn
    yb_ref[...] = (ug * (vn * wsp0_ref[...] + bsp0_ref[...])).astype(BF16)


def _mixer_step_call(layer, proj, proj_dt, state_ssm, conv_hist, ssm_all, conv_w, conv_b, dt_bias, a_log,
                     d_skip, norm_g, ln_g, ln_b, w_sp0, b_sp0):
    nseq = proj.shape[0]
    tb = SAMPLE_TILE

    def col(block, width=D_MODEL):
        return pl.BlockSpec((tb, width), lambda i: (i, block))

    def whole(a):
        zeros = (0,) * a.ndim
        return pl.BlockSpec(a.shape, lambda i: zeros)

    small = (conv_w, conv_b, dt_bias, a_log, d_skip, norm_g, ln_g, ln_b, w_sp0, b_sp0)
    hist_width = (CONV_WIDTH - 1) * CONV_DIM
    return pl.pallas_call(
        _mixer_step_kernel,
        grid=(nseq // tb,),
        in_specs=[col(COL_Z), col(COL_U), col(COL_V), col(COL_XS),
                  col(COL_BC_512, 2 * SSM_GROUPS * SSM_STATE),
                  pl.BlockSpec((tb, LANES), lambda i: (i, 0)),
                  pl.BlockSpec((None, tb, SSM_HEADS, SSM_HEAD_DIM, SSM_STATE),
                               lambda i: (layer, i, 0, 0, 0)),
                  pl.BlockSpec((tb, hist_width), lambda i: (i, 0))]
                 + [whole(a) for a in small] + [pl.BlockSpec(memory_space=pl.ANY)],
        out_specs=[
            pl.BlockSpec((tb, SSM_INNER), lambda i: (i, 0)),
            pl.BlockSpec((tb, D_MODEL), lambda i: (i, 0)),
            pl.BlockSpec((tb, D_MODEL), lambda i: (i, 0)),
            pl.BlockSpec((None, tb, SSM_HEADS, SSM_HEAD_DIM, SSM_STATE), lambda i: (layer, i, 0, 0, 0)),
            pl.BlockSpec((tb, hist_width), lambda i: (i, 0)),
        ],
        out_shape=[
            jax.ShapeDtypeStruct((nseq, SSM_INNER), BF16),
            jax.ShapeDtypeStruct((nseq, D_MODEL), BF16),
            jax.ShapeDtypeStruct((nseq, D_MODEL), F32),
            jax.ShapeDtypeStruct(ssm_all.shape, F32),
            jax.ShapeDtypeStruct((nseq, hist_width), F32),
        ],
        input_output_aliases={8 + len(small): 3},
        compiler_params=_params(("arbitrary",)),
        name="mixer_sample",
    )(proj, proj, proj, proj, proj, proj_dt, state_ssm, conv_hist, *small, ssm_all)


def _outproj_kernel(ya_ref, yb_ref, ga_ref, gb_ref, x_ref, g1_ref, pg_ref, wa_ref, wb_ref, wo_ref, o_ref):
    o_ref[0] = _merge_project(ya_ref[0], yb_ref[0], ga_ref[0], gb_ref[0], x_ref[0], g1_ref[0], pg_ref[...],
                              wa_ref, wb_ref, wo_ref)


def _outproj_call(layer, ya, yb, proj, x, mod, post_g, wa, wb, wo, tm):
    nb, seq, _ = x.shape

    def rows(block=0):
        return pl.BlockSpec((1, tm, D_MODEL), lambda b, i: (b, i, block))

    def whole(a):
        return pl.BlockSpec(a.shape, lambda b, i: (0, 0))

    def of_layer(a):
        return pl.BlockSpec((None,) + a.shape[1:], lambda b, i: (layer, 0, 0))

    return pl.pallas_call(
        _outproj_kernel,
        grid=(nb, seq // tm),
        in_specs=[rows(), rows(), rows(COL_GA), rows(COL_GB), rows(),
                  _mod_spec(mod, tm, MOD_G1, 2), whole(post_g), of_layer(wa), of_layer(wb), of_layer(wo)],
        out_specs=rows(),
        out_shape=jax.ShapeDtypeStruct(x.shape, F32),
        compiler_params=_params(("arbitrary", "arbitrary")),
        name="out_proj",
    )(ya, yb, proj, proj, x, mod, post_g, wa, wb, wo)


FFN_CHUNK = 256


def _swiglu_into(hb, wg_ref, wu_ref, wd_ref, acc_ref):
    n_chunks = FFN_DIM // FFN_CHUNK

    def gate_up(f):
        cols = slice(f * FFN_CHUNK, (f + 1) * FFN_CHUNK)
        return _dot(hb, wg_ref[0, :, cols]), _dot(hb, wu_ref[0, :, cols])

    g, u = gate_up(0)
    for f in range(n_chunks):
        ahead = gate_up(f + 1) if f + 1 < n_chunks else None
        act = (jax.nn.silu(g) * u).astype(BF16)
        part = _dot(act, wd_ref[0, f * FFN_CHUNK:(f + 1) * FFN_CHUNK, :])
        if f == 0:
            acc_ref[...] = part
        else:
            acc_ref[...] += part
        if ahead is not None:
            g, u = ahead


def _premod(x_ref, pre_ref, sh_ref, sc_ref):
    return _rms(x_ref[0]) * pre_ref[...] * (1.0 + sc_ref[0]) + sh_ref[0]


def _dense_ffn_kernel(x_ref, pre_ref, post_ref, sh_ref, sc_ref, g2_ref, wg_ref, wu_ref, wd_ref,
                      o_ref, acc_scr):
    hb = _premod(x_ref, pre_ref, sh_ref, sc_ref).astype(BF16)
    _swiglu_into(hb, wg_ref, wu_ref, wd_ref, acc_scr)
    o_ref[0] = x_ref[0] + g2_ref[0] * (_rms(acc_scr[...]) * post_ref[...])


def _dense_ffn_call(j, x, mod, pre_g, post_g, wg, wu, wd, tm):
    nb, seq, _ = x.shape

    def whole(a):
        zeros = (0,) * a.ndim
        return pl.BlockSpec(a.shape, lambda b, i: zeros)

    def of_layer(a):
        return pl.BlockSpec((1,) + a.shape[1:], lambda b, i: (j, 0, 0))

    rows = pl.BlockSpec((1, tm, D_MODEL), lambda b, i: (b, i, 0))
    return pl.pallas_call(
        _dense_ffn_kernel,
        grid=(nb, seq // tm),
        in_specs=[rows, whole(pre_g), whole(post_g),
                  _mod_spec(mod, tm, MOD_SH2, 2), _mod_spec(mod, tm, MOD_SC2, 2),
                  _mod_spec(mod, tm, MOD_G2, 2), of_layer(wg), of_layer(wu), of_layer(wd)],
        out_specs=rows,
        out_shape=jax.ShapeDtypeStruct(x.shape, F32),
        scratch_shapes=[pltpu.VMEM((tm, D_MODEL), F32)],
        compiler_params=_params(("arbitrary", "arbitrary")),
        name="dense_ffn",
    )(x, pre_g, post_g, mod, mod, mod, wg, wu, wd)


EXPERT_TILE = 512
HALF = D_MODEL // 2
INFO_I1, INFO_I2, INFO_P1, INFO_P2, INFO_R1, INFO_R2 = range(6)
HI_MASK = 0xFFFF0000
ISSUE_UNROLL = 8


def _route_kernel(x_ref, pre_ref, sh_ref, sc_ref, rw_ref, rb_ref, cnt0_ref,
                  hw_ref, info_ref, cnt_ref, carry):
    @pl.when(jnp.logical_and(pl.program_id(0) == 0, pl.program_id(1) == 0))
    def _():
        carry[...] = cnt0_ref[...]

    hb = _premod(x_ref, pre_ref, sh_ref, sc_ref).astype(BF16)
    tm = hb.shape[0]
    bits = lax.bitcast_convert_type(hb.astype(F32), jnp.uint32)
    hw_ref[0] = (bits[:, HALF:] & jnp.uint32(HI_MASK)) | (bits[:, :HALF] >> jnp.uint32(16))

    logits = _dot(hb, rw_ref[...].astype(BF16)) + rb_ref[...]
    lane = lax.broadcasted_iota(jnp.int32, logits.shape, 1).astype(F32)
    neg = jnp.float32(-jnp.inf)
    logits = jnp.where(lane < N_EXPERTS, logits, neg)
    m1 = jnp.max(logits, axis=-1, keepdims=True)
    i1 = jnp.min(jnp.where(logits == m1, lane, float(LANES)), axis=-1, keepdims=True)
    rest = jnp.where(lane == i1, neg, logits)
    m2 = jnp.max(rest, axis=-1, keepdims=True)
    i2 = jnp.min(jnp.where(rest == m2, lane, float(LANES)), axis=-1, keepdims=True)
    e2 = jnp.exp(m2 - m1)
    p1 = 1.0 / (1.0 + e2)
    p2 = e2 * p1

    member = jnp.logical_or(lane == i1, lane == i2)
    row = lax.broadcasted_iota(jnp.int32, (tm, tm), 0)
    col = lax.broadcasted_iota(jnp.int32, (tm, tm), 1)
    before = jnp.where(row > col, 1.0, 0.0).astype(BF16)
    ones = jnp.where(member, 1.0, 0.0)
    prior = _dot(before, ones.astype(BF16)) + carry[...]
    r1 = jnp.sum(jnp.where(lane == i1, prior, 0.0), axis=-1, keepdims=True)
    r2 = jnp.sum(jnp.where(lane == i2, prior, 0.0), axis=-1, keepdims=True)
    carry[...] += jnp.sum(ones, axis=0, keepdims=True)
    cnt_ref[...] = carry[...]

    info = jnp.zeros_like(logits)
    for k, v in ((INFO_I1, i1), (INFO_I2, i2), (INFO_P1, p1), (INFO_P2, p2), (INFO_R1, r1), (INFO_R2, r2)):
        info = jnp.where(lane == float(k), v, info)
    info_ref[0] = info


def _route_call(x, mod, pre_g, rw, rb, cnt0, tm):
    nb, seq, _ = x.shape

    def whole(a):
        return pl.BlockSpec(a.shape, lambda b, i: (0, 0))

    return pl.pallas_call(
        _route_kernel,
        grid=(nb, seq // tm),
        in_specs=[pl.BlockSpec((1, tm, D_MODEL), lambda b, i: (b, i, 0)), whole(pre_g),
                  _mod_spec(mod, tm, MOD_SH2, 2), _mod_spec(mod, tm, MOD_SC2, 2),
                  whole(rw), whole(rb), whole(cnt0)],
        out_specs=[pl.BlockSpec((1, tm, HALF), lambda b, i: (b, i, 0)),
                   pl.BlockSpec((1, tm, LANES), lambda b, i: (b, i, 0)),
                   pl.BlockSpec((1, LANES), lambda b, i: (0, 0))],
        out_shape=[jax.ShapeDtypeStruct((nb, seq, HALF), jnp.uint32),
                   jax.ShapeDtypeStruct((nb, seq, LANES), F32),
                   jax.ShapeDtypeStruct((1, LANES), F32)],
        scratch_shapes=[pltpu.VMEM((1, LANES), F32)],
        compiler_params=_params(("arbitrary", "arbitrary")),
        name="moe_route",
    )(x, pre_g, mod, mod, rw, rb, cnt0)


def _row_copy(src, src_row, dst, dst_row, sem):
    return pltpu.make_async_copy(src.at[pl.ds(src_row, 1)], dst.at[pl.ds(dst_row, 1)], sem)


def _dispatch_kernel(d1_ref, d2_ref, hw_ref, xs_in_ref, xs_ref, sem):
    del xs_in_ref
    tm = hw_ref.shape[0]
    base = pl.program_id(0) * tm

    def issue(r, carry):
        _row_copy(hw_ref, r, xs_ref, d1_ref[base + r], sem).start(priority=0)
        _row_copy(hw_ref, r, xs_ref, d2_ref[base + r], sem).start(priority=1)
        return carry

    lax.fori_loop(0, tm, issue, 0, unroll=ISSUE_UNROLL)
    for _ in range(2):
        pltpu.make_async_copy(hw_ref, xs_ref.at[pl.ds(0, tm)], sem).wait()


def _dispatch_call(dest1, dest2, hw, xs, tm):
    rows = hw.shape[0]
    return pl.pallas_call(
        _dispatch_kernel,
        grid_spec=pltpu.PrefetchScalarGridSpec(
            num_scalar_prefetch=2,
            grid=(rows // tm,),
            in_specs=[pl.BlockSpec((tm, HALF), lambda i, d1, d2: (i, 0)),
                      pl.BlockSpec(memory_space=pl.ANY)],
            out_specs=pl.BlockSpec(memory_space=pl.ANY),
            scratch_shapes=[pltpu.SemaphoreType.DMA(())],
        ),
        out_shape=jax.ShapeDtypeStruct(xs.shape, xs.dtype),
        input_output_aliases={3: 0},
        compiler_params=_params(("arbitrary",), row_dma=True),
        name="moe_dispatch",
    )(dest1, dest2, hw, xs)


def _experts_kernel(exp_ref, nv_ref, xs_ref, wg_ref, wu_ref, wd_ref, y_ref):
    del exp_ref
    used = pl.program_id(0) < nv_ref[0]

    @pl.when(used)
    def _():
        words = xs_ref[...]
        lo = lax.bitcast_convert_type(words << jnp.uint32(16), F32)
        hi = lax.bitcast_convert_type(words & jnp.uint32(HI_MASK), F32)
        hb = jnp.concatenate([lo, hi], axis=1).astype(BF16)
        _swiglu_into(hb, wg_ref, wu_ref, wd_ref, y_ref)

    @pl.when(jnp.logical_not(used))
    def _():
        y_ref[...] = jnp.zeros_like(y_ref)


def _experts_call(j, tile_exp, n_valid, xs, wg, wu, wd):
    n_tiles = tile_exp.shape[0]
    tm = EXPERT_TILE
    weight = lambda shape: pl.BlockSpec((None, 1) + shape, lambda i, ex, nv: (j, ex[i], 0, 0))
    return pl.pallas_call(
        _experts_kernel,
        grid_spec=pltpu.PrefetchScalarGridSpec(
            num_scalar_prefetch=2,
            grid=(n_tiles,),
            in_specs=[pl.BlockSpec((tm, HALF), lambda i, ex, nv: (i, 0)),
                      weight((D_MODEL, FFN_DIM)), weight((D_MODEL, FFN_DIM)), weight((FFN_DIM, D_MODEL))],
            out_specs=pl.BlockSpec((tm, D_MODEL), lambda i, ex, nv: (i, 0)),
        ),
        out_shape=jax.ShapeDtypeStruct((xs.shape[0], D_MODEL), F32),
        compiler_params=_params(("arbitrary",)),
        name="moe_experts",
    )(tile_exp, n_valid, xs, wg, wu, wd)


def _combine_kernel(d1_ref, d2_ref, y_ref, info_ref, x_ref, g2_ref, post_ref, o_ref, ybuf, sems):
    tm = x_ref.shape[1]
    step = pl.program_id(0) * pl.num_programs(1) + pl.program_id(1)
    n_steps = pl.num_programs(0) * pl.num_programs(1)

    def gather(tile, slot):
        base = tile * tm

        def issue(r, carry):
            _row_copy(y_ref, d1_ref[base + r], ybuf.at[slot, 0], r, sems.at[slot]).start(priority=0)
            _row_copy(y_ref, d2_ref[base + r], ybuf.at[slot, 1], r, sems.at[slot]).start(priority=1)
            return carry

        lax.fori_loop(0, tm, issue, 0, unroll=ISSUE_UNROLL)

    slot = step % 2

    @pl.when(step == 0)
    def _():
        gather(step, slot)

    @pl.when(step + 1 < n_steps)
    def _():
        gather(step + 1, 1 - slot)

    for k in range(2):
        pltpu.make_async_copy(y_ref.at[pl.ds(0, tm)], ybuf.at[slot, k], sems.at[slot]).wait()

    info = info_ref[0]
    f = info[:, INFO_P1:INFO_P1 + 1] * ybuf[slot, 0] + info[:, INFO_P2:INFO_P2 + 1] * ybuf[slot, 1]
    o_ref[0] = x_ref[0] + g2_ref[0] * (_rms(f) * post_ref[...])


def _combine_call(dest1, dest2, y, info, x, mod, post_g, tm):
    nb, seq, _ = x.shape
    per_row = mod.shape[1] != 1
    return pl.pallas_call(
        _combine_kernel,
        grid_spec=pltpu.PrefetchScalarGridSpec(
            num_scalar_prefetch=2,
            grid=(nb, seq // tm),
            in_specs=[pl.BlockSpec(memory_space=pl.ANY),
                      pl.BlockSpec((1, tm, LANES), lambda b, i, d1, d2: (b, i, 0)),
                      pl.BlockSpec((1, tm, D_MODEL), lambda b, i, d1, d2: (b, i, 0)),
                      pl.BlockSpec((1, tm if per_row else 1, D_MODEL),
                                   lambda b, i, d1, d2: (b, i if per_row else 0, MOD_G2)),
                      pl.BlockSpec(post_g.shape, lambda b, i, d1, d2: (0, 0))],
            out_specs=pl.BlockSpec((1, tm, D_MODEL), lambda b, i, d1, d2: (b, i, 0)),
            scratch_shapes=[pltpu.VMEM((2, 2, tm, D_MODEL), F32), pltpu.SemaphoreType.DMA((2,))],
        ),
        out_shape=jax.ShapeDtypeStruct(x.shape, F32),
        compiler_params=_params(("arbitrary", "arbitrary"), row_dma=True),
        name="moe_combine",
    )(dest1, dest2, y, info, x, mod, post_g)


def _moe_layer(j, xp, xs, mod_p, mod_s, pre_g, post_g, router_w, router_b, wg, wu, wd):
    rw = jnp.pad(router_w, ((0, 0), (0, LANES - N_EXPERTS)))
    rb = _pad_lanes(router_b)
    hw_p, info_p, cnt_p = _route_call(xp, mod_p, pre_g, rw, rb, jnp.zeros((1, LANES), F32), tm=512)
    hw_s, info_s, cnt = _route_call(xs, mod_s, pre_g, rw, rb, cnt_p, tm=xs.shape[1])

    tm = EXPERT_TILE
    n_assign = 2 * (xp.shape[0] * xp.shape[1] + xs.shape[1])
    n_tiles = n_assign // tm + N_EXPERTS
    counts = cnt[0, :N_EXPERTS].astype(jnp.int32)
    group_tiles = (counts + tm - 1) // tm
    tile_end = jnp.cumsum(group_tiles)
    start = (tile_end - group_tiles) * tm
    n_valid = tile_end[-1]
    tile_id = jnp.minimum(jnp.arange(n_tiles, dtype=jnp.int32), n_valid - 1)
    tile_exp = jnp.sum(tile_id[:, None] >= tile_end[None, :], axis=1).astype(jnp.int32)

    def dests(info):
        flat = info.reshape(-1, LANES)
        d = [start[flat[:, i].astype(jnp.int32)] + flat[:, r].astype(jnp.int32)
             for i, r in ((INFO_I1, INFO_R1), (INFO_I2, INFO_R2))]
        return d[0], d[1]

    dp = dests(info_p)
    ds = dests(info_s)
    slots = jnp.zeros((n_tiles * tm, HALF), jnp.uint32)
    slots = _dispatch_call(dp[0], dp[1], hw_p.reshape(-1, HALF), slots, tm=1024)
    slots = _dispatch_call(ds[0], ds[1], hw_s.reshape(-1, HALF), slots, tm=xs.shape[1])
    y = _experts_call(j, tile_exp, n_valid.reshape(1), slots, wg, wu, wd)
    xp = _combine_call(dp[0], dp[1], y, info_p, xp, mod_p, post_g, tm=512)
    xs = _combine_call(ds[0], ds[1], y, info_s, xs, mod_s, post_g, tm=xs.shape[1])
    return xp, xs


def _pad_lanes(v):
    return jnp.pad(v, (0, LANES - v.shape[0])).reshape(1, LANES)


def kernel(x_prompt, x_sample, state_ssm, state_conv, c_prompt, c_sample, w_mod, b_mod, mix_pre_g, mix_post_g, ffn_pre_g, ffn_post_g, w_in, conv_w, conv_b, dt_bias, a_log, d_skip, ssm_norm_g, w_ssd_out, cmlp_ln_g, cmlp_ln_b, w_spatial, b_spatial, w_cmlp_out, w_o, ffn_wg, ffn_wu, ffn_wd, router_w, router_b, exp_wg, exp_wu, exp_wd):
    n_prompt = x_prompt.shape[0]
    n_sample = x_sample.shape[0]

    c_all = jnp.concatenate([c_prompt, c_sample, jnp.zeros((8, D_MODEL), F32)], axis=0)
    mod_all = _mod_call(c_all, w_mod, b_mod)
    mod_p = mod_all[:, :n_prompt].reshape(DEPTH, n_prompt, 1, 6 * D_MODEL)
    mod_s = mod_all[:, n_prompt:n_prompt + n_sample].reshape(DEPTH, 1, n_sample, 6 * D_MODEL)

    xp = x_prompt
    xs = x_sample.reshape(1, n_sample, D_MODEL)
    conv_hist = state_conv.reshape(DEPTH, n_sample, (CONV_WIDTH - 1) * CONV_DIM)

    o_xbc = SSM_INNER
    o_dt = o_xbc + CONV_DIM
    o_u = o_dt + SSM_HEADS

    w_main = jnp.concatenate([w_in[:, :, :o_xbc], w_in[:, :, o_u:], w_in[:, :, o_xbc:o_dt]], axis=2).astype(BF16)
    w_dt = jnp.pad(w_in[:, :, o_dt:o_u], ((0, 0), (0, 0), (0, LANES - SSM_HEADS))).astype(BF16)
    wa, wb, wo = (a.astype(BF16) for a in (w_ssd_out, w_cmlp_out, w_o))
    dense_w = tuple(a.astype(BF16) for a in (ffn_wg, ffn_wu, ffn_wd))
    expert_w = tuple(a.astype(BF16) for a in (exp_wg, exp_wu, exp_wd))

    ssm_s = jnp.zeros(state_ssm.shape, F32)
    ssm_p, conv_p, conv_s, v_s = [], [], [], []
    for l in range(DEPTH):
        row = lambda a: a[l].reshape(1, -1)
        mixer_small = (conv_w[l], row(conv_b), _pad_lanes(dt_bias[l]), _pad_lanes(a_log[l]),
                       jnp.repeat(d_skip[l], SSM_HEAD_DIM).reshape(1, -1), row(ssm_norm_g),
                       row(cmlp_ln_g), row(cmlp_ln_b))

        proj_s, dt_s = _inproj_call(l, xs, mod_s[l], row(mix_pre_g), w_main, w_dt, tm=n_sample)
        xp, hs_p, cs_p = _mixer_call(xp, mod_p[l], row(mix_pre_g), row(mix_post_g), w_main[l], w_dt[l],
                                     wa[l], wb[l], wo[l], *mixer_small, w_spatial[l], b_spatial[l].T)
        ya_s, yb_s, vr_s, ssm_s, cs_s = _mixer_step_call(
            l, proj_s[0], dt_s[0], state_ssm, conv_hist[l], ssm_s, *mixer_small,
            jnp.repeat(w_spatial[l, :, 0, 0], CMLP_GROUP_DIM).reshape(1, -1),
            jnp.repeat(b_spatial[l, :, 0], CMLP_GROUP_DIM).reshape(1, -1))
        xs = _outproj_call(l, ya_s[None], yb_s[None], proj_s, xs, mod_s[l], row(mix_post_g), wa, wb, wo,
                           tm=n_sample)

        j = l // 2
        if l % 2 == 0:
            xp = _dense_ffn_call(j, xp, mod_p[l], row(ffn_pre_g), row(ffn_post_g), *dense_w, tm=512)
            xs = _dense_ffn_call(j, xs, mod_s[l], row(ffn_pre_g), row(ffn_post_g), *dense_w, tm=n_sample)
        else:
            xp, xs = _moe_layer(j, xp, xs, mod_p[l], mod_s[l], row(ffn_pre_g), row(ffn_post_g),
                                router_w[j], router_b[j], *expert_w)

        ssm_p.append(hs_p)
        conv_p.append(cs_p)
        conv_s.append(cs_s.reshape(n_sample, CONV_WIDTH - 1, CONV_DIM))
        v_s.append(vr_s.reshape(n_sample, 1, D_MODEL))

    return (xp, xs.reshape(n_sample, 1, D_MODEL), jnp.stack(ssm_p), jnp.stack(conv_p),
            ssm_s, jnp.stack(conv_s), jnp.stack(v_s))
```

```python
import functools

import jax
import jax.numpy as jnp
from jax import lax
from jax.experimental import pallas as pl
from jax.experimental.pallas import tpu as pltpu

F32 = jnp.float32
BF16 = jnp.bfloat16

D_MODEL = 1024
DEPTH = 4
SSM_HEADS = 16
SSM_HEAD_DIM = 64
SSM_GROUPS = 2
SSM_STATE = 128
SSM_INNER = 1024
GROUP_WIDTH = SSM_INNER // SSM_GROUPS
HEADS_PER_GROUP = SSM_HEADS // SSM_GROUPS
CONV_WIDTH = 4
CONV_DIM = 1536
CHUNK = 128
CMLP_GROUPS = 8
CMLP_GROUP_DIM = 128
FFN_DIM = 2816
N_EXPERTS = 8
EPS = 1e-6
LANES = 128
COL_Z, COL_U, COL_V, COL_GA, COL_GB, COL_XS = 0, 1, 2, 3, 4, 5
COL_BC_512 = 12
PROJ_MAIN = 6 * D_MODEL + 2 * SSM_GROUPS * SSM_STATE
PROJ_TILE = 1664
MOD_SH1, MOD_SC1, MOD_G1, MOD_SH2, MOD_SC2, MOD_G2 = range(6)
VMEM_LIMIT = 56 * 1024 * 1024


def _params(semantics, row_dma=False):
    return pltpu.CompilerParams(dimension_semantics=semantics, vmem_limit_bytes=VMEM_LIMIT,
                                disable_bounds_checks=row_dma)


def _rms(x):
    return x * lax.rsqrt(jnp.mean(x * x, axis=-1, keepdims=True) + EPS)


def _dot(a, b):
    return jnp.dot(a, b, preferred_element_type=F32)


def _dot_nt(a, b):
    return lax.dot_general(a, b, (((1,), (1,)), ((), ())), preferred_element_type=F32)


def _dot_tn(a, b):
    return lax.dot_general(a, b, (((0,), (0,)), ((), ())), preferred_element_type=F32)


def _mod_kernel(c_ref, w_ref, b_ref, o_ref):
    a = jax.nn.silu(c_ref[...]).astype(BF16)
    o_ref[0] = _dot(a, w_ref[0].astype(BF16)) + b_ref[0]


def _mod_call(c_all, w_mod, b_mod):
    rows = c_all.shape[0]
    width = 2 * D_MODEL
    return pl.pallas_call(
        _mod_kernel,
        grid=(DEPTH, 6 * D_MODEL // width),
        in_specs=[
            pl.BlockSpec((rows, D_MODEL), lambda l, j: (0, 0)),
            pl.BlockSpec((1, D_MODEL, width), lambda l, j: (l, 0, j)),
            pl.BlockSpec((1, 1, width), lambda l, j: (l, 0, j)),
        ],
        out_specs=pl.BlockSpec((1, rows, width), lambda l, j: (l, 0, j)),
        out_shape=jax.ShapeDtypeStruct((DEPTH, rows, 6 * D_MODEL), F32),
        compiler_params=_params(("arbitrary", "arbitrary")),
        name="adaln_mod",
    )(c_all, w_mod, b_mod.reshape(DEPTH, 1, 6 * D_MODEL))


def _mod_spec(mod, tm, seg, grid_rank):
    per_row = mod.shape[1] != 1
    rows = tm if per_row else 1
    if grid_rank == 2:
        return pl.BlockSpec((1, rows, D_MODEL), lambda b, i: (b, i if per_row else 0, seg))
    return pl.BlockSpec((1, rows, D_MODEL), lambda b, i, j: (b, i if per_row else 0, seg))


def _inproj_kernel(x_ref, g_ref, sh_ref, sc_ref, w_ref, wdt_ref, p_ref, pdt_ref, h_scr):
    @pl.when(pl.program_id(2) == 0)
    def _():
        h = _rms(x_ref[0]) * g_ref[...] * (1.0 + sc_ref[0]) + sh_ref[0]
        hb = h.astype(BF16)
        h_scr[...] = hb
        pdt_ref[0] = _dot(hb, wdt_ref[...])

    p_ref[0] = _dot(h_scr[...], w_ref[...])


def _inproj_call(layer, x, mod, g, w_main, w_dt, tm):
    nb, seq, _ = x.shape
    grid = (nb, seq // tm, PROJ_MAIN // PROJ_TILE)
    return pl.pallas_call(
        _inproj_kernel,
        grid=grid,
        in_specs=[
            pl.BlockSpec((1, tm, D_MODEL), lambda b, i, j: (b, i, 0)),
            pl.BlockSpec((1, D_MODEL), lambda b, i, j: (0, 0)),
            _mod_spec(mod, tm, MOD_SH1, 3),
            _mod_spec(mod, tm, MOD_SC1, 3),
            pl.BlockSpec((None, D_MODEL, PROJ_TILE), lambda b, i, j: (layer, 0, j)),
            pl.BlockSpec((None, D_MODEL, LANES), lambda b, i, j: (layer, 0, 0)),
        ],
        out_specs=[
            pl.BlockSpec((1, tm, PROJ_TILE), lambda b, i, j: (b, i, j)),
            pl.BlockSpec((1, tm, LANES), lambda b, i, j: (b, i, 0)),
        ],
        out_shape=[
            jax.ShapeDtypeStruct((nb, seq, PROJ_MAIN), F32),
            jax.ShapeDtypeStruct((nb, seq, LANES), F32),
        ],
        scratch_shapes=[pltpu.VMEM((tm, D_MODEL), BF16)],
        compiler_params=_params(("arbitrary", "arbitrary", "arbitrary")),
        name="in_proj",
    )(x, g, mod, mod, w_main, w_dt)


def _gated_group_norm(y, z, norm_g):
    y = y * jax.nn.silu(z)
    parts = [_rms(y[:, g * GROUP_WIDTH:(g + 1) * GROUP_WIDTH]) for g in range(SSM_GROUPS)]
    return jnp.concatenate(parts, axis=-1) * norm_g


def _layernorm(x, g, b):
    mu = jnp.mean(x, axis=-1, keepdims=True)
    xc = x - mu
    return xc * lax.rsqrt(jnp.mean(xc * xc, axis=-1, keepdims=True) + EPS) * g + b


def _pair_columns(v, pair, lane_lo):
    h0 = 2 * pair
    return jnp.where(lane_lo, v[:, h0:h0 + 1], v[:, h0 + 1:h0 + 2])


MIX_CHUNKS = 4
MIX_GROUP = 1
P0_COLS = {"z": (COL_Z * D_MODEL, D_MODEL), "u": (COL_U * D_MODEL, D_MODEL), "v": (COL_V * D_MODEL, D_MODEL),
           "ga": (COL_GA * D_MODEL, D_MODEL), "gb": (COL_GB * D_MODEL, D_MODEL),
           "xs": (COL_XS * D_MODEL, D_MODEL), "bc": (COL_BC_512 * 512, 2 * SSM_GROUPS * SSM_STATE),
           "dt": (PROJ_MAIN, LANES)}


def _mixer_chunk(z, u, v, xs, bc, dt_raw,
                 cw_ref, cb_ref, dtb_ref, alog_ref, dskip_ref, ng_ref, lng_ref, lnb_ref, bsp_ref,
                 h_scr, cbuf, wsp_scr, causal, lane_lo, fill):
    cbuf[8:8 + CHUNK, 0:SSM_INNER] = xs
    cbuf[8:8 + CHUNK, SSM_INNER:CONV_DIM] = bc
    window = cbuf[...]
    acc = window * cw_ref[0:1, :]
    for k in range(1, CONV_WIDTH):
        acc = pltpu.roll(acc, 1, 0) + window * cw_ref[k:k + 1, :]
    acc = acc[8:8 + CHUNK, :]
    cbuf[0:8, :] = cbuf[CHUNK:8 + CHUNK, :]
    fill()

    xbc = jax.nn.silu(acc + cb_ref[...])
    xc = xbc[:, 0:SSM_INNER]
    bm = xbc[:, SSM_INNER:SSM_INNER + SSM_GROUPS * SSM_STATE].astype(BF16)
    cm = xbc[:, SSM_INNER + SSM_GROUPS * SSM_STATE:CONV_DIM].astype(BF16)
    fill()

    dt = jax.nn.softplus(dt_raw + dtb_ref[...])
    da = dt * (-jnp.exp(alog_ref[...]))
    cum = jnp.dot(causal.astype(F32), da, preferred_element_type=F32,
                  precision=lax.Precision.HIGHEST)
    last = cum[CHUNK - 1:CHUNK, :]
    ecum = jnp.exp(cum)
    wend = jnp.exp(last - cum) * dt
    elast = jnp.exp(last)
    cum_t = cum.T
    dt_t = dt.T
    fill()

    y_pairs = []
    for g in range(SSM_GROUPS):
        bg = bm[:, g * SSM_STATE:(g + 1) * SSM_STATE]
        cg = cm[:, g * SSM_STATE:(g + 1) * SSM_STATE]
        cb = _dot_nt(cg, bg)
        hg = h_scr[g * GROUP_WIDTH:(g + 1) * GROUP_WIDTH, :]
        y_state = _dot_nt(cg, hg.astype(BF16))
        xw_parts = []
        for q in range(HEADS_PER_GROUP // 2):
            pair = g * (HEADS_PER_GROUP // 2) + q
            mixes = []
            for h in (2 * pair, 2 * pair + 1):
                seg = cum[:, h:h + 1] - cum_t[h:h + 1, :]
                decay = jnp.where(causal, jnp.exp(seg), 0.0)
                mixes.append((cb * decay * dt_t[h:h + 1, :]).astype(BF16))
            xp = xc[:, pair * LANES:(pair + 1) * LANES]
            rhs = jnp.concatenate([jnp.where(lane_lo, xp, 0.0), jnp.where(lane_lo, 0.0, xp)],
                                  axis=0).astype(BF16)
            y_in = _dot(jnp.concatenate(mixes, axis=1), rhs)
            y_st = y_state[:, q * LANES:(q + 1) * LANES] * _pair_columns(ecum, pair, lane_lo)
            y_pairs.append(y_in + y_st)
            xw_parts.append((xp * _pair_columns(wend, pair, lane_lo)).astype(BF16))
            fill()
        upd = _dot_tn(jnp.concatenate(xw_parts, axis=1), bg)
        for r in range(HEADS_PER_GROUP):
            h = g * HEADS_PER_GROUP + r
            head = slice(h * SSM_HEAD_DIM, (h + 1) * SSM_HEAD_DIM)
            scale = jnp.broadcast_to(elast[0:1, h:h + 1], (SSM_HEAD_DIM, SSM_STATE))
            h_scr[head, :] = h_scr[head, :] * scale + upd[r * SSM_HEAD_DIM:(r + 1) * SSM_HEAD_DIM, :]

    y = jnp.concatenate(y_pairs, axis=1) + dskip_ref[...] * xc
    ya = _gated_group_norm(y, z, ng_ref[...]).astype(BF16)
    fill()

    ug = jax.nn.gelu(u, approximate=True)
    vn = _layernorm(jax.nn.gelu(v, approximate=True), lng_ref[...], lnb_ref[...])
    fill()
    gates = []
    for g in range(CMLP_GROUPS):
        vg = vn[:, g * CMLP_GROUP_DIM:(g + 1) * CMLP_GROUP_DIM].astype(BF16)
        gates.append(_dot(wsp_scr[g], vg) + bsp_ref[:, g:g + 1])
        fill()
    yb = (ug * jnp.concatenate(gates, axis=1)).astype(BF16)
    return ya, yb


def _merge_project(ya, yb, ga, gb, x, g1, post_g, wa_ref, wb_ref, wo_ref):
    merged = (jax.nn.sigmoid(ga) * _dot(ya, wa_ref[...]) + jax.nn.sigmoid(gb) * _dot(yb, wb_ref[...]))
    o = _dot(merged.astype(BF16), wo_ref[...])
    return x + g1 * (_rms(o) * post_g)


def _mixer_kernel(x_ref, pre_ref, sh_ref, sc_ref, g1_ref, xn_ref, shn_ref, scn_ref, win_ref, wdt_ref,
                  cw_ref, cb_ref, dtb_ref, alog_ref, dskip_ref, ng_ref, lng_ref, lnb_ref, wsp_ref, bsp_ref,
                  pg_ref, wa_ref, wb_ref, wo_ref,
                  o_ref, ssm_ref, conv_ref,
                  h_scr, cbuf, wsp_scr, p0_scr):
    c = pl.program_id(1)
    row = lax.broadcasted_iota(jnp.int32, (CHUNK, CHUNK), 0)
    col = lax.broadcasted_iota(jnp.int32, (CHUNK, CHUNK), 1)
    causal = row >= col
    lane_lo = col < SSM_HEAD_DIM

    @pl.when(c == 0)
    def _():
        h_scr[...] = jnp.zeros_like(h_scr)
        cbuf[0:8, :] = jnp.zeros((8, CONV_DIM), F32)
        for g in range(CMLP_GROUPS):
            wsp_scr[g] = jnp.where(causal, wsp_ref[g], 0.0).astype(BF16)

    def project(hb, block, width=D_MODEL):
        return _dot(hb, win_ref[:, block * D_MODEL:block * D_MODEL + width])

    def projection_pieces(load_x, load_sh, load_sc, store):
        t = {}

        def prep():
            x = load_x()
            store("x", x)
            t["hb"] = (_rms(x) * pre_ref[...] * (1.0 + load_sc()) + load_sh()).astype(BF16)

        pieces = [prep]
        for name, block in (("z", COL_Z), ("u", COL_U), ("v", COL_V), ("xs", COL_XS), ("bc", COL_XS + 1),
                            ("ga", COL_GA), ("gb", COL_GB)):
            width = P0_COLS[name][1]
            pieces.append(lambda name=name, block=block, width=width:
                          store(name, project(t["hb"], block, width)))
        pieces.append(lambda: store("dt", _dot(t["hb"], wdt_ref[...])))
        return pieces

    group_rows = MIX_GROUP * CHUNK

    def store_first(name, value):
        if name != "x":
            start, width = P0_COLS[name]
            p0_scr[:, start:start + width] = value

    def out_projection(rows, p, ya, yb):
        t = {}

        def merge():
            t["m"] = (jax.nn.sigmoid(p["ga"]) * _dot(ya, wa_ref[...])
                      + jax.nn.sigmoid(p["gb"]) * _dot(yb, wb_ref[...])).astype(BF16)

        def finish():
            o = _dot(t["m"], wo_ref[...])
            o_ref[0, rows, :] = x_ref[0, rows, :] + g1_ref[0] * (_rms(o) * pg_ref[...])

        return [merge, finish]

    @pl.when(jnp.logical_and(pl.program_id(0) == 0, c == 0))
    def _():
        for piece in projection_pieces(lambda: x_ref[0, 0:group_rows, :], lambda: sh_ref[0], lambda: sc_ref[0],
                                       store_first):
            piece()

    class _FromScratch:
        def __getitem__(self, name):
            start, width = P0_COLS[name]
            return p0_scr.at[:, start:start + width]

    cur = _FromScratch()
    pending = []
    n_groups = MIX_CHUNKS // MIX_GROUP
    for j in range(n_groups):
        rows = slice(j * group_rows, (j + 1) * group_rows)
        if j + 1 < n_groups:
            nxt = {}
            nrows = slice((j + 1) * group_rows, (j + 2) * group_rows)
            more = projection_pieces(lambda nrows=nrows: x_ref[0, nrows, :], lambda: sh_ref[0],
                                     lambda: sc_ref[0], nxt.__setitem__)
        else:
            more = projection_pieces(lambda: xn_ref[0], lambda: shn_ref[0], lambda: scn_ref[0], store_first)
        pending = pending[:1] + more[:2] + pending[1:] + more[2:]

        def fill():
            if pending:
                pending.pop(0)()

        yas, ybs = [], []
        for i in range(MIX_GROUP):
            sub = slice(i * CHUNK, (i + 1) * CHUNK)
            ya, yb = _mixer_chunk(*(cur[name][sub, :] for name in ("z", "u", "v", "xs", "bc", "dt")),
                                  cw_ref, cb_ref, dtb_ref, alog_ref, dskip_ref, ng_ref, lng_ref, lnb_ref,
                                  bsp_ref, h_scr, cbuf, wsp_scr, causal, lane_lo, fill)
            yas.append(ya)
            ybs.append(yb)
        while pending:
            fill()
        gates = {name: cur[name][...] for name in ("ga", "gb")}
        pending = out_projection(rows, gates, jnp.concatenate(yas, axis=0), jnp.concatenate(ybs, axis=0))
        if j + 1 < n_groups:
            cur = nxt
    while pending:
        pending.pop(0)()

    @pl.when(c == pl.num_programs(1) - 1)
    def _():
        conv_ref[0] = cbuf[5:8, :]
        ssm_ref[0] = h_scr[...].reshape(SSM_HEADS, SSM_HEAD_DIM, SSM_STATE)


def _mixer_call(x, mod, pre_g, post_g, w_main, w_dt, wa, wb, wo,
                conv_w, conv_b, dt_bias, a_log, d_skip, norm_g, ln_g, ln_b, w_sp, b_sp_t):
    nb, seq, _ = x.shape
    tm = MIX_CHUNKS * CHUNK

    def whole(a):
        zeros = (0,) * a.ndim
        return pl.BlockSpec(a.shape, lambda b, c: zeros)

    nsteps = seq // tm

    def following(b, c):
        lin = jnp.minimum(b * nsteps + c + 1, nb * nsteps - 1)
        return lin // nsteps, lin % nsteps

    def next_x(b, c):
        bn, cn = following(b, c)
        return bn, cn * (MIX_CHUNKS // MIX_GROUP), 0

    def next_mod(seg):
        return pl.BlockSpec((1, 1, D_MODEL), lambda b, c: (following(b, c)[0], 0, seg))

    small = (conv_w, conv_b, dt_bias, a_log, d_skip, norm_g, ln_g, ln_b, w_sp, b_sp_t)
    return pl.pallas_call(
        _mixer_kernel,
        grid=(nb, nsteps),
        in_specs=[pl.BlockSpec((1, tm, D_MODEL), lambda b, c: (b, c, 0)), whole(pre_g),
                  _mod_spec(mod, tm, MOD_SH1, 2), _mod_spec(mod, tm, MOD_SC1, 2), _mod_spec(mod, tm, MOD_G1, 2),
                  pl.BlockSpec((1, MIX_GROUP * CHUNK, D_MODEL), next_x), next_mod(MOD_SH1), next_mod(MOD_SC1),
                  whole(w_main), whole(w_dt)]
                 + [whole(a) for a in small]
                 + [whole(post_g), whole(wa), whole(wb), whole(wo)],
        out_specs=[
            pl.BlockSpec((1, tm, D_MODEL), lambda b, c: (b, c, 0)),
            pl.BlockSpec((1, SSM_HEADS, SSM_HEAD_DIM, SSM_STATE), lambda b, c: (b, 0, 0, 0)),
            pl.BlockSpec((1, CONV_WIDTH - 1, CONV_DIM), lambda b, c: (b, 0, 0)),
        ],
        out_shape=[
            jax.ShapeDtypeStruct(x.shape, F32),
            jax.ShapeDtypeStruct((nb, SSM_HEADS, SSM_HEAD_DIM, SSM_STATE), F32),
            jax.ShapeDtypeStruct((nb, CONV_WIDTH - 1, CONV_DIM), F32),
        ],
        scratch_shapes=[
            pltpu.VMEM((SSM_INNER, SSM_STATE), F32),
            pltpu.VMEM((CHUNK + 8, CONV_DIM), F32),
            pltpu.VMEM((CMLP_GROUPS, CHUNK, CHUNK), BF16),
            pltpu.VMEM((MIX_GROUP * CHUNK, PROJ_MAIN + LANES), F32),
        ],
        compiler_params=_params(("arbitrary", "arbitrary")),
        name="mixer_prompt",
    )(x, pre_g, mod, mod, mod, x, mod, mod, w_main, w_dt, *small, post_g, wa, wb, wo)


SAMPLE_TILE = 16


def _mixer_step_kernel(z_ref, u_ref, v_ref, xs_ref, bc_ref, dt_ref, ssm_in_ref, conv_in_ref,
                       cw_ref, cb_ref, dtb_ref, alog_ref, dskip_ref, ng_ref,
                       lng_ref, lnb_ref, wsp0_ref, bsp0_ref, ssm_all_ref,
                       ya_ref, yb_ref, vout_ref, ssm_ref, conv_ref):
    del ssm_all_ref
    tb = SAMPLE_TILE
    xbc_new = jnp.concatenate([xs_ref[...], bc_ref[...]], axis=1)
    hist = conv_in_ref[...]
    acc = xbc_new * cw_ref[CONV_WIDTH - 1:CONV_WIDTH, :]
    for k in range(CONV_WIDTH - 1):
        acc = acc + hist[:, k * CONV_DIM:(k + 1) * CONV_DIM] * cw_ref[k:k + 1, :]
    conv_ref[:, 0:(CONV_WIDTH - 2) * CONV_DIM] = hist[:, CONV_DIM:]
    conv_ref[:, (CONV_WIDTH - 2) * CONV_DIM:] = xbc_new

    xbc = jax.nn.silu(acc + cb_ref[...])
    xc = xbc[:, 0:SSM_INNER]
    bm = xbc[:, SSM_INNER:SSM_INNER + SSM_GROUPS * SSM_STATE]
    cm = xbc[:, SSM_INNER + SSM_GROUPS * SSM_STATE:CONV_DIM]
    dt = jax.nn.softplus(dt_ref[...] + dtb_ref[...])
    dec = jnp.exp(dt * (-jnp.exp(alog_ref[...])))

    def transposed(a):
        pad = jnp.zeros((LANES - tb, a.shape[1]), F32)
        return jnp.concatenate([a, pad], axis=0).T

    xc_t = transposed(xc)
    dt_t = transposed(dt)
    dec_t = transposed(dec)
    row_id = lax.broadcasted_iota(jnp.int32, (tb, SSM_STATE), 0)

    y_groups = [jnp.zeros((tb, GROUP_WIDTH), F32) for _ in range(SSM_GROUPS)]
    for b in range(tb):
        for g in range(SSM_GROUPS):
            b_row = bm[b:b + 1, g * SSM_STATE:(g + 1) * SSM_STATE]
            c_only = jnp.where(row_id == b, cm[:, g * SSM_STATE:(g + 1) * SSM_STATE], 0.0).astype(BF16)
            new_heads = []
            for r in range(HEADS_PER_GROUP):
                h = g * HEADS_PER_GROUP + r
                x_col = xc_t[h * SSM_HEAD_DIM:(h + 1) * SSM_HEAD_DIM, b:b + 1]
                push = x_col * dt_t[h:h + 1, b:b + 1]
                keep = jnp.broadcast_to(dec_t[h:h + 1, b:b + 1], (SSM_HEAD_DIM, SSM_STATE))
                h_new = ssm_in_ref[b, h] * keep + push * b_row
                ssm_ref[b, h] = h_new
                new_heads.append(h_new.astype(BF16))
            hg = jnp.concatenate(new_heads, axis=0)
            y_groups[g] = y_groups[g] + _dot_nt(c_only, hg)

    y = jnp.concatenate(y_groups, axis=1) + dskip_ref[...] * xc
    ya_ref[...] = _gated_group_norm(y, z_ref[...], ng_ref[...]).astype(BF16)

    ug = jax.nn.gelu(u_ref[...], approximate=True)
    vn = _layernorm(jax.nn.gelu(v_ref[...], approximate=True), lng_ref[...], lnb_ref[...])
    vout_ref[...] = vn
    yb_ref[...] = (ug * (vn * wsp0_ref[...] + bsp0_ref[...])).astype(BF16)


def _mixer_step_call(layer, proj, proj_dt, state_ssm, conv_hist, ssm_all, conv_w, conv_b, dt_bias, a_log,
                     d_skip, norm_g, ln_g, ln_b, w_sp0, b_sp0):
    nseq = proj.shape[0]
    tb = SAMPLE_TILE

    def col(block, width=D_MODEL):
        return pl.BlockSpec((tb, width), lambda i: (i, block))

    def whole(a):
        zeros = (0,) * a.ndim
        return pl.BlockSpec(a.shape, lambda i: zeros)

    small = (conv_w, conv_b, dt_bias, a_log, d_skip, norm_g, ln_g, ln_b, w_sp0, b_sp0)
    hist_width = (CONV_WIDTH - 1) * CONV_DIM
    return pl.pallas_call(
        _mixer_step_kernel,
        grid=(nseq // tb,),
        in_specs=[col(COL_Z), col(COL_U), col(COL_V), col(COL_XS),
                  col(COL_BC_512, 2 * SSM_GROUPS * SSM_STATE),
                  pl.BlockSpec((tb, LANES), lambda i: (i, 0)),
                  pl.BlockSpec((None, tb, SSM_HEADS, SSM_HEAD_DIM, SSM_STATE),
                               lambda i: (layer, i, 0, 0, 0)),
                  pl.BlockSpec((tb, hist_width), lambda i: (i, 0))]
                 + [whole(a) for a in small] + [pl.BlockSpec(memory_space=pl.ANY)],
        out_specs=[
            pl.BlockSpec((tb, SSM_INNER), lambda i: (i, 0)),
            pl.BlockSpec((tb, D_MODEL), lambda i: (i, 0)),
            pl.BlockSpec((tb, D_MODEL), lambda i: (i, 0)),
            pl.BlockSpec((None, tb, SSM_HEADS, SSM_HEAD_DIM, SSM_STATE), lambda i: (layer, i, 0, 0, 0)),
            pl.BlockSpec((tb, hist_width), lambda i: (i, 0)),
        ],
        out_shape=[
            jax.ShapeDtypeStruct((nseq, SSM_INNER), BF16),
            jax.ShapeDtypeStruct((nseq, D_MODEL), BF16),
            jax.ShapeDtypeStruct((nseq, D_MODEL), F32),
            jax.ShapeDtypeStruct(ssm_all.shape, F32),
            jax.ShapeDtypeStruct((nseq, hist_width), F32),
        ],
        input_output_aliases={8 + len(small): 3},
        compiler_params=_params(("arbitrary",)),
        name="mixer_sample",
    )(proj, proj, proj, proj, proj, proj_dt, state_ssm, conv_hist, *small, ssm_all)


def _outproj_kernel(ya_ref, yb_ref, ga_ref, gb_ref, x_ref, g1_ref, pg_ref, wa_ref, wb_ref, wo_ref, o_ref):
    o_ref[0] = _merge_project(ya_ref[0], yb_ref[0], ga_ref[0], gb_ref[0], x_ref[0], g1_ref[0], pg_ref[...],
                              wa_ref, wb_ref, wo_ref)


def _outproj_call(layer, ya, yb, proj, x, mod, post_g, wa, wb, wo, tm):
    nb, seq, _ = x.shape

    def rows(block=0):
        return pl.BlockSpec((1, tm, D_MODEL), lambda b, i: (b, i, block))

    def whole(a):
        return pl.BlockSpec(a.shape, lambda b, i: (0, 0))

    def of_layer(a):
        return pl.BlockSpec((None,) + a.shape[1:], lambda b, i: (layer, 0, 0))

    return pl.pallas_call(
        _outproj_kernel,
        grid=(nb, seq // tm),
        in_specs=[rows(), rows(), rows(COL_GA), rows(COL_GB), rows(),
                  _mod_spec(mod, tm, MOD_G1, 2), whole(post_g), of_layer(wa), of_layer(wb), of_layer(wo)],
        out_specs=rows(),
        out_shape=jax.ShapeDtypeStruct(x.shape, F32),
        compiler_params=_params(("arbitrary", "arbitrary")),
        name="out_proj",
    )(ya, yb, proj, proj, x, mod, post_g, wa, wb, wo)


FFN_CHUNK = 256


def _swiglu_into(hb, wg_ref, wu_ref, wd_ref, acc_ref):
    n_chunks = FFN_DIM // FFN_CHUNK

    def gate_up(f):
        cols = slice(f * FFN_CHUNK, (f + 1) * FFN_CHUNK)
        return _dot(hb, wg_ref[0, :, cols]), _dot(hb, wu_ref[0, :, cols])

    g, u = gate_up(0)
    for f in range(n_chunks):
        ahead = gate_up(f + 1) if f + 1 < n_chunks else None
        act = (jax.nn.silu(g) * u).astype(BF16)
        part = _dot(act, wd_ref[0, f * FFN_CHUNK:(f + 1) * FFN_CHUNK, :])
        if f == 0:
            acc_ref[...] = part
        else:
            acc_ref[...] += part
        if ahead is not None:
            g, u = ahead


def _premod(x_ref, pre_ref, sh_ref, sc_ref):
    return _rms(x_ref[0]) * pre_ref[...] * (1.0 + sc_ref[0]) + sh_ref[0]


def _dense_ffn_kernel(x_ref, pre_ref, post_ref, sh_ref, sc_ref, g2_ref, wg_ref, wu_ref, wd_ref,
                      o_ref, acc_scr):
    hb = _premod(x_ref, pre_ref, sh_ref, sc_ref).astype(BF16)
    _swiglu_into(hb, wg_ref, wu_ref, wd_ref, acc_scr)
    o_ref[0] = x_ref[0] + g2_ref[0] * (_rms(acc_scr[...]) * post_ref[...])


def _dense_ffn_call(j, x, mod, pre_g, post_g, wg, wu, wd, tm):
    nb, seq, _ = x.shape

    def whole(a):
        zeros = (0,) * a.ndim
        return pl.BlockSpec(a.shape, lambda b, i: zeros)

    def of_layer(a):
        return pl.BlockSpec((1,) + a.shape[1:], lambda b, i: (j, 0, 0))

    rows = pl.BlockSpec((1, tm, D_MODEL), lambda b, i: (b, i, 0))
    return pl.pallas_call(
        _dense_ffn_kernel,
        grid=(nb, seq // tm),
        in_specs=[rows, whole(pre_g), whole(post_g),
                  _mod_spec(mod, tm, MOD_SH2, 2), _mod_spec(mod, tm, MOD_SC2, 2),
                  _mod_spec(mod, tm, MOD_G2, 2), of_layer(wg), of_layer(wu), of_layer(wd)],
        out_specs=rows,
        out_shape=jax.ShapeDtypeStruct(x.shape, F32),
        scratch_shapes=[pltpu.VMEM((tm, D_MODEL), F32)],
        compiler_params=_params(("arbitrary", "arbitrary")),
        name="dense_ffn",
    )(x, pre_g, post_g, mod, mod, mod, wg, wu, wd)


EXPERT_TILE = 512
HALF = D_MODEL // 2
INFO_I1, INFO_I2, INFO_P1, INFO_P2, INFO_R1, INFO_R2 = range(6)
HI_MASK = 0xFFFF0000
ISSUE_UNROLL = 8


def _route_kernel(x_ref, pre_ref, sh_ref, sc_ref, rw_ref, rb_ref, cnt0_ref,
                  hw_ref, info_ref, cnt_ref, carry):
    @pl.when(jnp.logical_and(pl.program_id(0) == 0, pl.program_id(1) == 0))
    def _():
        carry[...] = cnt0_ref[...]

    hb = _premod(x_ref, pre_ref, sh_ref, sc_ref).astype(BF16)
    tm = hb.shape[0]
    bits = lax.bitcast_convert_type(hb.astype(F32), jnp.uint32)
    hw_ref[0] = (bits[:, HALF:] & jnp.uint32(HI_MASK)) | (bits[:, :HALF] >> jnp.uint32(16))

    logits = _dot(hb, rw_ref[...].astype(BF16)) + rb_ref[...]
    lane = lax.broadcasted_iota(jnp.int32, logits.shape, 1).astype(F32)
    neg = jnp.float32(-jnp.inf)
    logits = jnp.where(lane < N_EXPERTS, logits, neg)
    m1 = jnp.max(logits, axis=-1, keepdims=True)
    i1 = jnp.min(jnp.where(logits == m1, lane, float(LANES)), axis=-1, keepdims=True)
    rest = jnp.where(lane == i1, neg, logits)
    m2 = jnp.max(rest, axis=-1, keepdims=True)
    i2 = jnp.min(jnp.where(rest == m2, lane, float(LANES)), axis=-1, keepdims=True)
    e2 = jnp.exp(m2 - m1)
    p1 = 1.0 / (1.0 + e2)
    p2 = e2 * p1

    member = jnp.logical_or(lane == i1, lane == i2)
    row = lax.broadcasted_iota(jnp.int32, (tm, tm), 0)
    col = lax.broadcasted_iota(jnp.int32, (tm, tm), 1)
    before = jnp.where(row > col, 1.0, 0.0).astype(BF16)
    ones = jnp.where(member, 1.0, 0.0)
    prior = _dot(before, ones.astype(BF16)) + carry[...]
    r1 = jnp.sum(jnp.where(lane == i1, prior, 0.0), axis=-1, keepdims=True)
    r2 = jnp.sum(jnp.where(lane == i2, prior, 0.0), axis=-1, keepdims=True)
    carry[...] += jnp.sum(ones, axis=0, keepdims=True)
    cnt_ref[...] = carry[...]

    info = jnp.zeros_like(logits)
    for k, v in ((INFO_I1, i1), (INFO_I2, i2), (INFO_P1, p1), (INFO_P2, p2), (INFO_R1, r1), (INFO_R2, r2)):
        info = jnp.where(lane == float(k), v, info)
    info_ref[0] = info


def _route_call(x, mod, pre_g, rw, rb, cnt0, tm):
    nb, seq, _ = x.shape

    def whole(a):
        return pl.BlockSpec(a.shape, lambda b, i: (0, 0))

    return pl.pallas_call(
        _route_kernel,
        grid=(nb, seq // tm),
        in_specs=[pl.BlockSpec((1, tm, D_MODEL), lambda b, i: (b, i, 0)), whole(pre_g),
                  _mod_spec(mod, tm, MOD_SH2, 2), _mod_spec(mod, tm, MOD_SC2, 2),
                  whole(rw), whole(rb), whole(cnt0)],
        out_specs=[pl.BlockSpec((1, tm, HALF), lambda b, i: (b, i, 0)),
                   pl.BlockSpec((1, tm, LANES), lambda b, i: (b, i, 0)),
                   pl.BlockSpec((1, LANES), lambda b, i: (0, 0))],
        out_shape=[jax.ShapeDtypeStruct((nb, seq, HALF), jnp.uint32),
                   jax.ShapeDtypeStruct((nb, seq, LANES), F32),
                   jax.ShapeDtypeStruct((1, LANES), F32)],
        scratch_shapes=[pltpu.VMEM((1, LANES), F32)],
        compiler_params=_params(("arbitrary", "arbitrary")),
        name="moe_route",
    )(x, pre_g, mod, mod, rw, rb, cnt0)


def _row_copy(src, src_row, dst, dst_row, sem):
    return pltpu.make_async_copy(src.at[pl.ds(src_row, 1)], dst.at[pl.ds(dst_row, 1)], sem)


def _dispatch_kernel(d1_ref, d2_ref, hw_ref, xs_in_ref, xs_ref, sem):
    del xs_in_ref
    tm = hw_ref.shape[0]
    base = pl.program_id(0) * tm

    def issue(r, carry):
        _row_copy(hw_ref, r, xs_ref, d1_ref[base + r], sem).start(priority=0)
        _row_copy(hw_ref, r, xs_ref, d2_ref[base + r], sem).start(priority=1)
        return carry

    lax.fori_loop(0, tm, issue, 0, unroll=ISSUE_UNROLL)
    for _ in range(2):
        pltpu.make_async_copy(hw_ref, xs_ref.at[pl.ds(0, tm)], sem).wait()


def _dispatch_call(dest1, dest2, hw, xs, tm):
    rows = hw.shape[0]
    return pl.pallas_call(
        _dispatch_kernel,
        grid_spec=pltpu.PrefetchScalarGridSpec(
            num_scalar_prefetch=2,
            grid=(rows // tm,),
            in_specs=[pl.BlockSpec((tm, HALF), lambda i, d1, d2: (i, 0)),
                      pl.BlockSpec(memory_space=pl.ANY)],
            out_specs=pl.BlockSpec(memory_space=pl.ANY),
            scratch_shapes=[pltpu.SemaphoreType.DMA(())],
        ),
        out_shape=jax.ShapeDtypeStruct(xs.shape, xs.dtype),
        input_output_aliases={3: 0},
        compiler_params=_params(("arbitrary",), row_dma=True),
        name="moe_dispatch",
    )(dest1, dest2, hw, xs)


def _experts_kernel(exp_ref, nv_ref, xs_ref, wg_ref, wu_ref, wd_ref, y_ref):
    del exp_ref
    used = pl.program_id(0) < nv_ref[0]

    @pl.when(used)
    def _():
        words = xs_ref[...]
        lo = lax.bitcast_convert_type(words << jnp.uint32(16), F32)
        hi = lax.bitcast_convert_type(words & jnp.uint32(HI_MASK), F32)
        hb = jnp.concatenate([lo, hi], axis=1).astype(BF16)
        _swiglu_into(hb, wg_ref, wu_ref, wd_ref, y_ref)

    @pl.when(jnp.logical_not(used))
    def _():
        y_ref[...] = jnp.zeros_like(y_ref)


def _experts_call(j, tile_exp, n_valid, xs, wg, wu, wd):
    n_tiles = tile_exp.shape[0]
    tm = EXPERT_TILE
    weight = lambda shape: pl.BlockSpec((None, 1) + shape, lambda i, ex, nv: (j, ex[i], 0, 0))
    return pl.pallas_call(
        _experts_kernel,
        grid_spec=pltpu.PrefetchScalarGridSpec(
            num_scalar_prefetch=2,
            grid=(n_tiles,),
            in_specs=[pl.BlockSpec((tm, HALF), lambda i, ex, nv: (i, 0)),
                      weight((D_MODEL, FFN_DIM)), weight((D_MODEL, FFN_DIM)), weight((FFN_DIM, D_MODEL))],
            out_specs=pl.BlockSpec((tm, D_MODEL), lambda i, ex, nv: (i, 0)),
        ),
        out_shape=jax.ShapeDtypeStruct((xs.shape[0], D_MODEL), F32),
        compiler_params=_params(("arbitrary",)),
        name="moe_experts",
    )(tile_exp, n_valid, xs, wg, wu, wd)


def _combine_kernel(d1_ref, d2_ref, y_ref, info_ref, x_ref, g2_ref, post_ref, o_ref, ybuf, sems):
    tm = x_ref.shape[1]
    step = pl.program_id(0) * pl.num_programs(1) + pl.program_id(1)
    n_steps = pl.num_programs(0) * pl.num_programs(1)

    def gather(tile, slot):
        base = tile * tm

        def issue(r, carry):
            _row_copy(y_ref, d1_ref[base + r], ybuf.at[slot, 0], r, sems.at[slot]).start(priority=0)
            _row_copy(y_ref, d2_ref[base + r], ybuf.at[slot, 1], r, sems.at[slot]).start(priority=1)
            return carry

        lax.fori_loop(0, tm, issue, 0, unroll=ISSUE_UNROLL)

    slot = step % 2

    @pl.when(step == 0)
    def _():
        gather(step, slot)

    @pl.when(step + 1 < n_steps)
    def _():
        gather(step + 1, 1 - slot)

    for k in range(2):
        pltpu.make_async_copy(y_ref.at[pl.ds(0, tm)], ybuf.at[slot, k], sems.at[slot]).wait()

    info = info_ref[0]
    f = info[:, INFO_P1:INFO_P1 + 1] * ybuf[slot, 0] + info[:, INFO_P2:INFO_P2 + 1] * ybuf[slot, 1]
    o_ref[0] = x_ref[0] + g2_ref[0] * (_rms(f) * post_ref[...])


def _combine_call(dest1, dest2, y, info, x, mod, post_g, tm):
    nb, seq, _ = x.shape
    per_row = mod.shape[1] != 1
    return pl.pallas_call(
        _combine_kernel,
        grid_spec=pltpu.PrefetchScalarGridSpec(
            num_scalar_prefetch=2,
            grid=(nb, seq // tm),
            in_specs=[pl.BlockSpec(memory_space=pl.ANY),
                      pl.BlockSpec((1, tm, LANES), lambda b, i, d1, d2: (b, i, 0)),
                      pl.BlockSpec((1, tm, D_MODEL), lambda b, i, d1, d2: (b, i, 0)),
                      pl.BlockSpec((1, tm if per_row else 1, D_MODEL),
                                   lambda b, i, d1, d2: (b, i if per_row else 0, MOD_G2)),
                      pl.BlockSpec(post_g.shape, lambda b, i, d1, d2: (0, 0))],
            out_specs=pl.BlockSpec((1, tm, D_MODEL), lambda b, i, d1, d2: (b, i, 0)),
            scratch_shapes=[pltpu.VMEM((2, 2, tm, D_MODEL), F32), pltpu.SemaphoreType.DMA((2,))],
        ),
        out_shape=jax.ShapeDtypeStruct(x.shape, F32),
        compiler_params=_params(("arbitrary", "arbitrary"), row_dma=True),
        name="moe_combine",
    )(dest1, dest2, y, info, x, mod, post_g)


def _moe_layer(j, xp, xs, mod_p, mod_s, pre_g, post_g, router_w, router_b, wg, wu, wd):
    rw = jnp.pad(router_w, ((0, 0), (0, LANES - N_EXPERTS)))
    rb = _pad_lanes(router_b)
    hw_p, info_p, cnt_p = _route_call(xp, mod_p, pre_g, rw, rb, jnp.zeros((1, LANES), F32), tm=512)
    hw_s, info_s, cnt = _route_call(xs, mod_s, pre_g, rw, rb, cnt_p, tm=xs.shape[1])

    tm = EXPERT_TILE
    n_assign = 2 * (xp.shape[0] * xp.shape[1] + xs.shape[1])
    n_tiles = n_assign // tm + N_EXPERTS
    counts = cnt[0, :N_EXPERTS].astype(jnp.int32)
    group_tiles = (counts + tm - 1) // tm
    tile_end = jnp.cumsum(group_tiles)
    start = (tile_end - group_tiles) * tm
    n_valid = tile_end[-1]
    tile_id = jnp.minimum(jnp.arange(n_tiles, dtype=jnp.int32), n_valid - 1)
    tile_exp = jnp.sum(tile_id[:, None] >= tile_end[None, :], axis=1).astype(jnp.int32)

    def dests(info):
        flat = info.reshape(-1, LANES)
        d = [start[flat[:, i].astype(jnp.int32)] + flat[:, r].astype(jnp.int32)
             for i, r in ((INFO_I1, INFO_R1), (INFO_I2, INFO_R2))]
        return d[0], d[1]

    dp = dests(info_p)
    ds = dests(info_s)
    slots = jnp.zeros((n_tiles * tm, HALF), jnp.uint32)
    slots = _dispatch_call(dp[0], dp[1], hw_p.reshape(-1, HALF), slots, tm=1024)
    slots = _dispatch_call(ds[0], ds[1], hw_s.reshape(-1, HALF), slots, tm=xs.shape[1])
    y = _experts_call(j, tile_exp, n_valid.reshape(1), slots, wg, wu, wd)
    xp = _combine_call(dp[0], dp[1], y, info_p, xp, mod_p, post_g, tm=512)
    xs = _combine_call(ds[0], ds[1], y, info_s, xs, mod_s, post_g, tm=xs.shape[1])
    return xp, xs


def _pad_lanes(v):
    return jnp.pad(v, (0, LANES - v.shape[0])).reshape(1, LANES)


def kernel(x_prompt, x_sample, state_ssm, state_conv, c_prompt, c_sample, w_mod, b_mod, mix_pre_g, mix_post_g, ffn_pre_g, ffn_post_g, w_in, conv_w, conv_b, dt_bias, a_log, d_skip, ssm_norm_g, w_ssd_out, cmlp_ln_g, cmlp_ln_b, w_spatial, b_spatial, w_cmlp_out, w_o, ffn_wg, ffn_wu, ffn_wd, router_w, router_b, exp_wg, exp_wu, exp_wd):
    n_prompt = x_prompt.shape[0]
    n_sample = x_sample.shape[0]

    c_all = jnp.concatenate([c_prompt, c_sample, jnp.zeros((8, D_MODEL), F32)], axis=0)
    mod_all = _mod_call(c_all, w_mod, b_mod)
    mod_p = mod_all[:, :n_prompt].reshape(DEPTH, n_prompt, 1, 6 * D_MODEL)
    mod_s = mod_all[:, n_prompt:n_prompt + n_sample].reshape(DEPTH, 1, n_sample, 6 * D_MODEL)

    xp = x_prompt
    xs = x_sample.reshape(1, n_sample, D_MODEL)
    conv_hist = state_conv.reshape(DEPTH, n_sample, (CONV_WIDTH - 1) * CONV_DIM)

    o_xbc = SSM_INNER
    o_dt = o_xbc + CONV_DIM
    o_u = o_dt + SSM_HEADS

    w_main = jnp.concatenate([w_in[:, :, :o_xbc], w_in[:, :, o_u:], w_in[:, :, o_xbc:o_dt]], axis=2).astype(BF16)
    w_dt = jnp.pad(w_in[:, :, o_dt:o_u], ((0, 0), (0, 0), (0, LANES - SSM_HEADS))).astype(BF16)
    wa, wb, wo = (a.astype(BF16) for a in (w_ssd_out, w_cmlp_out, w_o))
    dense_w = tuple(a.astype(BF16) for a in (ffn_wg, ffn_wu, ffn_wd))
    expert_w = tuple(a.astype(BF16) for a in (exp_wg, exp_wu, exp_wd))

    ssm_s = jnp.zeros(state_ssm.shape, F32)
    ssm_p, conv_p, conv_s, v_s = [], [], [], []
    for l in range(DEPTH):
        row = lambda a: a[l].reshape(1, -1)
        mixer_small = (conv_w[l], row(conv_b), _pad_lanes(dt_bias[l]), _pad_lanes(a_log[l]),
                       jnp.repeat(d_skip[l], SSM_HEAD_DIM).reshape(1, -1), row(ssm_norm_g),
                       row(cmlp_ln_g), row(cmlp_ln_b))

        proj_s, dt_s = _inproj_call(l, xs, mod_s[l], row(mix_pre_g), w_main, w_dt, tm=n_sample)
        xp, hs_p, cs_p = _mixer_call(xp, mod_p[l], row(mix_pre_g), row(mix_post_g), w_main[l], w_dt[l],
                                     wa[l], wb[l], wo[l], *mixer_small, w_spatial[l], b_spatial[l].T)
        ya_s, yb_s, vr_s, ssm_s, cs_s = _mixer_step_call(
            l, proj_s[0], dt_s[0], state_ssm, conv_hist[l], ssm_s, *mixer_small,
            jnp.repeat(w_spatial[l, :, 0, 0], CMLP_GROUP_DIM).reshape(1, -1),
            jnp.repeat(b_spatial[l, :, 0], CMLP_GROUP_DIM).reshape(1, -1))
        xs = _outproj_call(l, ya_s[None], yb_s[None], proj_s, xs, mod_s[l], row(mix_post_g), wa, wb, wo,
                           tm=n_sample)

        j = l // 2
        if l % 2 == 0:
            xp = _dense_ffn_call(j, xp, mod_p[l], row(ffn_pre_g), row(ffn_post_g), *dense_w, tm=512)
            xs = _dense_ffn_call(j, xs, mod_s[l], row(ffn_pre_g), row(ffn_post_g), *dense_w, tm=n_sample)
        else:
            xp, xs = _moe_layer(j, xp, xs, mod_p[l], mod_s[l], row(ffn_pre_g), row(ffn_post_g),
                                router_w[j], router_b[j], *expert_w)

        ssm_p.append(hs_p)
        conv_p.append(cs_p)
        conv_s.append(cs_s.reshape(n_sample, CONV_WIDTH - 1, CONV_DIM))
        v_s.append(vr_s.reshape(n_sample, 1, D_MODEL))

    return (xp, xs.reshape(n_sample, 1, D_MODEL), jnp.stack(ssm_p), jnp.stack(conv_p),
            ssm_s, jnp.stack(conv_s), jnp.stack(v_s))
```
